```python
import jax, jax.numpy as jnp
from jax import lax
import numpy as np

D_MODEL = 2048
BATCH = 8
SEQ = 2048
DEPTH = 1

CHUNK = 64
Q_BLOCK = 128
N_MEM = 256
EPS = 1e-6

MLA_HEADS = 8
MLA_NOPE = 128
MLA_ROPE = 64
MLA_QK = MLA_NOPE + MLA_ROPE
MLA_V = 128
MLA_Q_RANK = 512
MLA_KV_RANK = 256
ROPE_THETA = 10000.0

GLA_HEADS = 4
GLA_DK = 128
GLA_DV = 256
GLA_GATE_RANK = 16
GLA_TAU = 16.0

MIX_WIDTH = MLA_HEADS * MLA_V + GLA_HEADS * GLA_DV

MEM_HEADS = 4
MEM_HEAD_DIM = 128
MEM_WIDTH = MEM_HEADS * MEM_HEAD_DIM

D_FF = 5632

IN_SIZES = [
    MLA_Q_RANK,
    MLA_KV_RANK,
    MLA_ROPE,
    GLA_HEADS * GLA_DK,
    GLA_HEADS * GLA_DK,
    GLA_HEADS * GLA_DV,
    GLA_GATE_RANK,
    GLA_HEADS * GLA_DV,
]
IN_WIDTH = int(sum(IN_SIZES))
IN_SPLITS = [int(s) for s in np.cumsum(IN_SIZES)[:-1]]

kernel_name = "hybrid_mla_gla_macaron_memory_block"


def rmsnorm(x, g):
    xf = x.astype(jnp.float32)
    y = xf * lax.rsqrt(jnp.mean(xf * xf, axis=-1, keepdims=True) + EPS)
    return (y * g.astype(jnp.float32)).astype(x.dtype)


def swiglu(h, w_gate, w_up, w_down):
    return (jax.nn.silu(h @ w_gate) * (h @ w_up)) @ w_down


def rope(x, positions):
    half = x.shape[-1] // 2
    inv_freq = ROPE_THETA ** (-jnp.arange(half, dtype=jnp.float32) / half)
    ang = positions.astype(jnp.float32)[..., None] * inv_freq
    cos = jnp.cos(ang)[:, :, None, :]
    sin = jnp.sin(ang)[:, :, None, :]
    xf = x.astype(jnp.float32)
    x1, x2 = xf[..., :half], xf[..., half:]
    return jnp.concatenate([x1 * cos - x2 * sin, x2 * cos + x1 * sin], axis=-1).astype(x.dtype)


def chunk_causal_attention(q, k, v):
    B, S, H, Dk = q.shape
    Dv = v.shape[-1]
    n_blk = S // Q_BLOCK
    scale = Dk ** -0.5
    k_chunk = jnp.arange(S) // CHUNK
    q_blocks = q.reshape(B, n_blk, Q_BLOCK, H, Dk).transpose(1, 0, 2, 3, 4)

    def one_block(args):
        q_blk, blk = args
        s = jnp.einsum('bqhd,bkhd->bhqk', q_blk, k).astype(jnp.float32) * scale
        q_chunk = (blk * Q_BLOCK + jnp.arange(Q_BLOCK)) // CHUNK
        mask = k_chunk[None, :] <= q_chunk[:, None]
        s = jnp.where(mask[None, None], s, -jnp.inf)
        p = jax.nn.softmax(s, axis=-1).astype(v.dtype)
        return jnp.einsum('bhqk,bkhd->bqhd', p, v)

    out = lax.map(one_block, (q_blocks, jnp.arange(n_blk)))
    return out.transpose(1, 0, 2, 3, 4).reshape(B, S, H, Dv)


def gla_chunked(q, k, v, log_a):
    B, S, H, K = q.shape
    V = v.shape[-1]
    n_chunk = S // CHUNK
    f32 = jnp.float32
    qc = q.astype(f32).reshape(B, n_chunk, CHUNK, H, K) * (K ** -0.5)
    kc = k.astype(f32).reshape(B, n_chunk, CHUNK, H, K)
    vc = v.astype(f32).reshape(B, n_chunk, CHUNK, H, V)
    g = log_a.astype(f32).reshape(B, n_chunk, CHUNK, H, K)
    b = jnp.cumsum(g, axis=2)
    b_end = b[:, :, -1]
    k_dec = kc * jnp.exp(b_end[:, :, None] - b)
    u = jnp.einsum('bnchk,bnchv->bnhkv', k_dec, vc)
    decay = jnp.exp(b_end)

    def step(state, inp):
        d, uc = inp
        state = d[..., None] * state + uc
        return state, state

    s0 = jnp.zeros((B, H, K, V), f32)
    _, states = lax.scan(step, s0, (decay.transpose(1, 0, 2, 3), u.transpose(1, 0, 2, 3, 4)))
    states = states.transpose(1, 0, 2, 3, 4)
    o = jnp.einsum('bnchk,bnhkv->bnchv', qc, states)
    return o.reshape(B, S, H, V).astype(v.dtype)


def memory_cross_attention(h, m, w_q, w_k, w_v, w_o, g_q, g_k):
    B, S, _ = h.shape
    M = m.shape[1]
    q = rmsnorm((h @ w_q).reshape(B, S, MEM_HEADS, MEM_HEAD_DIM), g_q)
    k = rmsnorm((m @ w_k).reshape(B, M, MEM_HEADS, MEM_HEAD_DIM), g_k)
    v = (m @ w_v).reshape(B, M, MEM_HEADS, MEM_HEAD_DIM)
    s = jnp.einsum('bqhd,bkhd->bhqk', q, k).astype(jnp.float32) * (MEM_HEAD_DIM ** -0.5)
    p = jax.nn.softmax(s, axis=-1).astype(v.dtype)
    o = jnp.einsum('bhqk,bkhd->bqhd', p, v).reshape(B, S, MEM_WIDTH)
    return o @ w_o


def setup_inputs(seed: int = 0) -> dict:
    key = jax.random.key(seed)
    keys = iter(jax.random.split(key, 40))
    f32 = jnp.float32

    def w(fan_in, fan_out):
        return jax.random.normal(next(keys), (DEPTH, fan_in, fan_out), f32) * fan_in ** -0.5

    def g(n):
        return 1.0 + 0.02 * jax.random.normal(next(keys), (DEPTH, n), f32)

    x = jax.random.normal(next(keys), (BATCH, SEQ, D_MODEL), f32)
    mem = jax.random.normal(next(keys), (BATCH, N_MEM, D_MODEL), f32)
    offset = jax.random.randint(next(keys), (BATCH, 1), 0, 64, dtype=jnp.int32) * CHUNK
    positions = (offset + jnp.arange(SEQ, dtype=jnp.int32)[None, :]).astype(jnp.int32)

    return {
        "x": x,
        "mem": mem,
        "positions": positions,
        "ffn1_norm": g(D_MODEL),
        "ffn1_w_gate": w(D_MODEL, D_FF),
        "ffn1_w_up": w(D_MODEL, D_FF),
        "ffn1_w_down": w(D_FF, D_MODEL),
        "mix_norm": g(D_MODEL),
        "w_in": w(D_MODEL, IN_WIDTH),
        "q_a_norm": g(MLA_Q_RANK),
        "w_q_up": w(MLA_Q_RANK, MLA_HEADS * MLA_QK),
        "kv_a_norm": g(MLA_KV_RANK),
        "w_kv_up": w(MLA_KV_RANK, MLA_HEADS * (MLA_NOPE + MLA_V)),
        "mla_q_norm": g(MLA_QK),
        "mla_k_norm": g(MLA_QK),
        "gla_w_gate2": w(GLA_GATE_RANK, GLA_HEADS * GLA_DK),
        "gla_b_gate": 0.1 * jax.random.normal(next(keys), (DEPTH, GLA_HEADS * GLA_DK), f32),
        "gla_out_norm": g(GLA_DV),
        "w_out": w(MIX_WIDTH, D_MODEL),
        "mem_attn_norm": g(D_MODEL),
        "mem_norm": g(D_MODEL),
        "mem_w_q": w(D_MODEL, MEM_WIDTH),
        "mem_w_k": w(D_MODEL, MEM_WIDTH),
        "mem_w_v": w(D_MODEL, MEM_WIDTH),
        "mem_w_o": w(MEM_WIDTH, D_MODEL),
        "mem_q_norm": g(MEM_HEAD_DIM),
        "mem_k_norm": g(MEM_HEAD_DIM),
        "ffn2_norm": g(D_MODEL),
        "ffn2_w_gate": w(D_MODEL, D_FF),
        "ffn2_w_up": w(D_MODEL, D_FF),
        "ffn2_w_down": w(D_FF, D_MODEL),
    }


def reference(x, mem, positions, ffn1_norm, ffn1_w_gate, ffn1_w_up, ffn1_w_down,
              mix_norm, w_in, q_a_norm, w_q_up, kv_a_norm, w_kv_up, mla_q_norm,
              mla_k_norm, gla_w_gate2, gla_b_gate, gla_out_norm, w_out,
              mem_attn_norm, mem_norm, mem_w_q, mem_w_k, mem_w_v, mem_w_o,
              mem_q_norm, mem_k_norm, ffn2_norm, ffn2_w_gate, ffn2_w_up, ffn2_w_down):
    B, S, _ = x.shape
    for l in range(DEPTH):
        x = x + 0.5 * swiglu(rmsnorm(x, ffn1_norm[l]), ffn1_w_gate[l], ffn1_w_up[l], ffn1_w_down[l])

        h = rmsnorm(x, mix_norm[l])
        z = h @ w_in[l]
        zq, zkv, zkr, gq, gk, gv, zg, zr = jnp.split(z, IN_SPLITS, axis=-1)

        q = (rmsnorm(zq, q_a_norm[l]) @ w_q_up[l]).reshape(B, S, MLA_HEADS, MLA_QK)
        kv = (rmsnorm(zkv, kv_a_norm[l]) @ w_kv_up[l]).reshape(B, S, MLA_HEADS, MLA_NOPE + MLA_V)
        k_nope, v = kv[..., :MLA_NOPE], kv[..., MLA_NOPE:]
        k_rope = jnp.broadcast_to(zkr[:, :, None, :], (B, S, MLA_HEADS, MLA_ROPE))
        k = jnp.concatenate([k_nope, k_rope], axis=-1)
        q = rmsnorm(q, mla_q_norm[l])
        k = rmsnorm(k, mla_k_norm[l])
        q = jnp.concatenate([q[..., :MLA_NOPE], rope(q[..., MLA_NOPE:], positions)], axis=-1)
        k = jnp.concatenate([k[..., :MLA_NOPE], rope(k[..., MLA_NOPE:], positions)], axis=-1)
        o_mla = chunk_causal_attention(q, k, v).reshape(B, S, MLA_HEADS * MLA_V)

        log_a = jax.nn.log_sigmoid((zg @ gla_w_gate2[l] + gla_b_gate[l]).astype(jnp.float32)) / GLA_TAU
        o_gla = gla_chunked(gq.reshape(B, S, GLA_HEADS, GLA_DK),
                            gk.reshape(B, S, GLA_HEADS, GLA_DK),
                            gv.reshape(B, S, GLA_HEADS, GLA_DV),
                            log_a.reshape(B, S, GLA_HEADS, GLA_DK))
        o_gla = rmsnorm(o_gla, gla_out_norm[l]).reshape(B, S, GLA_HEADS * GLA_DV) * jax.nn.silu(zr)

        x = x + jnp.concatenate([o_mla, o_gla], axis=-1) @ w_out[l]

        x = x + memory_cross_attention(rmsnorm(x, mem_attn_norm[l]), rmsnorm(mem, mem_norm[l]),
                                       mem_w_q[l], mem_w_k[l], mem_w_v[l], mem_w_o[l],
                                       mem_q_norm[l], mem_k_norm[l])

        x = x + 0.5 * swiglu(rmsnorm(x, ffn2_norm[l]), ffn2_w_gate[l], ffn2_w_up[l], ffn2_w_down[l])
    return x
```

```python
import functools

import jax
import jax.numpy as jnp
from jax import lax
from jax.experimental import pallas as pl
from jax.experimental.pallas import tpu as pltpu

F32 = jnp.float32
BF16 = jnp.bfloat16

EPS = 1e-6
CHUNK = 64
ROPE_THETA = 10000.0
GLA_TAU = 16.0

MLA_HEADS = 8
MLA_NOPE = 128
MLA_ROPE = 64
MLA_QK = MLA_NOPE + MLA_ROPE
MLA_V = 128
MLA_Q_RANK = 512
MLA_KV_RANK = 256
GLA_HEADS = 4
GLA_DK = 128
GLA_DV = 256
GLA_GATE_RANK = 16
MEM_HEADS = 4
MEM_HEAD_DIM = 128

LANES = 128
HEAD_PAD = 2 * LANES
HALF_ROPE = MLA_ROPE // 2

_C_ZQ = 0
_C_ZKV = _C_ZQ + MLA_Q_RANK
_C_ZKR = _C_ZKV + MLA_KV_RANK
_C_GQ = _C_ZKR + LANES
_C_GK = _C_GQ + GLA_HEADS * GLA_DK
_C_GV = _C_GK + GLA_HEADS * GLA_DK
_C_ZR = _C_GV + GLA_HEADS * GLA_DV
_C_ZG = _C_ZR + GLA_HEADS * GLA_DV
_C_END = _C_ZG + LANES

V7X_VMEM_LIMIT = 56 * 1024 * 1024

FFN_TM = 512
FFN_TF = 512
PROJ_TM = 512
ATTN_TQ = 256
OUT_TM = 512


def _rms(x, g):
    ms = jnp.mean(x * x, axis=-1, keepdims=True)
    return x * lax.rsqrt(ms + EPS) * g


def _dot(a, b):
    return jnp.dot(a, b, preferred_element_type=F32)


def _dot_nt(a, b):
    return lax.dot_general(a, b, (((1,), (1,)), ((), ())), preferred_element_type=F32)


def _resident(shape):
    nd = len(shape)
    return pl.BlockSpec(shape, lambda *_: (0,) * nd, pipeline_mode=pl.Buffered(1))


def _ffn_body(x_ref, g_ref, wg_ref, wu_ref, wd_ref, o_ref, h_ref):
    @pl.when(pl.program_id(1) == 0)
    def _():
        x = x_ref[...]
        h_ref[...] = _rms(x, g_ref[...]).astype(BF16)
        o_ref[...] = x

    h = h_ref[...]
    gate = _dot(h, wg_ref[...])
    up = _dot(h, wu_ref[...])
    a = ((0.5 * gate) * jax.nn.sigmoid(gate) * up).astype(BF16)
    o_ref[...] += _dot(a, wd_ref[...])


def _ffn(x, g, wg, wu, wd):
    m, d = x.shape
    f = wg.shape[1]
    return pl.pallas_call(
        _ffn_body,
        grid=(m // FFN_TM, f // FFN_TF),
        in_specs=[
            pl.BlockSpec((FFN_TM, d), lambda i, j: (i, 0)),
            pl.BlockSpec((1, d), lambda i, j: (0, 0)),
            pl.BlockSpec((d, FFN_TF), lambda i, j: (0, j)),
            pl.BlockSpec((d, FFN_TF), lambda i, j: (0, j)),
            pl.BlockSpec((FFN_TF, d), lambda i, j: (j, 0)),
        ],
        out_specs=pl.BlockSpec((FFN_TM, d), lambda i, j: (i, 0)),
        out_shape=jax.ShapeDtypeStruct((m, d), F32),
        scratch_shapes=[pltpu.VMEM((FFN_TM, d), BF16)],
        compiler_params=pltpu.CompilerParams(
            dimension_semantics=("parallel", "arbitrary"),
            vmem_limit_bytes=V7X_VMEM_LIMIT),
        name="ffn",
    )(x, g, wg, wu, wd)


def _mixer_proj_body(x_ref, pos_ref, gmix_ref, win_ref, gqa_ref, wq_ref, gkva_ref, wkv_ref,
                     gq_ref, gk_ref, invf_ref, w2_ref, b2_ref,
                     q_out, k_out, v_out, gq_out, gk_out, gv_out, la_out, zr_out):
    h = _rms(x_ref[...], gmix_ref[...]).astype(BF16)

    def proj(a, b):
        return _dot(h, win_ref[:, a:b])

    ang = pos_ref[...].astype(F32) * invf_ref[...]
    lane = lax.broadcasted_iota(jnp.int32, (1, LANES), 1)
    cos_t = jnp.cos(ang)
    sin_t = jnp.sin(ang) * jnp.where(lane < LANES // 2, -1.0, 1.0)

    def rope(v):
        return v * cos_t + pltpu.roll(v, LANES // 2, axis=1) * sin_t

    inv_qk = 1.0 / MLA_QK

    qa = _rms(proj(_C_ZQ, _C_ZKV), gqa_ref[...]).astype(BF16)
    qf = _dot(qa, wq_ref[...])
    for hd in range(MLA_HEADS):
        c = hd * HEAD_PAD
        blk = qf[:, c:c + HEAD_PAD]
        ss = jnp.sum(blk * blk, axis=-1, keepdims=True)
        rinv = lax.rsqrt(ss * inv_qk + EPS) * (MLA_QK ** -0.5)
        q_out[:, c:c + LANES] = (blk[:, :LANES] * rinv * gq_ref[:, :LANES]).astype(BF16)
        q_out[:, c + LANES:c + HEAD_PAD] = rope(
            blk[:, LANES:] * rinv * gq_ref[:, LANES:]).astype(BF16)

    kva = _rms(proj(_C_ZKV, _C_ZKR), gkva_ref[...]).astype(BF16)
    kvf = _dot(kva, wkv_ref[...])
    zkr = proj(_C_ZKR, _C_GQ)
    ssr = jnp.sum(zkr * zkr, axis=-1, keepdims=True)
    kr_base = rope(zkr * gk_ref[:, LANES:])
    for hd in range(MLA_HEADS):
        kn = kvf[:, hd * MLA_NOPE:(hd + 1) * MLA_NOPE]
        ss = jnp.sum(kn * kn, axis=-1, keepdims=True) + ssr
        rinv = lax.rsqrt(ss * inv_qk + EPS)
        c = hd * HEAD_PAD
        k_out[:, c:c + LANES] = (kn * rinv * gk_ref[:, :LANES]).astype(BF16)
        k_out[:, c + LANES:c + HEAD_PAD] = (kr_base * rinv).astype(BF16)
    v_out[...] = kvf[:, MLA_HEADS * MLA_NOPE:].astype(BF16)

    gq_out[...] = (proj(_C_GQ, _C_GK) * (GLA_DK ** -0.5)).astype(BF16)
    gk_out[...] = proj(_C_GK, _C_GV)
    gv_out[...] = proj(_C_GV, _C_ZR).astype(BF16)
    zr_out[...] = proj(_C_ZR, _C_ZG).astype(BF16)
    zg = proj(_C_ZG, _C_END).astype(BF16)
    pre = _dot(zg, w2_ref[...]) + b2_ref[...]
    log_sig = jnp.minimum(pre, 0.0) - jnp.log1p(jnp.exp(-jnp.abs(pre)))
    la_out[...] = log_sig * (1.0 / GLA_TAU)


def _mixer_proj(x, pos, gmix, win, gqa, wq, gkva, wkv, gq, gk, invf, w2, b2):
    m, d = x.shape
    tm = PROJ_TM
    row = lambda i: (i, 0)
    widths = [
        (MLA_HEADS * HEAD_PAD, BF16),
        (MLA_HEADS * HEAD_PAD, BF16),
        (MLA_HEADS * MLA_V, BF16),
        (GLA_HEADS * GLA_DK, BF16),
        (GLA_HEADS * GLA_DK, F32),
        (GLA_HEADS * GLA_DV, BF16),
        (GLA_HEADS * GLA_DK, F32),
        (GLA_HEADS * GLA_DV, BF16),
    ]
    return pl.pallas_call(
        _mixer_proj_body,
        grid=(m // tm,),
        in_specs=[
            pl.BlockSpec((tm, d), row),
            pl.BlockSpec((tm, 1), row),
            _resident(gmix.shape), _resident(win.shape), _resident(gqa.shape),
            _resident(wq.shape), _resident(gkva.shape), _resident(wkv.shape),
            _resident(gq.shape), _resident(gk.shape), _resident(invf.shape),
            _resident(w2.shape), _resident(b2.shape),
        ],
        out_specs=[pl.BlockSpec((tm, w), row) for w, _ in widths],
        out_shape=[jax.ShapeDtypeStruct((m, w), dt) for w, dt in widths],
        compiler_params=pltpu.CompilerParams(
            dimension_semantics=("parallel",), vmem_limit_bytes=V7X_VMEM_LIMIT),
        name="mixer_proj",
    )(x, pos, gmix, win, gqa, wq, gkva, wkv, gq, gk, invf, w2, b2)


def _mla_attn_body(q_ref, k_ref, v_ref, o_ref):
    seq = q_ref.shape[1]
    tq = ATTN_TQ
    r = lax.broadcasted_iota(jnp.int32, (tq, tq), 0) // CHUNK
    c = lax.broadcasted_iota(jnp.int32, (tq, tq), 1) // CHUNK
    diag_mask = c <= r
    for i in range(seq // tq):
        lo = i * tq
        q = q_ref[0, lo:lo + tq, :]
        s_d = jnp.where(diag_mask, _dot_nt(q, k_ref[0, lo:lo + tq, :]), -jnp.inf)
        m = jnp.max(s_d, axis=-1, keepdims=True)
        if i > 0:
            s_o = _dot_nt(q, k_ref[0, :lo, :])
            m = jnp.maximum(m, jnp.max(s_o, axis=-1, keepdims=True))
        p_d = jnp.exp(s_d - m)
        l = jnp.sum(p_d, axis=-1, keepdims=True)
        acc = _dot(p_d.astype(BF16), v_ref[0, lo:lo + tq, :])
        if i > 0:
            p_o = jnp.exp(s_o - m)
            l = l + jnp.sum(p_o, axis=-1, keepdims=True)
            acc = acc + _dot(p_o.astype(BF16), v_ref[0, :lo, :])
        o_ref[0, lo:lo + tq, :] = (acc * (1.0 / l)).astype(o_ref.dtype)


def _mla_attn(q, k, v):
    b, s, _ = q.shape
    return pl.pallas_call(
        _mla_attn_body,
        grid=(b, MLA_HEADS),
        in_specs=[
            pl.BlockSpec((1, s, HEAD_PAD), lambda i, j: (i, 0, j)),
            pl.BlockSpec((1, s, HEAD_PAD), lambda i, j: (i, 0, j)),
            pl.BlockSpec((1, s, MLA_V), lambda i, j: (i, 0, j)),
        ],
        out_specs=pl.BlockSpec((1, s, MLA_V), lambda i, j: (i, 0, j)),
        out_shape=jax.ShapeDtypeStruct((b, s, MLA_HEADS * MLA_V), BF16),
        compiler_params=pltpu.CompilerParams(
            dimension_semantics=("parallel", "parallel"), vmem_limit_bytes=V7X_VMEM_LIMIT),
        name="mla_attn",
    )(q, k, v)


def _gla_body(q_ref, k_ref, v_ref, la_ref, zr_ref, g_ref, o_ref):
    seq = q_ref.shape[1]
    r = lax.broadcasted_iota(jnp.int32, (CHUNK, CHUNK), 0)
    c = lax.broadcasted_iota(jnp.int32, (CHUNK, CHUNK), 1)
    later = (c > r).astype(BF16)

    def chunk(n, state_t):
        sl = pl.ds(pl.multiple_of(n * CHUNK, CHUNK), CHUNK)
        g = la_ref[0, sl, :]
        g_hi = g.astype(BF16)
        r1 = g - g_hi.astype(F32)
        g_mid = r1.astype(BF16)
        g_lo = (r1 - g_mid.astype(F32)).astype(BF16)
        rest3 = _dot(later, jnp.concatenate([g_hi, g_mid, g_lo], axis=1))
        k_dim = g.shape[1]
        rest = rest3[:, :k_dim] + rest3[:, k_dim:2 * k_dim] + rest3[:, 2 * k_dim:]
        b_end = jnp.sum(g, axis=0, keepdims=True)
        k_dec = (k_ref[0, sl, :] * jnp.exp(rest)).astype(BF16)
        u_t = lax.dot_general(v_ref[0, sl, :], k_dec, (((0,), (0,)), ((), ())),
                              preferred_element_type=F32)
        state_t = state_t * jnp.exp(b_end) + u_t
        o = _dot_nt(q_ref[0, sl, :], state_t.astype(BF16))
        zr = zr_ref[0, sl, :].astype(F32)
        o_ref[0, sl, :] = (_rms(o, g_ref[...]) * (zr * jax.nn.sigmoid(zr))).astype(o_ref.dtype)
        return state_t

    lax.fori_loop(0, seq // CHUNK, chunk, jnp.zeros((GLA_DV, GLA_DK), F32))


def _gla(q, k, v, la, zr, g):
    b, s, _ = q.shape
    kspec = pl.BlockSpec((1, s, GLA_DK), lambda i, j: (i, 0, j))
    vspec = pl.BlockSpec((1, s, GLA_DV), lambda i, j: (i, 0, j))
    return pl.pallas_call(
        _gla_body,
        grid=(b, GLA_HEADS),
        in_specs=[kspec, kspec, vspec, kspec, vspec,
                  pl.BlockSpec((1, GLA_DV), lambda i, j: (0, 0))],
        out_specs=vspec,
        out_shape=jax.ShapeDtypeStruct((b, s, GLA_HEADS * GLA_DV), BF16),
        compiler_params=pltpu.CompilerParams(
            dimension_semantics=("parallel", "parallel"), vmem_limit_bytes=V7X_VMEM_LIMIT),
        name="gla",
    )(q, k, v, la, zr, g)


def _mem_kv_body(m_ref, gn_ref, wk_ref, wv_ref, gk_ref, k_ref, v_ref):
    hm = _rms(m_ref[0], gn_ref[...]).astype(BF16)
    kf = _dot(hm, wk_ref[...])
    for hd in range(MEM_HEADS):
        c = hd * MEM_HEAD_DIM
        k_ref[0, :, c:c + MEM_HEAD_DIM] = _rms(kf[:, c:c + MEM_HEAD_DIM], gk_ref[...]).astype(BF16)
    v_ref[0] = _dot(hm, wv_ref[...]).astype(BF16)


def _mem_kv(mem, gn, wk, wv, gk):
    b, n, d = mem.shape
    w = wk.shape[1]
    ospec = pl.BlockSpec((1, n, w), lambda i: (i, 0, 0))
    return pl.pallas_call(
        _mem_kv_body,
        grid=(b,),
        in_specs=[pl.BlockSpec((1, n, d), lambda i: (i, 0, 0)),
                  _resident(gn.shape), _resident(wk.shape), _resident(wv.shape),
                  _resident(gk.shape)],
        out_specs=[ospec, ospec],
        out_shape=[jax.ShapeDtypeStruct((b, n, w), BF16)] * 2,
        compiler_params=pltpu.CompilerParams(
            dimension_semantics=("parallel",), vmem_limit_bytes=V7X_VMEM_LIMIT),
        name="mem_kv",
    )(mem, gn, wk, wv, gk)


def _out_mem_body(x_ref, om_ref, og_ref, wo_ref, gn_ref, wq_ref, gq_ref, km_ref, vm_ref,
                  wmo_ref, o_ref):
    n_mla = om_ref.shape[2]
    x = x_ref[0] + _dot(om_ref[0], wo_ref[:n_mla, :]) + _dot(og_ref[0], wo_ref[n_mla:, :])
    h = _rms(x, gn_ref[...]).astype(BF16)
    qm = _dot(h, wq_ref[...])
    outs = []
    for hd in range(MEM_HEADS):
        c = hd * MEM_HEAD_DIM
        qn = (_rms(qm[:, c:c + MEM_HEAD_DIM], gq_ref[...]) * (MEM_HEAD_DIM ** -0.5)).astype(BF16)
        s = _dot_nt(qn, km_ref[0, :, c:c + MEM_HEAD_DIM])
        p = jnp.exp(s - jnp.max(s, axis=-1, keepdims=True))
        l = jnp.sum(p, axis=-1, keepdims=True)
        oh = _dot(p.astype(BF16), vm_ref[0, :, c:c + MEM_HEAD_DIM]) * (1.0 / l)
        outs.append(oh.astype(BF16))
    o_ref[0] = x + _dot(jnp.concatenate(outs, axis=1), wmo_ref[...])


def _out_mem(x, om, og, wo, gn, wq, gq, km, vm, wmo):
    b, s, d = x.shape
    tm = OUT_TM
    n_mem, wm = km.shape[1], km.shape[2]
    tok = lambda w: pl.BlockSpec((1, tm, w), lambda i, j: (i, j, 0))
    memspec = pl.BlockSpec((1, n_mem, wm), lambda i, j: (i, 0, 0))
    return pl.pallas_call(
        _out_mem_body,
        grid=(b, s // tm),
        in_specs=[tok(d), tok(om.shape[2]), tok(og.shape[2]),
                  _resident(wo.shape), _resident(gn.shape), _resident(wq.shape),
                  _resident(gq.shape), memspec, memspec, _resident(wmo.shape)],
        out_specs=tok(d),
        out_shape=jax.ShapeDtypeStruct((b, s, d), F32),
        compiler_params=pltpu.CompilerParams(
            dimension_semantics=("parallel", "parallel"), vmem_limit_bytes=V7X_VMEM_LIMIT),
        name="out_mem",
    )(x, om, og, wo, gn, wq, gq, km, vm, wmo)


def _pad_rope(w):
    z = jnp.zeros(w.shape[:-1] + (LANES // 2 - HALF_ROPE,), w.dtype)
    return jnp.concatenate([w[..., :HALF_ROPE], z, w[..., HALF_ROPE:], z], axis=-1)


def _pad_head(w):
    return jnp.concatenate([w[..., :MLA_NOPE], _pad_rope(w[..., MLA_NOPE:])], axis=-1)


def _pad_cols(w, width):
    return jnp.pad(w, [(0, 0)] * (w.ndim - 1) + [(0, width - w.shape[-1])])


def _layout_w_in(w_in):
    o = 0
    parts = {}
    for name, size in (("zq", MLA_Q_RANK), ("zkv", MLA_KV_RANK), ("zkr", MLA_ROPE),
                       ("gq", GLA_HEADS * GLA_DK), ("gk", GLA_HEADS * GLA_DK),
                       ("gv", GLA_HEADS * GLA_DV), ("zg", GLA_GATE_RANK),
                       ("zr", GLA_HEADS * GLA_DV)):
        parts[name] = w_in[:, o:o + size]
        o += size
    return jnp.concatenate(
        [parts["zq"], parts["zkv"], _pad_rope(parts["zkr"]), parts["gq"], parts["gk"],
         parts["gv"], parts["zr"], _pad_cols(parts["zg"], LANES)], axis=1)


def kernel(x, mem, positions, ffn1_norm, ffn1_w_gate, ffn1_w_up, ffn1_w_down, mix_norm, w_in, q_a_norm, w_q_up, kv_a_norm, w_kv_up, mla_q_norm, mla_k_norm, gla_w_gate2, gla_b_gate, gla_out_norm, w_out, mem_attn_norm, mem_norm, mem_w_q, mem_w_k, mem_w_v, mem_w_o, mem_q_norm, mem_k_norm, ffn2_norm, ffn2_w_gate, ffn2_w_up, ffn2_w_down):
    b, s, d = x.shape
    m = b * s
    depth = w_in.shape[0]
    row = lambda v: v.reshape(1, -1)
    bf = lambda w: w.astype(BF16)

    half = jnp.arange(HALF_ROPE, dtype=F32)
    inv_freq = ROPE_THETA ** (-half / HALF_ROPE)
    invf = _pad_rope(jnp.concatenate([inv_freq, inv_freq])).reshape(1, LANES)
    pos = positions.reshape(m, 1)

    xf = x.reshape(m, d)
    for l in range(depth):
        xf = _ffn(xf, row(ffn1_norm[l]), bf(ffn1_w_gate[l]), bf(ffn1_w_up[l]), bf(ffn1_w_down[l]))

        wq = _pad_head(w_q_up[l].reshape(MLA_Q_RANK, MLA_HEADS, MLA_QK)).reshape(MLA_Q_RANK, -1)
        wkv = w_kv_up[l].reshape(MLA_KV_RANK, MLA_HEADS, MLA_NOPE + MLA_V)
        wkv = jnp.concatenate([wkv[..., :MLA_NOPE].reshape(MLA_KV_RANK, -1),
                               wkv[..., MLA_NOPE:].reshape(MLA_KV_RANK, -1)], axis=1)
        w2 = jnp.pad(gla_w_gate2[l], ((0, LANES - GLA_GATE_RANK), (0, 0)))
        q, k, v, gq, gk, gv, la, zr = _mixer_proj(
            xf, pos, row(mix_norm[l]), bf(_layout_w_in(w_in[l])), row(q_a_norm[l]), bf(wq),
            row(kv_a_norm[l]), bf(wkv), row(_pad_head(mla_q_norm[l])), row(_pad_head(mla_k_norm[l])),
            invf, bf(w2), row(gla_b_gate[l]))

        r3 = lambda a: a.reshape(b, s, a.shape[-1])
        o_mla = _mla_attn(r3(q), r3(k), r3(v))
        o_gla = _gla(r3(gq), r3(gk), r3(gv), r3(la), r3(zr), row(gla_out_norm[l]))

        km, vm = _mem_kv(mem, row(mem_norm[l]), bf(mem_w_k[l]), bf(mem_w_v[l]), row(mem_k_norm[l]))
        x3 = _out_mem(r3(xf), o_mla, o_gla, bf(w_out[l]), row(mem_attn_norm[l]), bf(mem_w_q[l]),
                      row(mem_q_norm[l]), km, vm, bf(mem_w_o[l]))

        xf = _ffn(x3.reshape(m, d), row(ffn2_norm[l]), bf(ffn2_w_gate[l]), bf(ffn2_w_up[l]),
                  bf(ffn2_w_down[l]))
    return xf.reshape(b, s, d)
```

```python
import jax
import jax.numpy as jnp
from jax import lax
from jax.experimental import pallas as pl
from jax.experimental.pallas import tpu as pltpu

F32 = jnp.float32
BF16 = jnp.bfloat16

EPS = 1e-6
CHUNK = 64
ROPE_THETA = 10000.0
GLA_TAU = 16.0

MLA_HEADS = 8
MLA_NOPE = 128
MLA_ROPE = 64
MLA_QK = MLA_NOPE + MLA_ROPE
MLA_V = 128
MLA_Q_RANK = 512
MLA_KV_RANK = 256
GLA_HEADS = 4
GLA_DK = 128
GLA_DV = 256
GLA_GATE_RANK = 16
MEM_HEADS = 4
MEM_HEAD_DIM = 128

LANES = 128
HEAD_PAD = 2 * LANES
HALF_ROPE = MLA_ROPE // 2

V7X_VMEM_LIMIT = 56 * 1024 * 1024

FFN_TM = 1024
FFN_TF = 256
FFN_TN = 512
PROJ_TM = 512
ATTN_TQ = 256
OUT_TM = 512


def _rms(x, g):
    ms = jnp.mean(x * x, axis=-1, keepdims=True)
    return x * lax.rsqrt(ms + EPS) * g


def _dot(a, b):
    return jnp.dot(a, b, preferred_element_type=F32)


def _dot_nt(a, b):
    return lax.dot_general(a, b, (((1,), (1,)), ((), ())), preferred_element_type=F32)


def _resident(shape):
    nd = len(shape)
    return pl.BlockSpec(shape, lambda *_: (0,) * nd, pipeline_mode=pl.Buffered(1))


def _ffn_body(x_ref, g_ref, wg_ref, wu_ref, wd_ref, o_ref, h_ref):
    @pl.when(pl.program_id(1) == 0)
    def _():
        x = x_ref[...]
        h_ref[...] = _rms(x, g_ref[...]).astype(BF16)
        o_ref[...] = x

    h = h_ref[...]
    gate = _dot(h, wg_ref[...])
    up = _dot(h, wu_ref[...])
    a = ((0.5 * gate) * jax.nn.sigmoid(gate) * up).astype(BF16)
    for c in range(0, o_ref.shape[1], FFN_TN):
        o_ref[:, c:c + FFN_TN] += _dot(a, wd_ref[:, c:c + FFN_TN])


def _ffn(x, g, wg, wu, wd):
    m, d = x.shape
    f = wg.shape[1]
    return pl.pallas_call(
        _ffn_body,
        grid=(m // FFN_TM, f // FFN_TF),
        in_specs=[
            pl.BlockSpec((FFN_TM, d), lambda i, j: (i, 0)),
            pl.BlockSpec((1, d), lambda i, j: (0, 0)),
            pl.BlockSpec((d, FFN_TF), lambda i, j: (0, j)),
            pl.BlockSpec((d, FFN_TF), lambda i, j: (0, j)),
            pl.BlockSpec((FFN_TF, d), lambda i, j: (j, 0)),
        ],
        out_specs=pl.BlockSpec((FFN_TM, d), lambda i, j: (i, 0)),
        out_shape=jax.ShapeDtypeStruct((m, d), F32),
        scratch_shapes=[pltpu.VMEM((FFN_TM, d), BF16)],
        compiler_params=pltpu.CompilerParams(
            dimension_semantics=("parallel", "arbitrary"),
            vmem_limit_bytes=V7X_VMEM_LIMIT),
        name="ffn",
    )(x, g, wg, wu, wd)


def _mixer_proj_body(x_ref, pos_ref, gmix_ref, w_lat_ref, w_kr_ref, w_gla_ref, w_zr_ref, w_zg_ref,
                     gqa_ref, wq_ref, gkva_ref, wkv_ref, gq_ref, gk_ref, invf_ref, w2_ref, b2_ref,
                     q_out, k_out, v_out, gq_out, gk_out, gv_out, la_out, zr_out):
    h = _rms(x_ref[...], gmix_ref[...]).astype(BF16)

    ang = pos_ref[...].astype(F32) * invf_ref[...]
    lane = lax.broadcasted_iota(jnp.int32, (1, LANES), 1)
    cos_t = jnp.cos(ang)
    sin_t = jnp.sin(ang) * jnp.where(lane < LANES // 2, -1.0, 1.0)

    def rope(v):
        return v * cos_t + pltpu.roll(v, LANES // 2, axis=1) * sin_t

    inv_qk = 1.0 / MLA_QK

    qa = _rms(_dot(h, w_lat_ref[:, :MLA_Q_RANK]), gqa_ref[...]).astype(BF16)
    qf = _dot(qa, wq_ref[...])
    for hd in range(MLA_HEADS):
        c = hd * HEAD_PAD
        blk = qf[:, c:c + HEAD_PAD]
        ss = jnp.sum(blk * blk, axis=-1, keepdims=True)
        rinv = lax.rsqrt(ss * inv_qk + EPS) * (MLA_QK ** -0.5)
        q_out[:, c:c + LANES] = (blk[:, :LANES] * rinv * gq_ref[:, :LANES]).astype(BF16)
        q_out[:, c + LANES:c + HEAD_PAD] = rope(
            blk[:, LANES:] * rinv * gq_ref[:, LANES:]).astype(BF16)

    kva = _rms(_dot(h, w_lat_ref[:, MLA_Q_RANK:]), gkva_ref[...]).astype(BF16)
    kvf = _dot(kva, wkv_ref[...])
    zkr = _dot(h, w_kr_ref[...])
    ssr = jnp.sum(zkr * zkr, axis=-1, keepdims=True)
    kr_base = rope(zkr * gk_ref[:, LANES:])
    for hd in range(MLA_HEADS):
        kn = kvf[:, hd * MLA_NOPE:(hd + 1) * MLA_NOPE]
        ss = jnp.sum(kn * kn, axis=-1, keepdims=True) + ssr
        rinv = lax.rsqrt(ss * inv_qk + EPS)
        c = hd * HEAD_PAD
        k_out[:, c:c + LANES] = (kn * rinv * gk_ref[:, :LANES]).astype(BF16)
        k_out[:, c + LANES:c + HEAD_PAD] = (kr_base * rinv).astype(BF16)
    v_out[...] = kvf[:, MLA_HEADS * MLA_NOPE:].astype(BF16)

    n_qk = GLA_HEADS * GLA_DK
    gq_out[...] = (_dot(h, w_gla_ref[:, :n_qk]) * (GLA_DK ** -0.5)).astype(BF16)
    gk_out[...] = _dot(h, w_gla_ref[:, n_qk:2 * n_qk])
    gv_out[...] = _dot(h, w_gla_ref[:, 2 * n_qk:]).astype(BF16)
    zr_out[...] = _dot(h, w_zr_ref[...]).astype(BF16)
    zg = _dot(h, w_zg_ref[...]).astype(BF16)
    pre = _dot(zg, w2_ref[...]) + b2_ref[...]
    log_sig = jnp.minimum(pre, 0.0) - jnp.log1p(jnp.exp(-jnp.abs(pre)))
    la_out[...] = log_sig * (1.0 / GLA_TAU)


def _mixer_proj(x, pos, *weights):
    m, d = x.shape
    tm = PROJ_TM
    row = lambda i: (i, 0)
    widths = [
        (MLA_HEADS * HEAD_PAD, BF16),
        (MLA_HEADS * HEAD_PAD, BF16),
        (MLA_HEADS * MLA_V, BF16),
        (GLA_HEADS * GLA_DK, BF16),
        (GLA_HEADS * GLA_DK, F32),
        (GLA_HEADS * GLA_DV, BF16),
        (GLA_HEADS * GLA_DK, F32),
        (GLA_HEADS * GLA_DV, BF16),
    ]
    return pl.pallas_call(
        _mixer_proj_body,
        grid=(m // tm,),
        in_specs=[pl.BlockSpec((tm, d), row), pl.BlockSpec((tm, 1), row)]
        + [_resident(w.shape) for w in weights],
        out_specs=[pl.BlockSpec((tm, w), row) for w, _ in widths],
        out_shape=[jax.ShapeDtypeStruct((m, w), dt) for w, dt in widths],
        compiler_params=pltpu.CompilerParams(
            dimension_semantics=("parallel",), vmem_limit_bytes=V7X_VMEM_LIMIT),
        name="mixer_proj",
    )(x, pos, *weights)


def _mla_attn_body(q_ref, k_ref, v_ref, o_ref):
    seq = q_ref.shape[1]
    tq = ATTN_TQ
    r = lax.broadcasted_iota(jnp.int32, (tq, tq), 0) // CHUNK
    c = lax.broadcasted_iota(jnp.int32, (tq, tq), 1) // CHUNK
    diag_mask = c <= r
    for i in range(seq // tq):
        lo = i * tq
        q = q_ref[0, lo:lo + tq, :]
        s_d = jnp.where(diag_mask, _dot_nt(q, k_ref[0, lo:lo + tq, :]), -jnp.inf)
        m = jnp.max(s_d, axis=-1, keepdims=True)
        if i > 0:
            s_o = _dot_nt(q, k_ref[0, :lo, :])
            m = jnp.maximum(m, jnp.max(s_o, axis=-1, keepdims=True))
        p_d = jnp.exp(s_d - m)
        l = jnp.sum(p_d, axis=-1, keepdims=True)
        acc = _dot(p_d.astype(BF16), v_ref[0, lo:lo + tq, :])
        if i > 0:
            p_o = jnp.exp(s_o - m)
            l = l + jnp.sum(p_o, axis=-1, keepdims=True)
            acc = acc + _dot(p_o.astype(BF16), v_ref[0, :lo, :])
        o_ref[0, lo:lo + tq, :] = (acc * (1.0 / l)).astype(o_ref.dtype)


def _mla_attn(q, k, v):
    b, s, _ = q.shape
    return pl.pallas_call(
        _mla_attn_body,
        grid=(b, MLA_HEADS),
        in_specs=[
            pl.BlockSpec((1, s, HEAD_PAD), lambda i, j: (i, 0, j)),
            pl.BlockSpec((1, s, HEAD_PAD), lambda i, j: (i, 0, j)),
            pl.BlockSpec((1, s, MLA_V), lambda i, j: (i, 0, j)),
        ],
        out_specs=pl.BlockSpec((1, s, MLA_V), lambda i, j: (i, 0, j)),
        out_shape=jax.ShapeDtypeStruct((b, s, MLA_HEADS * MLA_V), BF16),
        compiler_params=pltpu.CompilerParams(
            dimension_semantics=("parallel", "parallel"), vmem_limit_bytes=V7X_VMEM_LIMIT),
        name="mla_attn",
    )(q, k, v)


def _gla_body(q_ref, k_ref, v_ref, la_ref, zr_ref, g_ref, o_ref, u_ref, st_ref):
    seq, k_dim = k_ref.shape[1], k_ref.shape[2]
    v_dim = v_ref.shape[2]
    n_chunks = seq // CHUNK
    r = lax.broadcasted_iota(jnp.int32, (CHUNK, CHUNK), 0)
    c = lax.broadcasted_iota(jnp.int32, (CHUNK, CHUNK), 1)
    later = (c > r).astype(BF16)

    b_end = jnp.sum(la_ref[0].reshape(n_chunks, CHUNK, k_dim), axis=1)
    dec = jnp.concatenate([jnp.exp(b_end), jnp.zeros((LANES - n_chunks, k_dim), F32)], axis=0)
    dec_t = dec.T

    for n in range(n_chunks):
        sl = slice(n * CHUNK, (n + 1) * CHUNK)
        g = la_ref[0, sl, :]
        g_hi = g.astype(BF16)
        g_lo = (g - g_hi.astype(F32)).astype(BF16)
        rest2 = _dot(later, jnp.concatenate([g_hi, g_lo], axis=1))
        rest = rest2[:, :k_dim] + rest2[:, k_dim:]
        k_dec = (k_ref[0, sl, :] * jnp.exp(rest)).astype(BF16)
        u_ref[n] = lax.dot_general(k_dec, v_ref[0, sl, :], (((0,), (0,)), ((), ())),
                                   preferred_element_type=F32)

    state = jnp.zeros((k_dim, v_dim), F32)
    for n in range(n_chunks):
        state = state * dec_t[:, n:n + 1] + u_ref[n]
        st_ref[n] = state.astype(BF16)

    for n in range(n_chunks):
        sl = slice(n * CHUNK, (n + 1) * CHUNK)
        o = _dot(q_ref[0, sl, :], st_ref[n])
        zr = zr_ref[0, sl, :].astype(F32)
        o_ref[0, sl, :] = (_rms(o, g_ref[...]) * (zr * jax.nn.sigmoid(zr))).astype(o_ref.dtype)


def _gla(q, k, v, la, zr, g):
    b, s, _ = q.shape
    n_chunks = s // CHUNK
    kspec = pl.BlockSpec((1, s, GLA_DK), lambda i, j: (i, 0, j))
    vspec = pl.BlockSpec((1, s, GLA_DV), lambda i, j: (i, 0, j))
    return pl.pallas_call(
        _gla_body,
        grid=(b, GLA_HEADS),
        in_specs=[kspec, kspec, vspec, kspec, vspec,
                  pl.BlockSpec((1, GLA_DV), lambda i, j: (0, 0))],
        out_specs=vspec,
        out_shape=jax.ShapeDtypeStruct((b, s, GLA_HEADS * GLA_DV), BF16),
        scratch_shapes=[pltpu.VMEM((n_chunks, GLA_DK, GLA_DV), F32),
                        pltpu.VMEM((n_chunks, GLA_DK, GLA_DV), BF16)],
        compiler_params=pltpu.CompilerParams(
            dimension_semantics=("parallel", "parallel"), vmem_limit_bytes=V7X_VMEM_LIMIT),
        name="gla",
    )(q, k, v, la, zr, g)


def _mem_kv_body(m_ref, gn_ref, wk_ref, wv_ref, gk_ref, k_ref, v_ref):
    hm = _rms(m_ref[0], gn_ref[...]).astype(BF16)
    kf = _dot(hm, wk_ref[...])
    for hd in range(MEM_HEADS):
        c = hd * MEM_HEAD_DIM
        k_ref[0, :, c:c + MEM_HEAD_DIM] = _rms(kf[:, c:c + MEM_HEAD_DIM], gk_ref[...]).astype(BF16)
    v_ref[0] = _dot(hm, wv_ref[...]).astype(BF16)


def _mem_kv(mem, gn, wk, wv, gk):
    b, n, d = mem.shape
    w = wk.shape[1]
    ospec = pl.BlockSpec((1, n, w), lambda i: (i, 0, 0))
    return pl.pallas_call(
        _mem_kv_body,
        grid=(b,),
        in_specs=[pl.BlockSpec((1, n, d), lambda i: (i, 0, 0)),
                  _resident(gn.shape), _resident(wk.shape), _resident(wv.shape),
                  _resident(gk.shape)],
        out_specs=[ospec, ospec],
        out_shape=[jax.ShapeDtypeStruct((b, n, w), BF16)] * 2,
        compiler_params=pltpu.CompilerParams(
            dimension_semantics=("parallel",), vmem_limit_bytes=V7X_VMEM_LIMIT),
        name="mem_kv",
    )(mem, gn, wk, wv, gk)


def _out_mem_body(x_ref, om_ref, og_ref, wo_ref, gn_ref, wq_ref, gq_ref, km_ref, vm_ref,
                  wmo_ref, o_ref):
    n_mla = om_ref.shape[2]
    x = x_ref[0] + _dot(om_ref[0], wo_ref[:n_mla, :]) + _dot(og_ref[0], wo_ref[n_mla:, :])
    h = _rms(x, gn_ref[...]).astype(BF16)
    qm = _dot(h, wq_ref[...])
    outs = []
    for hd in range(MEM_HEADS):
        c = hd * MEM_HEAD_DIM
        qn = (_rms(qm[:, c:c + MEM_HEAD_DIM], gq_ref[...]) * (MEM_HEAD_DIM ** -0.5)).astype(BF16)
        s = _dot_nt(qn, km_ref[0, :, c:c + MEM_HEAD_DIM])
        p = jnp.exp(s - jnp.max(s, axis=-1, keepdims=True))
        l = jnp.sum(p, axis=-1, keepdims=True)
        oh = _dot(p.astype(BF16), vm_ref[0, :, c:c + MEM_HEAD_DIM]) * (1.0 / l)
        outs.append(oh.astype(BF16))
    o_ref[0] = x + _dot(jnp.concatenate(outs, axis=1), wmo_ref[...])


def _out_mem(x, om, og, wo, gn, wq, gq, km, vm, wmo):
    b, s, d = x.shape
    tm = OUT_TM
    n_mem, wm = km.shape[1], km.shape[2]
    tok = lambda w: pl.BlockSpec((1, tm, w), lambda i, j: (i, j, 0))
    memspec = pl.BlockSpec((1, n_mem, wm), lambda i, j: (i, 0, 0))
    return pl.pallas_call(
        _out_mem_body,
        grid=(b, s // tm),
        in_specs=[tok(d), tok(om.shape[2]), tok(og.shape[2]),
                  _resident(wo.shape), _resident(gn.shape), _resident(wq.shape),
                  _resident(gq.shape), memspec, memspec, _resident(wmo.shape)],
        out_specs=tok(d),
        out_shape=jax.ShapeDtypeStruct((b, s, d), F32),
        compiler_params=pltpu.CompilerParams(
            dimension_semantics=("parallel", "parallel"), vmem_limit_bytes=V7X_VMEM_LIMIT),
        name="out_mem",
    )(x, om, og, wo, gn, wq, gq, km, vm, wmo)


def _pad_rope(w):
    z = jnp.zeros(w.shape[:-1] + (LANES // 2 - HALF_ROPE,), w.dtype)
    return jnp.concatenate([w[..., :HALF_ROPE], z, w[..., HALF_ROPE:], z], axis=-1)


def _pad_head(w):
    return jnp.concatenate([w[..., :MLA_NOPE], _pad_rope(w[..., MLA_NOPE:])], axis=-1)


def _split_w_in(w_in):
    o = 0
    parts = {}
    for name, size in (("zq", MLA_Q_RANK), ("zkv", MLA_KV_RANK), ("zkr", MLA_ROPE),
                       ("gq", GLA_HEADS * GLA_DK), ("gk", GLA_HEADS * GLA_DK),
                       ("gv", GLA_HEADS * GLA_DV), ("zg", GLA_GATE_RANK),
                       ("zr", GLA_HEADS * GLA_DV)):
        parts[name] = (o, o + size)
        o += size
    cols = lambda a, b: w_in[:, parts[a][0]:parts[b][1]].astype(BF16)
    w_zg = jnp.pad(cols("zg", "zg"), ((0, 0), (0, LANES - GLA_GATE_RANK)))
    return cols("zq", "zkv"), _pad_rope(cols("zkr", "zkr")), cols("gq", "gv"), cols("zr", "zr"), w_zg


def kernel(x, mem, positions, ffn1_norm, ffn1_w_gate, ffn1_w_up, ffn1_w_down, mix_norm, w_in, q_a_norm, w_q_up, kv_a_norm, w_kv_up, mla_q_norm, mla_k_norm, gla_w_gate2, gla_b_gate, gla_out_norm, w_out, mem_attn_norm, mem_norm, mem_w_q, mem_w_k, mem_w_v, mem_w_o, mem_q_norm, mem_k_norm, ffn2_norm, ffn2_w_gate, ffn2_w_up, ffn2_w_down):
    b, s, d = x.shape
    m = b * s
    depth = w_in.shape[0]
    row = lambda v: v.reshape(1, -1)
    bf = lambda w: w.astype(BF16)

    half = jnp.arange(HALF_ROPE, dtype=F32)
    inv_freq = ROPE_THETA ** (-half / HALF_ROPE)
    invf = _pad_rope(jnp.concatenate([inv_freq, inv_freq])).reshape(1, LANES)
    pos = positions.reshape(m, 1)

    xf = x.reshape(m, d)
    for l in range(depth):
        xf = _ffn(xf, row(ffn1_norm[l]), bf(ffn1_w_gate[l]), bf(ffn1_w_up[l]), bf(ffn1_w_down[l]))

        wq = _pad_head(w_q_up[l].reshape(MLA_Q_RANK, MLA_HEADS, MLA_QK)).reshape(MLA_Q_RANK, -1)
        wkv = w_kv_up[l].reshape(MLA_KV_RANK, MLA_HEADS, MLA_NOPE + MLA_V)
        wkv = jnp.concatenate([wkv[..., :MLA_NOPE].reshape(MLA_KV_RANK, -1),
                               wkv[..., MLA_NOPE:].reshape(MLA_KV_RANK, -1)], axis=1)
        w2 = jnp.pad(gla_w_gate2[l], ((0, LANES - GLA_GATE_RANK), (0, 0)))
        q, k, v, gq, gk, gv, la, zr = _mixer_proj(
            xf, pos, row(mix_norm[l]), *_split_w_in(w_in[l]), row(q_a_norm[l]), bf(wq),
            row(kv_a_norm[l]), bf(wkv), row(_pad_head(mla_q_norm[l])), row(_pad_head(mla_k_norm[l])),
            invf, bf(w2), row(gla_b_gate[l]))

        r3 = lambda a: a.reshape(b, s, a.shape[-1])
        o_mla = _mla_attn(r3(q), r3(k), r3(v))
        o_gla = _gla(r3(gq), r3(gk), r3(gv), r3(la), r3(zr), row(gla_out_norm[l]))

        km, vm = _mem_kv(mem, row(mem_norm[l]), bf(mem_w_k[l]), bf(mem_w_v[l]), row(mem_k_norm[l]))
        x3 = _out_mem(r3(xf), o_mla, o_gla, bf(w_out[l]), row(mem_attn_norm[l]), bf(mem_w_q[l]),
                      row(mem_q_norm[l]), km, vm, bf(mem_w_o[l]))

        xf = _ffn(x3.reshape(m, d), row(ffn2_norm[l]), bf(ffn2_w_gate[l]), bf(ffn2_w_up[l]),
                  bf(ffn2_w_down[l]))
    return xf.reshape(b, s, d)
```

```python
import jax
import jax.numpy as jnp
from jax import lax
from jax.experimental import pallas as pl
from jax.experimental.pallas import tpu as pltpu

F32 = jnp.float32
BF16 = jnp.bfloat16

EPS = 1e-6
CHUNK = 64
ROPE_THETA = 10000.0
GLA_TAU = 16.0

MLA_HEADS = 8
MLA_NOPE = 128
MLA_ROPE = 64
MLA_QK = MLA_NOPE + MLA_ROPE
MLA_V = 128
MLA_Q_RANK = 512
MLA_KV_RANK = 256
GLA_HEADS = 4
GLA_DK = 128
GLA_DV = 256
GLA_GATE_RANK = 16
MEM_HEADS = 4
MEM_HEAD_DIM = 128

LANES = 128
HEAD_PAD = 2 * LANES
HALF_ROPE = MLA_ROPE // 2

V7X_VMEM_LIMIT = 56 * 1024 * 1024

FFN_TM = 1024
FFN_TF = 256
FFN_TN = 512
PROJ_TM = 512
ATTN_TQ = 256
OUT_TM = 512


def _rms(x, g):
    ms = jnp.mean(x * x, axis=-1, keepdims=True)
    return x * lax.rsqrt(ms + EPS) * g


def _dot(a, b):
    return jnp.dot(a, b, preferred_element_type=F32)


def _dot_nt(a, b):
    return lax.dot_general(a, b, (((1,), (1,)), ((), ())), preferred_element_type=F32)


def _resident(shape):
    nd = len(shape)
    return pl.BlockSpec(shape, lambda *_: (0,) * nd, pipeline_mode=pl.Buffered(1))


def _ffn_body(x_ref, g_ref, wg_ref, wu_ref, wd_ref, o_ref, h_ref):
    @pl.when(pl.program_id(1) == 0)
    def _():
        x = x_ref[...]
        h_ref[...] = _rms(x, g_ref[...]).astype(BF16)
        o_ref[...] = x

    h = h_ref[...]
    gate = _dot(h, wg_ref[...].astype(BF16))
    up = _dot(h, wu_ref[...].astype(BF16))
    a = ((0.5 * gate) * jax.nn.sigmoid(gate) * up).astype(BF16)
    for c in range(0, o_ref.shape[1], FFN_TN):
        o_ref[:, c:c + FFN_TN] += _dot(a, wd_ref[:, c:c + FFN_TN].astype(BF16))


def _ffn(x, g, wg, wu, wd):
    m, d = x.shape
    f = wg.shape[1]
    return pl.pallas_call(
        _ffn_body,
        grid=(m // FFN_TM, f // FFN_TF),
        in_specs=[
            pl.BlockSpec((FFN_TM, d), lambda i, j: (i, 0)),
            pl.BlockSpec((1, d), lambda i, j: (0, 0)),
            pl.BlockSpec((d, FFN_TF), lambda i, j: (0, j)),
            pl.BlockSpec((d, FFN_TF), lambda i, j: (0, j)),
            pl.BlockSpec((FFN_TF, d), lambda i, j: (j, 0)),
        ],
        out_specs=pl.BlockSpec((FFN_TM, d), lambda i, j: (i, 0)),
        out_shape=jax.ShapeDtypeStruct((m, d), F32),
        scratch_shapes=[pltpu.VMEM((FFN_TM, d), BF16)],
        compiler_params=pltpu.CompilerParams(
            dimension_semantics=("parallel", "arbitrary"),
            vmem_limit_bytes=V7X_VMEM_LIMIT),
        name="ffn",
    )(x, g, wg, wu, wd)


def _mixer_proj_body(x_ref, pos_ref, gmix_ref, w_lat_ref, w_kr_ref, w_gla_ref, w_zr_ref, w_zg_ref,
                     gqa_ref, wq_ref, gkva_ref, wkv_ref, gq_ref, gk_ref, invf_ref, w2_ref, b2_ref,
                     q_out, k_out, v_out, gq_out, gk_out, gv_out, la_out, zr_out):
    h = _rms(x_ref[...], gmix_ref[...]).astype(BF16)
    n_qk = GLA_HEADS * GLA_DK
    zq = _dot(h, w_lat_ref[:, :MLA_Q_RANK])
    zkv = _dot(h, w_lat_ref[:, MLA_Q_RANK:])
    zkr = _dot(h, w_kr_ref[...])
    zg = _dot(h, w_zg_ref[...]).astype(BF16)
    gq_out[...] = (_dot(h, w_gla_ref[:, :n_qk]) * (GLA_DK ** -0.5)).astype(BF16)

    pre = _dot(zg, w2_ref[...]) + b2_ref[...]
    log_sig = jnp.minimum(pre, 0.0) - jnp.log1p(jnp.exp(-jnp.abs(pre)))
    la_out[...] = log_sig * (1.0 / GLA_TAU)

    ang = pos_ref[...].astype(F32) * invf_ref[...]
    lane = lax.broadcasted_iota(jnp.int32, (1, LANES), 1)
    cos_t = jnp.cos(ang)
    sin_t = jnp.sin(ang) * jnp.where(lane < LANES // 2, -1.0, 1.0)

    def rope(v):
        return v * cos_t + pltpu.roll(v, LANES // 2, axis=1) * sin_t

    inv_qk = 1.0 / MLA_QK

    qa = _rms(zq, gqa_ref[...]).astype(BF16)
    kva = _rms(zkv, gkva_ref[...]).astype(BF16)
    qf = _dot(qa, wq_ref[...])
    kvf = _dot(kva, wkv_ref[...])
    gk_out[...] = _dot(h, w_gla_ref[:, n_qk:2 * n_qk])
    for hd in range(MLA_HEADS):
        c = hd * HEAD_PAD
        blk = qf[:, c:c + HEAD_PAD]
        ss = jnp.sum(blk * blk, axis=-1, keepdims=True)
        rinv = lax.rsqrt(ss * inv_qk + EPS) * (MLA_QK ** -0.5)
        q_out[:, c:c + LANES] = (blk[:, :LANES] * rinv * gq_ref[:, :LANES]).astype(BF16)
        q_out[:, c + LANES:c + HEAD_PAD] = rope(
            blk[:, LANES:] * rinv * gq_ref[:, LANES:]).astype(BF16)

    gv_out[...] = _dot(h, w_gla_ref[:, 2 * n_qk:]).astype(BF16)

    ssr = jnp.sum(zkr * zkr, axis=-1, keepdims=True)
    kr_base = rope(zkr * gk_ref[:, LANES:])
    for hd in range(MLA_HEADS):
        kn = kvf[:, hd * MLA_NOPE:(hd + 1) * MLA_NOPE]
        ss = jnp.sum(kn * kn, axis=-1, keepdims=True) + ssr
        rinv = lax.rsqrt(ss * inv_qk + EPS)
        c = hd * HEAD_PAD
        k_out[:, c:c + LANES] = (kn * rinv * gk_ref[:, :LANES]).astype(BF16)
        k_out[:, c + LANES:c + HEAD_PAD] = (kr_base * rinv).astype(BF16)
    v_out[...] = kvf[:, MLA_HEADS * MLA_NOPE:].astype(BF16)
    zr_out[...] = _dot(h, w_zr_ref[...]).astype(BF16)


def _mixer_proj(x, pos, *weights):
    m, d = x.shape
    tm = PROJ_TM
    row = lambda i: (i, 0)
    widths = [
        (MLA_HEADS * HEAD_PAD, BF16),
        (MLA_HEADS * HEAD_PAD, BF16),
        (MLA_HEADS * MLA_V, BF16),
        (GLA_HEADS * GLA_DK, BF16),
        (GLA_HEADS * GLA_DK, F32),
        (GLA_HEADS * GLA_DV, BF16),
        (GLA_HEADS * GLA_DK, F32),
        (GLA_HEADS * GLA_DV, BF16),
    ]
    return pl.pallas_call(
        _mixer_proj_body,
        grid=(m // tm,),
        in_specs=[pl.BlockSpec((tm, d), row), pl.BlockSpec((tm, 1), row)]
        + [_resident(w.shape) for w in weights],
        out_specs=[pl.BlockSpec((tm, w), row) for w, _ in widths],
        out_shape=[jax.ShapeDtypeStruct((m, w), dt) for w, dt in widths],
        compiler_params=pltpu.CompilerParams(
            dimension_semantics=("parallel",), vmem_limit_bytes=V7X_VMEM_LIMIT),
        name="mixer_proj",
    )(x, pos, *weights)


def _mla_attn_body(q_ref, k_ref, v_ref, o_ref):
    seq = q_ref.shape[1]
    tq = ATTN_TQ
    r = lax.broadcasted_iota(jnp.int32, (tq, tq), 0) // CHUNK
    c = lax.broadcasted_iota(jnp.int32, (tq, tq), 1) // CHUNK
    diag_mask = c <= r

    def scores(i):
        lo = i * tq
        q = q_ref[0, lo:lo + tq, :]
        s_d = jnp.where(diag_mask, _dot_nt(q, k_ref[0, lo:lo + tq, :]), -jnp.inf)
        s_o = _dot_nt(q, k_ref[0, :lo, :]) if i > 0 else None
        return s_d, s_o

    n_tiles = seq // tq
    nxt = scores(0)
    for i in range(n_tiles):
        lo = i * tq
        s_d, s_o = nxt
        if i + 1 < n_tiles:
            nxt = scores(i + 1)
        m = jnp.max(s_d, axis=-1, keepdims=True)
        if i > 0:
            m = jnp.maximum(m, jnp.max(s_o, axis=-1, keepdims=True))
        p_d = jnp.exp(s_d - m)
        l = jnp.sum(p_d, axis=-1, keepdims=True)
        acc = _dot(p_d.astype(BF16), v_ref[0, lo:lo + tq, :])
        if i > 0:
            p_o = jnp.exp(s_o - m)
            l = l + jnp.sum(p_o, axis=-1, keepdims=True)
            acc = acc + _dot(p_o.astype(BF16), v_ref[0, :lo, :])
        o_ref[0, lo:lo + tq, :] = (acc * (1.0 / l)).astype(o_ref.dtype)


def _mla_attn(q, k, v):
    b, s, _ = q.shape
    return pl.pallas_call(
        _mla_attn_body,
        grid=(b, MLA_HEADS),
        in_specs=[
            pl.BlockSpec((1, s, HEAD_PAD), lambda i, j: (i, 0, j)),
            pl.BlockSpec((1, s, HEAD_PAD), lambda i, j: (i, 0, j)),
            pl.BlockSpec((1, s, MLA_V), lambda i, j: (i, 0, j)),
        ],
        out_specs=pl.BlockSpec((1, s, MLA_V), lambda i, j: (i, 0, j)),
        out_shape=jax.ShapeDtypeStruct((b, s, MLA_HEADS * MLA_V), BF16),
        compiler_params=pltpu.CompilerParams(
            dimension_semantics=("parallel", "parallel"), vmem_limit_bytes=V7X_VMEM_LIMIT),
        name="mla_attn",
    )(q, k, v)


def _gla_body(q_ref, k_ref, v_ref, la_ref, zr_ref, g_ref, o_ref, u_ref, st_ref):
    seq, k_dim = k_ref.shape[1], k_ref.shape[2]
    v_dim = v_ref.shape[2]
    n_chunks = seq // CHUNK
    r = lax.broadcasted_iota(jnp.int32, (CHUNK, CHUNK), 0)
    c = lax.broadcasted_iota(jnp.int32, (CHUNK, CHUNK), 1)
    later = (c > r).astype(BF16)

    b_end = jnp.sum(la_ref[0].reshape(n_chunks, CHUNK, k_dim), axis=1)
    dec = jnp.concatenate([jnp.exp(b_end), jnp.zeros((LANES - n_chunks, k_dim), F32)], axis=0)
    dec_t = dec.T

    for n in range(n_chunks):
        sl = slice(n * CHUNK, (n + 1) * CHUNK)
        g = la_ref[0, sl, :]
        g_hi = g.astype(BF16)
        g_lo = (g - g_hi.astype(F32)).astype(BF16)
        rest2 = _dot(later, jnp.concatenate([g_hi, g_lo], axis=1))
        rest = rest2[:, :k_dim] + rest2[:, k_dim:]
        k_dec = (k_ref[0, sl, :] * jnp.exp(rest)).astype(BF16)
        u_ref[n] = lax.dot_general(k_dec, v_ref[0, sl, :], (((0,), (0,)), ((), ())),
                                   preferred_element_type=F32)

    state = jnp.zeros((k_dim, v_dim), F32)
    for n in range(n_chunks):
        state = state * dec_t[:, n:n + 1] + u_ref[n]
        st_ref[n] = state.astype(BF16)

    for n in range(n_chunks):
        sl = slice(n * CHUNK, (n + 1) * CHUNK)
        o = _dot(q_ref[0, sl, :], st_ref[n])
        zr = zr_ref[0, sl, :].astype(F32)
        o_ref[0, sl, :] = (_rms(o, g_ref[...]) * (zr * jax.nn.sigmoid(zr))).astype(o_ref.dtype)


def _gla(q, k, v, la, zr, g):
    b, s, _ = q.shape
    n_chunks = s // CHUNK
    kspec = pl.BlockSpec((1, s, GLA_DK), lambda i, j: (i, 0, j))
    vspec = pl.BlockSpec((1, s, GLA_DV), lambda i, j: (i, 0, j))
    return pl.pallas_call(
        _gla_body,
        grid=(b, GLA_HEADS),
        in_specs=[kspec, kspec, vspec, kspec, vspec,
                  pl.BlockSpec((1, GLA_DV), lambda i, j: (0, 0))],
        out_specs=vspec,
        out_shape=jax.ShapeDtypeStruct((b, s, GLA_HEADS * GLA_DV), BF16),
        scratch_shapes=[pltpu.VMEM((n_chunks, GLA_DK, GLA_DV), F32),
                        pltpu.VMEM((n_chunks, GLA_DK, GLA_DV), BF16)],
        compiler_params=pltpu.CompilerParams(
            dimension_semantics=("parallel", "parallel"), vmem_limit_bytes=V7X_VMEM_LIMIT),
        name="gla",
    )(q, k, v, la, zr, g)


def _mem_kv_body(m_ref, gn_ref, wk_ref, wv_ref, gk_ref, k_ref, v_ref):
    hm = _rms(m_ref[0], gn_ref[...]).astype(BF16)
    kf = _dot(hm, wk_ref[...])
    for hd in range(MEM_HEADS):
        c = hd * MEM_HEAD_DIM
        k_ref[0, :, c:c + MEM_HEAD_DIM] = _rms(kf[:, c:c + MEM_HEAD_DIM], gk_ref[...]).astype(BF16)
    v_ref[0] = _dot(hm, wv_ref[...]).astype(BF16)


def _mem_kv(mem, gn, wk, wv, gk):
    b, n, d = mem.shape
    w = wk.shape[1]
    ospec = pl.BlockSpec((1, n, w), lambda i: (i, 0, 0))
    return pl.pallas_call(
        _mem_kv_body,
        grid=(b,),
        in_specs=[pl.BlockSpec((1, n, d), lambda i: (i, 0, 0)),
                  _resident(gn.shape), _resident(wk.shape), _resident(wv.shape),
                  _resident(gk.shape)],
        out_specs=[ospec, ospec],
        out_shape=[jax.ShapeDtypeStruct((b, n, w), BF16)] * 2,
        compiler_params=pltpu.CompilerParams(
            dimension_semantics=("parallel",), vmem_limit_bytes=V7X_VMEM_LIMIT),
        name="mem_kv",
    )(mem, gn, wk, wv, gk)


def _out_mem_body(x_ref, om_ref, og_ref, wo_ref, gn_ref, wq_ref, gq_ref, km_ref, vm_ref,
                  wmo_ref, o_ref):
    n_mla = om_ref.shape[2]
    x = x_ref[0] + _dot(om_ref[0], wo_ref[:n_mla, :]) + _dot(og_ref[0], wo_ref[n_mla:, :])
    h = _rms(x, gn_ref[...]).astype(BF16)
    qm = _dot(h, wq_ref[...])
    outs = []
    for hd in range(MEM_HEADS):
        c = hd * MEM_HEAD_DIM
        qn = (_rms(qm[:, c:c + MEM_HEAD_DIM], gq_ref[...]) * (MEM_HEAD_DIM ** -0.5)).astype(BF16)
        s = _dot_nt(qn, km_ref[0, :, c:c + MEM_HEAD_DIM])
        p = jnp.exp(s - jnp.max(s, axis=-1, keepdims=True))
        l = jnp.sum(p, axis=-1, keepdims=True)
        oh = _dot(p.astype(BF16), vm_ref[0, :, c:c + MEM_HEAD_DIM]) * (1.0 / l)
        outs.append(oh.astype(BF16))
    o_ref[0] = x + _dot(jnp.concatenate(outs, axis=1), wmo_ref[...])


def _out_mem(x, om, og, wo, gn, wq, gq, km, vm, wmo):
    b, s, d = x.shape
    tm = OUT_TM
    n_mem, wm = km.shape[1], km.shape[2]
    tok = lambda w: pl.BlockSpec((1, tm, w), lambda i, j: (i, j, 0))
    memspec = pl.BlockSpec((1, n_mem, wm), lambda i, j: (i, 0, 0))
    return pl.pallas_call(
        _out_mem_body,
        grid=(b, s // tm),
        in_specs=[tok(d), tok(om.shape[2]), tok(og.shape[2]),
                  _resident(wo.shape), _resident(gn.shape), _resident(wq.shape),
                  _resident(gq.shape), memspec, memspec, _resident(wmo.shape)],
        out_specs=tok(d),
        out_shape=jax.ShapeDtypeStruct((b, s, d), F32),
        compiler_params=pltpu.CompilerParams(
            dimension_semantics=("parallel", "parallel"), vmem_limit_bytes=V7X_VMEM_LIMIT),
        name="out_mem",
    )(x, om, og, wo, gn, wq, gq, km, vm, wmo)


def _pad_rope(w):
    z = jnp.zeros(w.shape[:-1] + (LANES // 2 - HALF_ROPE,), w.dtype)
    return jnp.concatenate([w[..., :HALF_ROPE], z, w[..., HALF_ROPE:], z], axis=-1)


def _pad_head(w):
    return jnp.concatenate([w[..., :MLA_NOPE], _pad_rope(w[..., MLA_NOPE:])], axis=-1)


def _split_w_in(w_in):
    o = 0
    parts = {}
    for name, size in (("zq", MLA_Q_RANK), ("zkv", MLA_KV_RANK), ("zkr", MLA_ROPE),
                       ("gq", GLA_HEADS * GLA_DK), ("gk", GLA_HEADS * GLA_DK),
                       ("gv", GLA_HEADS * GLA_DV), ("zg", GLA_GATE_RANK),
                       ("zr", GLA_HEADS * GLA_DV)):
        parts[name] = (o, o + size)
        o += size
    cols = lambda a, b: w_in[:, parts[a][0]:parts[b][1]].astype(BF16)
    w_zg = jnp.pad(cols("zg", "zg"), ((0, 0), (0, LANES - GLA_GATE_RANK)))
    return cols("zq", "zkv"), _pad_rope(cols("zkr", "zkr")), cols("gq", "gv"), cols("zr", "zr"), w_zg


def kernel(x, mem, positions, ffn1_norm, ffn1_w_gate, ffn1_w_up, ffn1_w_down, mix_norm, w_in, q_a_norm, w_q_up, kv_a_norm, w_kv_up, mla_q_norm, mla_k_norm, gla_w_gate2, gla_b_gate, gla_out_norm, w_out, mem_attn_norm, mem_norm, mem_w_q, mem_w_k, mem_w_v, mem_w_o, mem_q_norm, mem_k_norm, ffn2_norm, ffn2_w_gate, ffn2_w_up, ffn2_w_down):
    b, s, d = x.shape
    m = b * s
    depth = w_in.shape[0]
    row = lambda v: v.reshape(1, -1)
    bf = lambda w: w.astype(BF16)

    half = jnp.arange(HALF_ROPE, dtype=F32)
    inv_freq = ROPE_THETA ** (-half / HALF_ROPE)
    invf = _pad_rope(jnp.concatenate([inv_freq, inv_freq])).reshape(1, LANES)
    pos = positions.reshape(m, 1)

    xf = x.reshape(m, d)
    for l in range(depth):
        xf = _ffn(xf, row(ffn1_norm[l]), ffn1_w_gate[l], ffn1_w_up[l], ffn1_w_down[l])

        wq = _pad_head(w_q_up[l].reshape(MLA_Q_RANK, MLA_HEADS, MLA_QK)).reshape(MLA_Q_RANK, -1)
        wkv = w_kv_up[l].reshape(MLA_KV_RANK, MLA_HEADS, MLA_NOPE + MLA_V)
        wkv = jnp.concatenate([wkv[..., :MLA_NOPE].reshape(MLA_KV_RANK, -1),
                               wkv[..., MLA_NOPE:].reshape(MLA_KV_RANK, -1)], axis=1)
        w2 = jnp.pad(gla_w_gate2[l], ((0, LANES - GLA_GATE_RANK), (0, 0)))
        q, k, v, gq, gk, gv, la, zr = _mixer_proj(
            xf, pos, row(mix_norm[l]), *_split_w_in(w_in[l]), row(q_a_norm[l]), bf(wq),
            row(kv_a_norm[l]), bf(wkv), row(_pad_head(mla_q_norm[l])), row(_pad_head(mla_k_norm[l])),
            invf, bf(w2), row(gla_b_gate[l]))

        r3 = lambda a: a.reshape(b, s, a.shape[-1])
        o_mla = _mla_attn(r3(q), r3(k), r3(v))
        o_gla = _gla(r3(gq), r3(gk), r3(gv), r3(la), r3(zr), row(gla_out_norm[l]))

        km, vm = _mem_kv(mem, row(mem_norm[l]), bf(mem_w_k[l]), bf(mem_w_v[l]), row(mem_k_norm[l]))
        x3 = _out_mem(r3(xf), o_mla, o_gla, bf(w_out[l]), row(mem_attn_norm[l]), bf(mem_w_q[l]),
                      row(mem_q_norm[l]), km, vm, bf(mem_w_o[l]))

        xf = _ffn(x3.reshape(m, d), row(ffn2_norm[l]), ffn2_w_gate[l], ffn2_w_up[l], ffn2_w_down[l])
    return xf.reshape(b, s, d)
```

```python
import jax
import jax.numpy as jnp
from jax import lax
from jax.experimental import pallas as pl
from jax.experimental.pallas import tpu as pltpu

F32 = jnp.float32
BF16 = jnp.bfloat16

EPS = 1e-6
CHUNK = 64
ROPE_THETA = 10000.0
GLA_TAU = 16.0

MLA_HEADS = 8
MLA_NOPE = 128
MLA_ROPE = 64
MLA_QK = MLA_NOPE + MLA_ROPE
MLA_V = 128
MLA_Q_RANK = 512
MLA_KV_RANK = 256
GLA_HEADS = 4
GLA_DK = 128
GLA_DV = 256
GLA_GATE_RANK = 16
MEM_HEADS = 4
MEM_HEAD_DIM = 128

LANES = 128
HEAD_PAD = 2 * LANES
HALF_ROPE = MLA_ROPE // 2

V7X_VMEM_LIMIT = 56 * 1024 * 1024

FFN_TM = 1024
FFN_TF = 512
FFN_TN = 512
PROJ_TM = 512
ATTN_TQ = 256
GLA_GROUP = 256
OUT_TM = 512


def _rms(x, g):
    ms = jnp.mean(x * x, axis=-1, keepdims=True)
    return x * lax.rsqrt(ms + EPS) * g


def _dot(a, b):
    return jnp.dot(a, b, preferred_element_type=F32)


def _dot_nt(a, b):
    return lax.dot_general(a, b, (((1,), (1,)), ((), ())), preferred_element_type=F32)


def _resident(shape):
    nd = len(shape)
    return pl.BlockSpec(shape, lambda *_: (0,) * nd, pipeline_mode=pl.Buffered(1))


def _ffn_body(x_hbm, g_ref, wg_ref, wu_ref, wd_ref, o_hbm, xbuf, acc, h_ref, x_sem, o_sem):
    i, f = pl.program_id(0), pl.program_id(1)
    n_i, n_f = pl.num_programs(0), pl.num_programs(1)
    tm = acc.shape[0]

    def rows(tile):
        return pl.ds(pl.multiple_of(tile * tm, tm), tm)

    def x_copy(tile):
        return pltpu.make_async_copy(x_hbm.at[rows(tile), :], xbuf, x_sem)

    def o_copy(tile):
        return pltpu.make_async_copy(acc, o_hbm.at[rows(tile), :], o_sem)

    @pl.when(f == 0)
    def _():
        @pl.when(i == 0)
        def _():
            x_copy(0).start()

        x_copy(i).wait()
        h_ref[...] = _rms(xbuf[...], g_ref[...]).astype(BF16)

        @pl.when(i > 0)
        def _():
            o_copy(i - 1).wait()

        acc[...] = xbuf[...]

        @pl.when(i + 1 < n_i)
        def _():
            x_copy(i + 1).start()

    h = h_ref[...]
    gate = _dot(h, wg_ref[...].astype(BF16))
    up = _dot(h, wu_ref[...].astype(BF16))
    a = ((0.5 * gate) * jax.nn.sigmoid(gate) * up).astype(BF16)
    for c in range(0, acc.shape[1], FFN_TN):
        acc[:, c:c + FFN_TN] += _dot(a, wd_ref[:, c:c + FFN_TN].astype(BF16))

    @pl.when(f == n_f - 1)
    def _():
        o_copy(i).start()

        @pl.when(i == n_i - 1)
        def _():
            o_copy(i).wait()


def _ffn(x, g, wg, wu, wd):
    m, d = x.shape
    f = wg.shape[1]
    return pl.pallas_call(
        _ffn_body,
        grid=(m // FFN_TM, f // FFN_TF),
        in_specs=[
            pl.BlockSpec(memory_space=pl.ANY),
            pl.BlockSpec((1, d), lambda i, j: (0, 0)),
            pl.BlockSpec((d, FFN_TF), lambda i, j: (0, j)),
            pl.BlockSpec((d, FFN_TF), lambda i, j: (0, j)),
            pl.BlockSpec((FFN_TF, d), lambda i, j: (j, 0)),
        ],
        out_specs=pl.BlockSpec(memory_space=pl.ANY),
        out_shape=jax.ShapeDtypeStruct((m, d), F32),
        scratch_shapes=[pltpu.VMEM((FFN_TM, d), F32),
                        pltpu.VMEM((FFN_TM, d), F32),
                        pltpu.VMEM((FFN_TM, d), BF16),
                        pltpu.SemaphoreType.DMA(()),
                        pltpu.SemaphoreType.DMA(())],
        compiler_params=pltpu.CompilerParams(
            dimension_semantics=("arbitrary", "arbitrary"),
            vmem_limit_bytes=V7X_VMEM_LIMIT),
        name="ffn",
    )(x, g, wg, wu, wd)


def _mixer_proj_body(x_ref, pos_ref, gmix_ref, w_lat_ref, w_kr_ref, w_gla_ref, w_zr_ref, w_zg_ref,
                     gqa_ref, wq_ref, gkva_ref, wkv_ref, gq_ref, gk_ref, invf_ref, w2_ref, b2_ref,
                     q_out, k_out, v_out, gq_out, gk_out, gv_out, la_out, zr_out):
    h = _rms(x_ref[...], gmix_ref[...]).astype(BF16)
    n_qk = GLA_HEADS * GLA_DK
    zq = _dot(h, w_lat_ref[:, :MLA_Q_RANK])
    zkv = _dot(h, w_lat_ref[:, MLA_Q_RANK:])
    zkr = _dot(h, w_kr_ref[...])
    zg = _dot(h, w_zg_ref[...]).astype(BF16)
    gq_out[...] = (_dot(h, w_gla_ref[:, :n_qk]) * (GLA_DK ** -0.5)).astype(BF16)

    pre = _dot(zg, w2_ref[...]) + b2_ref[...]
    log_sig = jnp.minimum(pre, 0.0) - jnp.log1p(jnp.exp(-jnp.abs(pre)))
    la_out[...] = log_sig * (1.0 / GLA_TAU)

    ang = pos_ref[...].astype(F32) * invf_ref[...]
    lane = lax.broadcasted_iota(jnp.int32, (1, LANES), 1)
    cos_t = jnp.cos(ang)
    sin_t = jnp.sin(ang) * jnp.where(lane < LANES // 2, -1.0, 1.0)

    def rope(v):
        return v * cos_t + pltpu.roll(v, LANES // 2, axis=1) * sin_t

    inv_qk = 1.0 / MLA_QK

    qa = _rms(zq, gqa_ref[...]).astype(BF16)
    kva = _rms(zkv, gkva_ref[...]).astype(BF16)
    qf = _dot(qa, wq_ref[...])
    kvf = _dot(kva, wkv_ref[...])
    gk_out[...] = _dot(h, w_gla_ref[:, n_qk:2 * n_qk])
    for hd in range(MLA_HEADS):
        c = hd * HEAD_PAD
        blk = qf[:, c:c + HEAD_PAD]
        ss = jnp.sum(blk * blk, axis=-1, keepdims=True)
        rinv = lax.rsqrt(ss * inv_qk + EPS) * (MLA_QK ** -0.5)
        q_out[:, c:c + LANES] = (blk[:, :LANES] * rinv * gq_ref[:, :LANES]).astype(BF16)
        q_out[:, c + LANES:c + HEAD_PAD] = rope(
            blk[:, LANES:] * rinv * gq_ref[:, LANES:]).astype(BF16)

    gv_out[...] = _dot(h, w_gla_ref[:, 2 * n_qk:]).astype(BF16)

    ssr = jnp.sum(zkr * zkr, axis=-1, keepdims=True)
    kr_base = rope(zkr * gk_ref[:, LANES:])
    for hd in range(MLA_HEADS):
        kn = kvf[:, hd * MLA_NOPE:(hd + 1) * MLA_NOPE]
        ss = jnp.sum(kn * kn, axis=-1, keepdims=True) + ssr
        rinv = lax.rsqrt(ss * inv_qk + EPS)
        c = hd * HEAD_PAD
        k_out[:, c:c + LANES] = (kn * rinv * gk_ref[:, :LANES]).astype(BF16)
        k_out[:, c + LANES:c + HEAD_PAD] = (kr_base * rinv).astype(BF16)
    v_out[...] = kvf[:, MLA_HEADS * MLA_NOPE:].astype(BF16)
    zr_out[...] = _dot(h, w_zr_ref[...]).astype(BF16)


def _mixer_proj(x, pos, *weights):
    m, d = x.shape
    tm = PROJ_TM
    row = lambda i: (i, 0)
    widths = [
        (MLA_HEADS * HEAD_PAD, BF16),
        (MLA_HEADS * HEAD_PAD, BF16),
        (MLA_HEADS * MLA_V, BF16),
        (GLA_HEADS * GLA_DK, BF16),
        (GLA_HEADS * GLA_DK, F32),
        (GLA_HEADS * GLA_DV, BF16),
        (GLA_HEADS * GLA_DK, F32),
        (GLA_HEADS * GLA_DV, BF16),
    ]
    return pl.pallas_call(
        _mixer_proj_body,
        grid=(m // tm,),
        in_specs=[pl.BlockSpec((tm, d), row), pl.BlockSpec((tm, 1), row)]
        + [_resident(w.shape) for w in weights],
        out_specs=[pl.BlockSpec((tm, w), row) for w, _ in widths],
        out_shape=[jax.ShapeDtypeStruct((m, w), dt) for w, dt in widths],
        compiler_params=pltpu.CompilerParams(
            dimension_semantics=("parallel",), vmem_limit_bytes=V7X_VMEM_LIMIT),
        name="mixer_proj",
    )(x, pos, *weights)


def _mla_attn_body(q_ref, k_ref, v_ref, o_ref):
    seq = q_ref.shape[1]
    tq = ATTN_TQ
    r = lax.broadcasted_iota(jnp.int32, (tq, tq), 0) // CHUNK
    c = lax.broadcasted_iota(jnp.int32, (tq, tq), 1) // CHUNK
    diag_mask = c <= r

    def scores(i):
        lo = i * tq
        q = q_ref[0, lo:lo + tq, :]
        s_d = jnp.where(diag_mask, _dot_nt(q, k_ref[0, lo:lo + tq, :]), -jnp.inf)
        s_o = _dot_nt(q, k_ref[0, :lo, :]) if i > 0 else None
        return s_d, s_o

    n_tiles = seq // tq
    nxt = scores(0)
    for i in range(n_tiles):
        lo = i * tq
        s_d, s_o = nxt
        if i + 1 < n_tiles:
            nxt = scores(i + 1)
        m = jnp.max(s_d, axis=-1, keepdims=True)
        if i > 0:
            m = jnp.maximum(m, jnp.max(s_o, axis=-1, keepdims=True))
        p_d = jnp.exp(s_d - m)
        l = jnp.sum(p_d, axis=-1, keepdims=True)
        acc = _dot(p_d.astype(BF16), v_ref[0, lo:lo + tq, :])
        if i > 0:
            p_o = jnp.exp(s_o - m)
            l = l + jnp.sum(p_o, axis=-1, keepdims=True)
            acc = acc + _dot(p_o.astype(BF16), v_ref[0, :lo, :])
        o_ref[0, lo:lo + tq, :] = (acc * (1.0 / l)).astype(o_ref.dtype)


def _mla_attn(q, k, v):
    b, s, _ = q.shape
    return pl.pallas_call(
        _mla_attn_body,
        grid=(b, MLA_HEADS),
        in_specs=[
            pl.BlockSpec((1, s, HEAD_PAD), lambda i, j: (i, 0, j)),
            pl.BlockSpec((1, s, HEAD_PAD), lambda i, j: (i, 0, j)),
            pl.BlockSpec((1, s, MLA_V), lambda i, j: (i, 0, j)),
        ],
        out_specs=pl.BlockSpec((1, s, MLA_V), lambda i, j: (i, 0, j)),
        out_shape=jax.ShapeDtypeStruct((b, s, MLA_HEADS * MLA_V), BF16),
        compiler_params=pltpu.CompilerParams(
            dimension_semantics=("parallel", "parallel"), vmem_limit_bytes=V7X_VMEM_LIMIT),
        name="mla_attn",
    )(q, k, v)


def _gla_body(q_ref, k_ref, v_ref, la_ref, zr_ref, g_ref, o_ref, u_ref, st_ref, kd_ref, oraw_ref):
    seq, k_dim = k_ref.shape[1], k_ref.shape[2]
    v_dim = v_ref.shape[2]
    n_chunks = seq // CHUNK
    grp = GLA_GROUP
    r = lax.broadcasted_iota(jnp.int32, (grp, grp), 0)
    c = lax.broadcasted_iota(jnp.int32, (grp, grp), 1)
    later = ((c > r) & (c // CHUNK == r // CHUNK)).astype(BF16)

    b_end = jnp.sum(la_ref[0].reshape(n_chunks, CHUNK, k_dim), axis=1)
    dec = jnp.concatenate([jnp.exp(b_end), jnp.zeros((LANES - n_chunks, k_dim), F32)], axis=0)
    dec_t = dec.T

    for n in range(seq // grp):
        sl = slice(n * grp, (n + 1) * grp)
        g = la_ref[0, sl, :]
        g_hi = g.astype(BF16)
        g_lo = (g - g_hi.astype(F32)).astype(BF16)
        rest2 = _dot(later, jnp.concatenate([g_hi, g_lo], axis=1))
        rest = rest2[:, :k_dim] + rest2[:, k_dim:]
        kd_ref[sl, :] = (k_ref[0, sl, :] * jnp.exp(rest)).astype(BF16)

    for n in range(n_chunks):
        sl = slice(n * CHUNK, (n + 1) * CHUNK)
        u_ref[n] = lax.dot_general(kd_ref[sl, :], v_ref[0, sl, :], (((0,), (0,)), ((), ())),
                                   preferred_element_type=F32)

    state = jnp.zeros((k_dim, v_dim), F32)
    for n in range(n_chunks):
        state = state * dec_t[:, n:n + 1] + u_ref[n]
        st_ref[n] = state.astype(BF16)

    for n in range(n_chunks):
        sl = slice(n * CHUNK, (n + 1) * CHUNK)
        oraw_ref[sl, :] = _dot(q_ref[0, sl, :], st_ref[n])

    for n in range(seq // grp):
        sl = slice(n * grp, (n + 1) * grp)
        zr = zr_ref[0, sl, :].astype(F32)
        o_ref[0, sl, :] = (_rms(oraw_ref[sl, :], g_ref[...])
                           * (zr * jax.nn.sigmoid(zr))).astype(o_ref.dtype)


def _gla(q, k, v, la, zr, g):
    b, s, _ = q.shape
    n_chunks = s // CHUNK
    kspec = pl.BlockSpec((1, s, GLA_DK), lambda i, j: (i, 0, j))
    vspec = pl.BlockSpec((1, s, GLA_DV), lambda i, j: (i, 0, j))
    return pl.pallas_call(
        _gla_body,
        grid=(b, GLA_HEADS),
        in_specs=[kspec, kspec, vspec, kspec, vspec,
                  pl.BlockSpec((1, GLA_DV), lambda i, j: (0, 0))],
        out_specs=vspec,
        out_shape=jax.ShapeDtypeStruct((b, s, GLA_HEADS * GLA_DV), BF16),
        scratch_shapes=[pltpu.VMEM((n_chunks, GLA_DK, GLA_DV), F32),
                        pltpu.VMEM((n_chunks, GLA_DK, GLA_DV), BF16),
                        pltpu.VMEM((s, GLA_DK), BF16),
                        pltpu.VMEM((s, GLA_DV), F32)],
        compiler_params=pltpu.CompilerParams(
            dimension_semantics=("parallel", "parallel"), vmem_limit_bytes=V7X_VMEM_LIMIT),
        name="gla",
    )(q, k, v, la, zr, g)


def _mem_kv_body(m_ref, gn_ref, wk_ref, wv_ref, gk_ref, k_ref, v_ref):
    hm = _rms(m_ref[0], gn_ref[...]).astype(BF16)
    kf = _dot(hm, wk_ref[...])
    for hd in range(MEM_HEADS):
        c = hd * MEM_HEAD_DIM
        k_ref[0, :, c:c + MEM_HEAD_DIM] = _rms(kf[:, c:c + MEM_HEAD_DIM], gk_ref[...]).astype(BF16)
    v_ref[0] = _dot(hm, wv_ref[...]).astype(BF16)


def _mem_kv(mem, gn, wk, wv, gk):
    b, n, d = mem.shape
    w = wk.shape[1]
    ospec = pl.BlockSpec((1, n, w), lambda i: (i, 0, 0))
    return pl.pallas_call(
        _mem_kv_body,
        grid=(b,),
        in_specs=[pl.BlockSpec((1, n, d), lambda i: (i, 0, 0)),
                  _resident(gn.shape), _resident(wk.shape), _resident(wv.shape),
                  _resident(gk.shape)],
        out_specs=[ospec, ospec],
        out_shape=[jax.ShapeDtypeStruct((b, n, w), BF16)] * 2,
        compiler_params=pltpu.CompilerParams(
            dimension_semantics=("parallel",), vmem_limit_bytes=V7X_VMEM_LIMIT),
        name="mem_kv",
    )(mem, gn, wk, wv, gk)


def _out_mem_body(x_ref, om_ref, og_ref, wo_ref, gn_ref, wq_ref, gq_ref, km_ref, vm_ref,
                  wmo_ref, o_ref):
    n_mla = om_ref.shape[2]
    x = x_ref[0] + _dot(om_ref[0], wo_ref[:n_mla, :]) + _dot(og_ref[0], wo_ref[n_mla:, :])
    h = _rms(x, gn_ref[...]).astype(BF16)
    qm = _dot(h, wq_ref[...])
    outs = []
    for hd in range(MEM_HEADS):
        c = hd * MEM_HEAD_DIM
        qn = (_rms(qm[:, c:c + MEM_HEAD_DIM], gq_ref[...]) * (MEM_HEAD_DIM ** -0.5)).astype(BF16)
        s = _dot_nt(qn, km_ref[0, :, c:c + MEM_HEAD_DIM])
        p = jnp.exp(s - jnp.max(s, axis=-1, keepdims=True))
        l = jnp.sum(p, axis=-1, keepdims=True)
        oh = _dot(p.astype(BF16), vm_ref[0, :, c:c + MEM_HEAD_DIM]) * (1.0 / l)
        outs.append(oh.astype(BF16))
    o_ref[0] = x + _dot(jnp.concatenate(outs, axis=1), wmo_ref[...])


def _out_mem(x, om, og, wo, gn, wq, gq, km, vm, wmo):
    b, s, d = x.shape
    tm = OUT_TM
    n_mem, wm = km.shape[1], km.shape[2]
    tok = lambda w: pl.BlockSpec((1, tm, w), lambda i, j: (i, j, 0))
    memspec = pl.BlockSpec((1, n_mem, wm), lambda i, j: (i, 0, 0))
    return pl.pallas_call(
        _out_mem_body,
        grid=(b, s // tm),
        in_specs=[tok(d), tok(om.shape[2]), tok(og.shape[2]),
                  _resident(wo.shape), _resident(gn.shape), _resident(wq.shape),
                  _resident(gq.shape), memspec, memspec, _resident(wmo.shape)],
        out_specs=tok(d),
        out_shape=jax.ShapeDtypeStruct((b, s, d), F32),
        compiler_params=pltpu.CompilerParams(
            dimension_semantics=("parallel", "parallel"), vmem_limit_bytes=V7X_VMEM_LIMIT),
        name="out_mem",
    )(x, om, og, wo, gn, wq, gq, km, vm, wmo)


def _pad_rope(w):
    z = jnp.zeros(w.shape[:-1] + (LANES // 2 - HALF_ROPE,), w.dtype)
    return jnp.concatenate([w[..., :HALF_ROPE], z, w[..., HALF_ROPE:], z], axis=-1)


def _pad_head(w):
    return jnp.concatenate([w[..., :MLA_NOPE], _pad_rope(w[..., MLA_NOPE:])], axis=-1)


def _split_w_in(w_in):
    o = 0
    parts = {}
    for name, size in (("zq", MLA_Q_RANK), ("zkv", MLA_KV_RANK), ("zkr", MLA_ROPE),
                       ("gq", GLA_HEADS * GLA_DK), ("gk", GLA_HEADS * GLA_DK),
                       ("gv", GLA_HEADS * GLA_DV), ("zg", GLA_GATE_RANK),
                       ("zr", GLA_HEADS * GLA_DV)):
        parts[name] = (o, o + size)
        o += size
    w_bf = w_in.astype(BF16)
    cols = lambda a, b: w_bf[:, parts[a][0]:parts[b][1]]
    w_zg = jnp.pad(cols("zg", "zg"), ((0, 0), (0, LANES - GLA_GATE_RANK)))
    return cols("zq", "zkv"), _pad_rope(cols("zkr", "zkr")), cols("gq", "gv"), cols("zr", "zr"), w_zg


def kernel(x, mem, positions, ffn1_norm, ffn1_w_gate, ffn1_w_up, ffn1_w_down, mix_norm, w_in, q_a_norm, w_q_up, kv_a_norm, w_kv_up, mla_q_norm, mla_k_norm, gla_w_gate2, gla_b_gate, gla_out_norm, w_out, mem_attn_norm, mem_norm, mem_w_q, mem_w_k, mem_w_v, mem_w_o, mem_q_norm, mem_k_norm, ffn2_norm, ffn2_w_gate, ffn2_w_up, ffn2_w_down):
    b, s, d = x.shape
    m = b * s
    depth = w_in.shape[0]
    row = lambda v: v.reshape(1, -1)
    bf = lambda w: w.astype(BF16)

    half = jnp.arange(HALF_ROPE, dtype=F32)
    inv_freq = ROPE_THETA ** (-half / HALF_ROPE)
    invf = _pad_rope(jnp.concatenate([inv_freq, inv_freq])).reshape(1, LANES)
    pos = positions.reshape(m, 1)

    xf = x.reshape(m, d)
    for l in range(depth):
        xf = _ffn(xf, row(ffn1_norm[l]), ffn1_w_gate[l], ffn1_w_up[l], ffn1_w_down[l])

        wq = _pad_head(w_q_up[l].reshape(MLA_Q_RANK, MLA_HEADS, MLA_QK)).reshape(MLA_Q_RANK, -1)
        wkv = w_kv_up[l].reshape(MLA_KV_RANK, MLA_HEADS, MLA_NOPE + MLA_V)
        wkv = jnp.concatenate([wkv[..., :MLA_NOPE].reshape(MLA_KV_RANK, -1),
                               wkv[..., MLA_NOPE:].reshape(MLA_KV_RANK, -1)], axis=1)
        w2 = jnp.pad(gla_w_gate2[l], ((0, LANES - GLA_GATE_RANK), (0, 0)))
        q, k, v, gq, gk, gv, la, zr = _mixer_proj(
            xf, pos, row(mix_norm[l]), *_split_w_in(w_in[l]), row(q_a_norm[l]), bf(wq),
            row(kv_a_norm[l]), bf(wkv), row(_pad_head(mla_q_norm[l])), row(_pad_head(mla_k_norm[l])),
            invf, bf(w2), row(gla_b_gate[l]))

        r3 = lambda a: a.reshape(b, s, a.shape[-1])
        o_mla = _mla_attn(r3(q), r3(k), r3(v))
        o_gla = _gla(r3(gq), r3(gk), r3(gv), r3(la), r3(zr), row(gla_out_norm[l]))

        km, vm = _mem_kv(mem, row(mem_norm[l]), bf(mem_w_k[l]), bf(mem_w_v[l]), row(mem_k_norm[l]))
        x3 = _out_mem(r3(xf), o_mla, o_gla, bf(w_out[l]), row(mem_attn_norm[l]), bf(mem_w_q[l]),
                      row(mem_q_norm[l]), km, vm, bf(mem_w_o[l]))

        xf = _ffn(x3.reshape(m, d), row(ffn2_norm[l]), ffn2_w_gate[l], ffn2_w_up[l], ffn2_w_down[l])
    return xf.reshape(b, s, d)
```

```python
import jax
import jax.numpy as jnp
from jax import lax
from jax.experimental import pallas as pl
from jax.experimental.pallas import tpu as pltpu

F32 = jnp.float32
BF16 = jnp.bfloat16

EPS = 1e-6
CHUNK = 64
ROPE_THETA = 10000.0
GLA_TAU = 16.0
LOG2_E = 1.4426950408889634

MLA_HEADS = 8
MLA_NOPE = 128
MLA_ROPE = 64
MLA_QK = MLA_NOPE + MLA_ROPE
MLA_V = 128
MLA_Q_RANK = 512
MLA_KV_RANK = 256
GLA_HEADS = 4
GLA_DK = 128
GLA_DV = 256
GLA_GATE_RANK = 16
MEM_HEADS = 4
MEM_HEAD_DIM = 128

LANES = 128
HEAD_PAD = 2 * LANES
HALF_ROPE = MLA_ROPE // 2

V7X_VMEM_LIMIT = 56 * 1024 * 1024

FFN_TM = 1024
FFN_TF = 512
FFN_TN = 512
PROJ_TM = 512
ATTN_TQ = 256
GLA_GROUP = 256
OUT_TM = 512


def _rms(x, g):
    ms = jnp.mean(x * x, axis=-1, keepdims=True)
    return x * lax.rsqrt(ms + EPS) * g


def _dot(a, b):
    return jnp.dot(a, b, preferred_element_type=F32)


def _dot_nt(a, b):
    return lax.dot_general(a, b, (((1,), (1,)), ((), ())), preferred_element_type=F32)


def _resident(shape):
    nd = len(shape)
    return pl.BlockSpec(shape, lambda *_: (0,) * nd, pipeline_mode=pl.Buffered(1))


def _ffn_body(x_hbm, g_ref, wg_ref, wu_ref, wd_ref, o_hbm, xbuf, acc, h_ref, x_sem, o_sem):
    i, f = pl.program_id(0), pl.program_id(1)
    n_i, n_f = pl.num_programs(0), pl.num_programs(1)
    tm = acc.shape[0]

    def rows(tile):
        return pl.ds(pl.multiple_of(tile * tm, tm), tm)

    def x_copy(tile):
        return pltpu.make_async_copy(x_hbm.at[rows(tile), :], xbuf, x_sem)

    def o_copy(tile):
        return pltpu.make_async_copy(acc, o_hbm.at[rows(tile), :], o_sem)

    @pl.when(f == 0)
    def _():
        @pl.when(i == 0)
        def _():
            x_copy(0).start()

        x_copy(i).wait()
        h_ref[...] = _rms(xbuf[...], g_ref[...]).astype(BF16)

        @pl.when(i > 0)
        def _():
            o_copy(i - 1).wait()

        acc[...] = xbuf[...]

        @pl.when(i + 1 < n_i)
        def _():
            x_copy(i + 1).start()

    h = h_ref[...]
    gate = _dot(h, wg_ref[...].astype(BF16))
    up = _dot(h, wu_ref[...].astype(BF16))
    a = ((0.5 * gate) * jax.nn.sigmoid(gate) * up).astype(BF16)
    for c in range(0, acc.shape[1], FFN_TN):
        acc[:, c:c + FFN_TN] += _dot(a, wd_ref[:, c:c + FFN_TN].astype(BF16))

    @pl.when(f == n_f - 1)
    def _():
        o_copy(i).start()

        @pl.when(i == n_i - 1)
        def _():
            o_copy(i).wait()


def _ffn(x, g, wg, wu, wd):
    m, d = x.shape
    f = wg.shape[1]
    return pl.pallas_call(
        _ffn_body,
        grid=(m // FFN_TM, f // FFN_TF),
        in_specs=[
            pl.BlockSpec(memory_space=pl.ANY),
            pl.BlockSpec((1, d), lambda i, j: (0, 0)),
            pl.BlockSpec((d, FFN_TF), lambda i, j: (0, j)),
            pl.BlockSpec((d, FFN_TF), lambda i, j: (0, j)),
            pl.BlockSpec((FFN_TF, d), lambda i, j: (j, 0)),
        ],
        out_specs=pl.BlockSpec(memory_space=pl.ANY),
        out_shape=jax.ShapeDtypeStruct((m, d), F32),
        scratch_shapes=[pltpu.VMEM((FFN_TM, d), F32),
                        pltpu.VMEM((FFN_TM, d), F32),
                        pltpu.VMEM((FFN_TM, d), BF16),
                        pltpu.SemaphoreType.DMA(()),
                        pltpu.SemaphoreType.DMA(())],
        compiler_params=pltpu.CompilerParams(
            dimension_semantics=("arbitrary", "arbitrary"),
            vmem_limit_bytes=V7X_VMEM_LIMIT),
        name="ffn",
    )(x, g, wg, wu, wd)


def _mixer_proj_body(x_ref, pos_ref, gmix_ref, w_lat_ref, w_kr_ref, w_gla_ref, w_zr_ref, w_zg_ref,
                     gqa_ref, wq_ref, gkva_ref, wkv_ref, gq_ref, gk_ref, invf_ref, w2_ref, b2_ref,
                     q_out, k_out, v_out, gq_out, gk_out, gv_out, la_out, zr_out):
    h = _rms(x_ref[...], gmix_ref[...]).astype(BF16)
    n_qk = GLA_HEADS * GLA_DK
    zq = _dot(h, w_lat_ref[:, :MLA_Q_RANK])
    zkv = _dot(h, w_lat_ref[:, MLA_Q_RANK:])
    zkr = _dot(h, w_kr_ref[...])
    zg = _dot(h, w_zg_ref[...]).astype(BF16)
    gq_out[...] = (_dot(h, w_gla_ref[:, :n_qk]) * (GLA_DK ** -0.5)).astype(BF16)

    pre = _dot(zg, w2_ref[...]) + b2_ref[...]
    log_sig = jnp.minimum(pre, 0.0) - jnp.log1p(jnp.exp(-jnp.abs(pre)))
    la_out[...] = log_sig * (1.0 / GLA_TAU)

    ang = pos_ref[...].astype(F32) * invf_ref[...]
    lane = lax.broadcasted_iota(jnp.int32, (1, LANES), 1)
    cos_t = jnp.cos(ang)
    sin_t = jnp.sin(ang) * jnp.where(lane < LANES // 2, -1.0, 1.0)

    def rope(v):
        return v * cos_t + pltpu.roll(v, LANES // 2, axis=1) * sin_t

    inv_qk = 1.0 / MLA_QK

    qa = _rms(zq, gqa_ref[...]).astype(BF16)
    kva = _rms(zkv, gkva_ref[...]).astype(BF16)
    qf = _dot(qa, wq_ref[...])
    kvf = _dot(kva, wkv_ref[...])
    gk_out[...] = _dot(h, w_gla_ref[:, n_qk:2 * n_qk])
    for hd in range(MLA_HEADS):
        c = hd * HEAD_PAD
        blk = qf[:, c:c + HEAD_PAD]
        ss = jnp.sum(blk * blk, axis=-1, keepdims=True)
        rinv = lax.rsqrt(ss * inv_qk + EPS) * (MLA_QK ** -0.5 * LOG2_E)
        q_out[:, c:c + LANES] = (blk[:, :LANES] * rinv * gq_ref[:, :LANES]).astype(BF16)
        q_out[:, c + LANES:c + HEAD_PAD] = rope(
            blk[:, LANES:] * rinv * gq_ref[:, LANES:]).astype(BF16)

    gv_out[...] = _dot(h, w_gla_ref[:, 2 * n_qk:]).astype(BF16)

    ssr = jnp.sum(zkr * zkr, axis=-1, keepdims=True)
    kr_base = rope(zkr * gk_ref[:, LANES:])
    for hd in range(MLA_HEADS):
        kn = kvf[:, hd * MLA_NOPE:(hd + 1) * MLA_NOPE]
        ss = jnp.sum(kn * kn, axis=-1, keepdims=True) + ssr
        rinv = lax.rsqrt(ss * inv_qk + EPS)
        c = hd * HEAD_PAD
        k_out[:, c:c + LANES] = (kn * rinv * gk_ref[:, :LANES]).astype(BF16)
        k_out[:, c + LANES:c + HEAD_PAD] = (kr_base * rinv).astype(BF16)
    v_out[...] = kvf[:, MLA_HEADS * MLA_NOPE:].astype(BF16)
    zr_out[...] = _dot(h, w_zr_ref[...]).astype(BF16)


def _mixer_proj(x, pos, *weights):
    m, d = x.shape
    tm = PROJ_TM
    row = lambda i: (i, 0)
    widths = [
        (MLA_HEADS * HEAD_PAD, BF16),
        (MLA_HEADS * HEAD_PAD, BF16),
        (MLA_HEADS * MLA_V, BF16),
        (GLA_HEADS * GLA_DK, BF16),
        (GLA_HEADS * GLA_DK, F32),
        (GLA_HEADS * GLA_DV, BF16),
        (GLA_HEADS * GLA_DK, F32),
        (GLA_HEADS * GLA_DV, BF16),
    ]
    return pl.pallas_call(
        _mixer_proj_body,
        grid=(m // tm,),
        in_specs=[pl.BlockSpec((tm, d), row), pl.BlockSpec((tm, 1), row)]
        + [_resident(w.shape) for w in weights],
        out_specs=[pl.BlockSpec((tm, w), row) for w, _ in widths],
        out_shape=[jax.ShapeDtypeStruct((m, w), dt) for w, dt in widths],
        compiler_params=pltpu.CompilerParams(
            dimension_semantics=("parallel",), vmem_limit_bytes=V7X_VMEM_LIMIT),
        name="mixer_proj",
    )(x, pos, *weights)


def _mla_attn_body(q_ref, k_ref, v_ref, o_ref, vext_ref):
    seq = q_ref.shape[1]
    dv = v_ref.shape[2]
    tq = ATTN_TQ
    r = lax.broadcasted_iota(jnp.int32, (tq, tq), 0) // CHUNK
    c = lax.broadcasted_iota(jnp.int32, (tq, tq), 1) // CHUNK
    diag_mask = c <= r

    vext_ref[:, :dv] = v_ref[0]
    vext_ref[:, dv:] = jnp.ones((seq, vext_ref.shape[1] - dv), vext_ref.dtype)

    def scores(i):
        lo = i * tq
        q = q_ref[0, lo:lo + tq, :]
        s_d = jnp.where(diag_mask, _dot_nt(q, k_ref[0, lo:lo + tq, :]), -jnp.inf)
        s_o = _dot_nt(q, k_ref[0, :lo, :]) if i > 0 else None
        return s_d, s_o

    n_tiles = seq // tq
    nxt = scores(0)
    for i in range(n_tiles):
        lo = i * tq
        s_d, s_o = nxt
        if i + 1 < n_tiles:
            nxt = scores(i + 1)
        m = jnp.max(s_d, axis=-1, keepdims=True)
        if i > 0:
            m = jnp.maximum(m, jnp.max(s_o, axis=-1, keepdims=True))
        pv = _dot(jnp.exp2(s_d - m).astype(BF16), vext_ref[lo:lo + tq, :])
        if i > 0:
            pv = pv + _dot(jnp.exp2(s_o - m).astype(BF16), vext_ref[:lo, :])
        o_ref[0, lo:lo + tq, :] = (pv[:, :dv] * (1.0 / pv[:, dv:dv + 1])).astype(o_ref.dtype)


def _mla_attn(q, k, v):
    b, s, _ = q.shape
    return pl.pallas_call(
        _mla_attn_body,
        grid=(b, MLA_HEADS),
        in_specs=[
            pl.BlockSpec((1, s, HEAD_PAD), lambda i, j: (i, 0, j)),
            pl.BlockSpec((1, s, HEAD_PAD), lambda i, j: (i, 0, j)),
            pl.BlockSpec((1, s, MLA_V), lambda i, j: (i, 0, j)),
        ],
        out_specs=pl.BlockSpec((1, s, MLA_V), lambda i, j: (i, 0, j)),
        out_shape=jax.ShapeDtypeStruct((b, s, MLA_HEADS * MLA_V), BF16),
        scratch_shapes=[pltpu.VMEM((s, 2 * MLA_V), BF16)],
        compiler_params=pltpu.CompilerParams(
            dimension_semantics=("parallel", "parallel"), vmem_limit_bytes=V7X_VMEM_LIMIT),
        name="mla_attn",
    )(q, k, v)


def _gla_body(q_ref, k_ref, v_ref, la_ref, zr_ref, g_ref, o_ref, u_ref, st_ref, kd_ref, oraw_ref):
    seq, k_dim = k_ref.shape[1], k_ref.shape[2]
    v_dim = v_ref.shape[2]
    n_chunks = seq // CHUNK
    grp = GLA_GROUP
    r = lax.broadcasted_iota(jnp.int32, (grp, grp), 0)
    c = lax.broadcasted_iota(jnp.int32, (grp, grp), 1)
    later = ((c > r) & (c // CHUNK == r // CHUNK)).astype(BF16)

    b_end = jnp.sum(la_ref[0].reshape(n_chunks, CHUNK, k_dim), axis=1)
    dec = jnp.concatenate([jnp.exp(b_end), jnp.zeros((LANES - n_chunks, k_dim), F32)], axis=0)
    dec_t = dec.T

    for n in range(seq // grp):
        sl = slice(n * grp, (n + 1) * grp)
        g = la_ref[0, sl, :]
        g_hi = g.astype(BF16)
        g_lo = (g - g_hi.astype(F32)).astype(BF16)
        rest2 = _dot(later, jnp.concatenate([g_hi, g_lo], axis=1))
        rest = rest2[:, :k_dim] + rest2[:, k_dim:]
        kd_ref[sl, :] = (k_ref[0, sl, :] * jnp.exp(rest)).astype(BF16)

    for n in range(n_chunks):
        sl = slice(n * CHUNK, (n + 1) * CHUNK)
        u_ref[n] = lax.dot_general(kd_ref[sl, :], v_ref[0, sl, :], (((0,), (0,)), ((), ())),
                                   preferred_element_type=F32)

    state = jnp.zeros((k_dim, v_dim), F32)
    for n in range(n_chunks):
        state = state * dec_t[:, n:n + 1] + u_ref[n]
        st_ref[n] = state.astype(BF16)

    for n in range(n_chunks):
        sl = slice(n * CHUNK, (n + 1) * CHUNK)
        oraw_ref[sl, :] = _dot(q_ref[0, sl, :], st_ref[n])

    for n in range(seq // grp):
        sl = slice(n * grp, (n + 1) * grp)
        zr = zr_ref[0, sl, :].astype(F32)
        o_ref[0, sl, :] = (_rms(oraw_ref[sl, :], g_ref[...])
                           * (zr * jax.nn.sigmoid(zr))).astype(o_ref.dtype)


def _gla(q, k, v, la, zr, g):
    b, s, _ = q.shape
    n_chunks = s // CHUNK
    kspec = pl.BlockSpec((1, s, GLA_DK), lambda i, j: (i, 0, j))
    vspec = pl.BlockSpec((1, s, GLA_DV), lambda i, j: (i, 0, j))
    return pl.pallas_call(
        _gla_body,
        grid=(b, GLA_HEADS),
        in_specs=[kspec, kspec, vspec, kspec, vspec,
                  pl.BlockSpec((1, GLA_DV), lambda i, j: (0, 0))],
        out_specs=vspec,
        out_shape=jax.ShapeDtypeStruct((b, s, GLA_HEADS * GLA_DV), BF16),
        scratch_shapes=[pltpu.VMEM((n_chunks, GLA_DK, GLA_DV), F32),
                        pltpu.VMEM((n_chunks, GLA_DK, GLA_DV), BF16),
                        pltpu.VMEM((s, GLA_DK), BF16),
                        pltpu.VMEM((s, GLA_DV), F32)],
        compiler_params=pltpu.CompilerParams(
            dimension_semantics=("parallel", "parallel"), vmem_limit_bytes=V7X_VMEM_LIMIT),
        name="gla",
    )(q, k, v, la, zr, g)


def _mem_kv_body(m_ref, gn_ref, wk_ref, wv_ref, gk_ref, k_ref, v_ref):
    hm = _rms(m_ref[0], gn_ref[...]).astype(BF16)
    kf = _dot(hm, wk_ref[...])
    for hd in range(MEM_HEADS):
        c = hd * MEM_HEAD_DIM
        k_ref[0, :, c:c + MEM_HEAD_DIM] = _rms(kf[:, c:c + MEM_HEAD_DIM], gk_ref[...]).astype(BF16)
    v_ref[0] = _dot(hm, wv_ref[...]).astype(BF16)


def _mem_kv(mem, gn, wk, wv, gk):
    b, n, d = mem.shape
    w = wk.shape[1]
    ospec = pl.BlockSpec((1, n, w), lambda i: (i, 0, 0))
    return pl.pallas_call(
        _mem_kv_body,
        grid=(b,),
        in_specs=[pl.BlockSpec((1, n, d), lambda i: (i, 0, 0)),
                  _resident(gn.shape), _resident(wk.shape), _resident(wv.shape),
                  _resident(gk.shape)],
        out_specs=[ospec, ospec],
        out_shape=[jax.ShapeDtypeStruct((b, n, w), BF16)] * 2,
        compiler_params=pltpu.CompilerParams(
            dimension_semantics=("parallel",), vmem_limit_bytes=V7X_VMEM_LIMIT),
        name="mem_kv",
    )(mem, gn, wk, wv, gk)


def _out_mem_body(x_ref, om_ref, og_ref, wo_ref, gn_ref, wq_ref, gq_ref, km_ref, vm_ref,
                  wmo_ref, o_ref):
    n_mla = om_ref.shape[2]
    x = x_ref[0] + _dot(om_ref[0], wo_ref[:n_mla, :]) + _dot(og_ref[0], wo_ref[n_mla:, :])
    h = _rms(x, gn_ref[...]).astype(BF16)
    qm = _dot(h, wq_ref[...])
    outs = []
    for hd in range(MEM_HEADS):
        c = hd * MEM_HEAD_DIM
        qn = (_rms(qm[:, c:c + MEM_HEAD_DIM], gq_ref[...]) * (MEM_HEAD_DIM ** -0.5)).astype(BF16)
        s = _dot_nt(qn, km_ref[0, :, c:c + MEM_HEAD_DIM])
        p = jnp.exp(s - jnp.max(s, axis=-1, keepdims=True))
        l = jnp.sum(p, axis=-1, keepdims=True)
        oh = _dot(p.astype(BF16), vm_ref[0, :, c:c + MEM_HEAD_DIM]) * (1.0 / l)
        outs.append(oh.astype(BF16))
    o_ref[0] = x + _dot(jnp.concatenate(outs, axis=1), wmo_ref[...])


def _out_mem(x, om, og, wo, gn, wq, gq, km, vm, wmo):
    b, s, d = x.shape
    tm = OUT_TM
    n_mem, wm = km.shape[1], km.shape[2]
    tok = lambda w: pl.BlockSpec((1, tm, w), lambda i, j: (i, j, 0))
    memspec = pl.BlockSpec((1, n_mem, wm), lambda i, j: (i, 0, 0))
    return pl.pallas_call(
        _out_mem_body,
        grid=(b, s // tm),
        in_specs=[tok(d), tok(om.shape[2]), tok(og.shape[2]),
                  _resident(wo.shape), _resident(gn.shape), _resident(wq.shape),
                  _resident(gq.shape), memspec, memspec, _resident(wmo.shape)],
        out_specs=tok(d),
        out_shape=jax.ShapeDtypeStruct((b, s, d), F32),
        compiler_params=pltpu.CompilerParams(
            dimension_semantics=("parallel", "parallel"), vmem_limit_bytes=V7X_VMEM_LIMIT),
        name="out_mem",
    )(x, om, og, wo, gn, wq, gq, km, vm, wmo)


def _pad_rope(w):
    z = jnp.zeros(w.shape[:-1] + (LANES // 2 - HALF_ROPE,), w.dtype)
    return jnp.concatenate([w[..., :HALF_ROPE], z, w[..., HALF_ROPE:], z], axis=-1)


def _pad_head(w):
    return jnp.concatenate([w[..., :MLA_NOPE], _pad_rope(w[..., MLA_NOPE:])], axis=-1)


def _split_w_in(w_in):
    o = 0
    parts = {}
    for name, size in (("zq", MLA_Q_RANK), ("zkv", MLA_KV_RANK), ("zkr", MLA_ROPE),
                       ("gq", GLA_HEADS * GLA_DK), ("gk", GLA_HEADS * GLA_DK),
                       ("gv", GLA_HEADS * GLA_DV), ("zg", GLA_GATE_RANK),
                       ("zr", GLA_HEADS * GLA_DV)):
        parts[name] = (o, o + size)
        o += size
    w_bf = w_in.astype(BF16)
    cols = lambda a, b: w_bf[:, parts[a][0]:parts[b][1]]
    w_zg = jnp.pad(cols("zg", "zg"), ((0, 0), (0, LANES - GLA_GATE_RANK)))
    return cols("zq", "zkv"), _pad_rope(cols("zkr", "zkr")), cols("gq", "gv"), cols("zr", "zr"), w_zg


def kernel(x, mem, positions, ffn1_norm, ffn1_w_gate, ffn1_w_up, ffn1_w_down, mix_norm, w_in, q_a_norm, w_q_up, kv_a_norm, w_kv_up, mla_q_norm, mla_k_norm, gla_w_gate2, gla_b_gate, gla_out_norm, w_out, mem_attn_norm, mem_norm, mem_w_q, mem_w_k, mem_w_v, mem_w_o, mem_q_norm, mem_k_norm, ffn2_norm, ffn2_w_gate, ffn2_w_up, ffn2_w_down):
    b, s, d = x.shape
    m = b * s
    depth = w_in.shape[0]
    row = lambda v: v.reshape(1, -1)
    bf = lambda w: w.astype(BF16)

    half = jnp.arange(HALF_ROPE, dtype=F32)
    inv_freq = ROPE_THETA ** (-half / HALF_ROPE)
    invf = _pad_rope(jnp.concatenate([inv_freq, inv_freq])).reshape(1, LANES)
    pos = positions.reshape(m, 1)

    xf = x.reshape(m, d)
    for l in range(depth):
        xf = _ffn(xf, row(ffn1_norm[l]), ffn1_w_gate[l], ffn1_w_up[l], ffn1_w_down[l])

        wq = _pad_head(w_q_up[l].reshape(MLA_Q_RANK, MLA_HEADS, MLA_QK)).reshape(MLA_Q_RANK, -1)
        wkv = w_kv_up[l].reshape(MLA_KV_RANK, MLA_HEADS, MLA_NOPE + MLA_V)
        wkv = jnp.concatenate([wkv[..., :MLA_NOPE].reshape(MLA_KV_RANK, -1),
                               wkv[..., MLA_NOPE:].reshape(MLA_KV_RANK, -1)], axis=1)
        w2 = jnp.pad(gla_w_gate2[l], ((0, LANES - GLA_GATE_RANK), (0, 0)))
        q, k, v, gq, gk, gv, la, zr = _mixer_proj(
            xf, pos, row(mix_norm[l]), *_split_w_in(w_in[l]), row(q_a_norm[l]), bf(wq),
            row(kv_a_norm[l]), bf(wkv), row(_pad_head(mla_q_norm[l])), row(_pad_head(mla_k_norm[l])),
            invf, bf(w2), row(gla_b_gate[l]))

        r3 = lambda a: a.reshape(b, s, a.shape[-1])
        o_mla = _mla_attn(r3(q), r3(k), r3(v))
        o_gla = _gla(r3(gq), r3(gk), r3(gv), r3(la), r3(zr), row(gla_out_norm[l]))

        km, vm = _mem_kv(mem, row(mem_norm[l]), bf(mem_w_k[l]), bf(mem_w_v[l]), row(mem_k_norm[l]))
        x3 = _out_mem(r3(xf), o_mla, o_gla, bf(w_out[l]), row(mem_attn_norm[l]), bf(mem_w_q[l]),
                      row(mem_q_norm[l]), km, vm, bf(mem_w_o[l]))

        xf = _ffn(x3.reshape(m, d), row(ffn2_norm[l]), ffn2_w_gate[l], ffn2_w_up[l], ffn2_w_down[l])
    return xf.reshape(b, s, d)
```

```python
import jax
import jax.numpy as jnp
from jax import lax
from jax.experimental import pallas as pl
from jax.experimental.pallas import tpu as pltpu

F32 = jnp.float32
BF16 = jnp.bfloat16

EPS = 1e-6
CHUNK = 64
ROPE_THETA = 10000.0
GLA_TAU = 16.0
LOG2_E = 1.4426950408889634

MLA_HEADS = 8
MLA_NOPE = 128
MLA_ROPE = 64
MLA_QK = MLA_NOPE + MLA_ROPE
MLA_V = 128
MLA_Q_RANK = 512
MLA_KV_RANK = 256
GLA_HEADS = 4
GLA_DK = 128
GLA_DV = 256
GLA_GATE_RANK = 16
MEM_HEADS = 4
MEM_HEAD_DIM = 128

LANES = 128
HEAD_PAD = 2 * LANES
HALF_ROPE = MLA_ROPE // 2

V7X_VMEM_LIMIT = 56 * 1024 * 1024

FFN_TM = 1024
FFN_TF = 512
FFN_TN = 512
FFN_NORM_CHUNKS = 8
PROJ_TM = 512
ATTN_TQ = 256
GLA_GROUP = 256
OUT_TM = 512


def _rms(x, g):
    ms = jnp.mean(x * x, axis=-1, keepdims=True)
    return x * lax.rsqrt(ms + EPS) * g


def _dot(a, b):
    return jnp.dot(a, b, preferred_element_type=F32)


def _dot_nt(a, b):
    return lax.dot_general(a, b, (((1,), (1,)), ((), ())), preferred_element_type=F32)


def _resident(shape):
    nd = len(shape)
    return pl.BlockSpec(shape, lambda *_: (0,) * nd, pipeline_mode=pl.Buffered(1))


def _ffn_body(x_hbm, g_ref, wg_ref, wu_ref, wd_ref, o_hbm, xbuf, acc, h_ref, x_sem, o_sem):
    i, f = pl.program_id(0), pl.program_id(1)
    n_i, n_f = pl.num_programs(0), pl.num_programs(1)
    tm, d = acc.shape
    cur = i % 2
    norm_rows = tm // FFN_NORM_CHUNKS

    def rows(tile):
        return pl.ds(pl.multiple_of(tile * tm, tm), tm)

    def x_copy(tile):
        return pltpu.make_async_copy(x_hbm.at[rows(tile), :], xbuf, x_sem)

    def o_copy(tile):
        return pltpu.make_async_copy(acc, o_hbm.at[rows(tile), :], o_sem)

    @pl.when((i == 0) & (f == 0))
    def _():
        x_copy(0).start()
        x_copy(0).wait()
        h_ref[0] = _rms(xbuf[...], g_ref[...]).astype(BF16)

    @pl.when((f == 1) & (i + 1 < n_i))
    def _():
        x_copy(i + 1).wait()

    def body(first):
        h = h_ref[cur]
        chunk = jnp.clip(f - 1, 0, FFN_NORM_CHUNKS - 1)
        rs = pl.ds(pl.multiple_of(chunk * norm_rows, norm_rows), norm_rows)
        h_ref[1 - cur, rs, :] = _rms(xbuf[rs, :], g_ref[...]).astype(BF16)

        gate = _dot(h, wg_ref[...].astype(BF16))
        up = _dot(h, wu_ref[...].astype(BF16))
        a = ((0.5 * gate) * jax.nn.sigmoid(gate) * up).astype(BF16)
        if first:
            @pl.when(i > 0)
            def _():
                o_copy(i - 1).wait()

        for c in range(0, d, FFN_TN):
            part = _dot(a, wd_ref[:, c:c + FFN_TN].astype(BF16))
            if first:
                acc[:, c:c + FFN_TN] = xbuf[:, c:c + FFN_TN] + part
            else:
                acc[:, c:c + FFN_TN] += part

    @pl.when(f == 0)
    def _():
        body(True)

        @pl.when(i + 1 < n_i)
        def _():
            x_copy(i + 1).start()

    @pl.when(f > 0)
    def _():
        body(False)

    @pl.when(f == n_f - 1)
    def _():
        o_copy(i).start()

        @pl.when(i == n_i - 1)
        def _():
            o_copy(i).wait()


def _ffn(x, g, wg, wu, wd):
    m, d = x.shape
    f = wg.shape[1]
    assert f // FFN_TF > FFN_NORM_CHUNKS and FFN_TM % (16 * FFN_NORM_CHUNKS) == 0
    return pl.pallas_call(
        _ffn_body,
        grid=(m // FFN_TM, f // FFN_TF),
        in_specs=[
            pl.BlockSpec(memory_space=pl.ANY),
            pl.BlockSpec((1, d), lambda i, j: (0, 0)),
            pl.BlockSpec((d, FFN_TF), lambda i, j: (0, j)),
            pl.BlockSpec((d, FFN_TF), lambda i, j: (0, j)),
            pl.BlockSpec((FFN_TF, d), lambda i, j: (j, 0)),
        ],
        out_specs=pl.BlockSpec(memory_space=pl.ANY),
        out_shape=jax.ShapeDtypeStruct((m, d), F32),
        scratch_shapes=[pltpu.VMEM((FFN_TM, d), F32),
                        pltpu.VMEM((FFN_TM, d), F32),
                        pltpu.VMEM((2, FFN_TM, d), BF16),
                        pltpu.SemaphoreType.DMA(()),
                        pltpu.SemaphoreType.DMA(())],
        compiler_params=pltpu.CompilerParams(
            dimension_semantics=("arbitrary", "arbitrary"),
            vmem_limit_bytes=V7X_VMEM_LIMIT),
        name="ffn",
    )(x, g, wg, wu, wd)


def _mixer_proj_body(x_ref, pos_ref, gmix_ref, w_lat_ref, w_krg_ref, w_gla_ref, w_zr_ref,
                     gqa_ref, wq_ref, gkva_ref, wkv_ref, gq_ref, gk_ref, invf_ref, w2_ref, b2_ref,
                     q_out, k_out, v_out, gq_out, gk_out, gv_out, la_out, zr_out):
    h = _rms(x_ref[...], gmix_ref[...]).astype(BF16)
    n_qk = GLA_HEADS * GLA_DK
    zq = _dot(h, w_lat_ref[:, :MLA_Q_RANK])
    zkv = _dot(h, w_lat_ref[:, MLA_Q_RANK:])
    zkrg = _dot(h, w_krg_ref[...])
    lane = lax.broadcasted_iota(jnp.int32, (1, LANES), 1)
    zkr = jnp.where(lane % (LANES // 2) < HALF_ROPE, zkrg, 0.0)
    gq_out[...] = (_dot(h, w_gla_ref[:, :n_qk]) * (GLA_DK ** -0.5)).astype(BF16)

    pre = _dot(zkrg.astype(BF16), w2_ref[...]) + b2_ref[...]
    log_sig = jnp.minimum(pre, 0.0) - jnp.log1p(jnp.exp(-jnp.abs(pre)))
    la_out[...] = log_sig * (1.0 / GLA_TAU)

    ang = pos_ref[...].astype(F32) * invf_ref[...]
    lane = lax.broadcasted_iota(jnp.int32, (1, LANES), 1)
    cos_t = jnp.cos(ang)
    sin_t = jnp.sin(ang) * jnp.where(lane < LANES // 2, -1.0, 1.0)

    def rope(v):
        return v * cos_t + pltpu.roll(v, LANES // 2, axis=1) * sin_t

    inv_qk = 1.0 / MLA_QK

    qa = _rms(zq, gqa_ref[...]).astype(BF16)
    kva = _rms(zkv, gkva_ref[...]).astype(BF16)
    qf = _dot(qa, wq_ref[...])
    kvf = _dot(kva, wkv_ref[...])
    gk_out[...] = _dot(h, w_gla_ref[:, n_qk:2 * n_qk])
    for hd in range(MLA_HEADS):
        c = hd * HEAD_PAD
        blk = qf[:, c:c + HEAD_PAD]
        ss = jnp.sum(blk * blk, axis=-1, keepdims=True)
        rinv = lax.rsqrt(ss * inv_qk + EPS) * (MLA_QK ** -0.5 * LOG2_E)
        q_out[:, c:c + LANES] = (blk[:, :LANES] * rinv * gq_ref[:, :LANES]).astype(BF16)
        q_out[:, c + LANES:c + HEAD_PAD] = rope(
            blk[:, LANES:] * rinv * gq_ref[:, LANES:]).astype(BF16)

    gv_out[...] = _dot(h, w_gla_ref[:, 2 * n_qk:]).astype(BF16)

    ssr = jnp.sum(zkr * zkr, axis=-1, keepdims=True)
    kr_base = rope(zkr * gk_ref[:, LANES:])
    for hd in range(MLA_HEADS):
        kn = kvf[:, hd * MLA_NOPE:(hd + 1) * MLA_NOPE]
        ss = jnp.sum(kn * kn, axis=-1, keepdims=True) + ssr
        rinv = lax.rsqrt(ss * inv_qk + EPS)
        c = hd * HEAD_PAD
        k_out[:, c:c + LANES] = (kn * rinv * gk_ref[:, :LANES]).astype(BF16)
        k_out[:, c + LANES:c + HEAD_PAD] = (kr_base * rinv).astype(BF16)
    v_out[...] = kvf[:, MLA_HEADS * MLA_NOPE:].astype(BF16)
    zr_out[...] = _dot(h, w_zr_ref[...]).astype(BF16)


def _mixer_proj(x, pos, *weights):
    m, d = x.shape
    tm = PROJ_TM
    row = lambda i: (i, 0)
    widths = [
        (MLA_HEADS * HEAD_PAD, BF16),
        (MLA_HEADS * HEAD_PAD, BF16),
        (MLA_HEADS * MLA_V, BF16),
        (GLA_HEADS * GLA_DK, BF16),
        (GLA_HEADS * GLA_DK, F32),
        (GLA_HEADS * GLA_DV, BF16),
        (GLA_HEADS * GLA_DK, F32),
        (GLA_HEADS * GLA_DV, BF16),
    ]
    return pl.pallas_call(
        _mixer_proj_body,
        grid=(m // tm,),
        in_specs=[pl.BlockSpec((tm, d), row), pl.BlockSpec((tm, 1), row)]
        + [_resident(w.shape) for w in weights],
        out_specs=[pl.BlockSpec((tm, w), row) for w, _ in widths],
        out_shape=[jax.ShapeDtypeStruct((m, w), dt) for w, dt in widths],
        compiler_params=pltpu.CompilerParams(
            dimension_semantics=("parallel",), vmem_limit_bytes=V7X_VMEM_LIMIT),
        name="mixer_proj",
    )(x, pos, *weights)


def _mla_attn_body(q_ref, k_ref, v_ref, o_ref, vext_ref):
    seq = q_ref.shape[1]
    dv = v_ref.shape[2]
    tq = ATTN_TQ
    r = lax.broadcasted_iota(jnp.int32, (tq, tq), 0) // CHUNK
    c = lax.broadcasted_iota(jnp.int32, (tq, tq), 1) // CHUNK
    diag_mask = c <= r

    vext_ref[:, :dv] = v_ref[0]
    vext_ref[:, dv:] = jnp.ones((seq, vext_ref.shape[1] - dv), vext_ref.dtype)

    def scores(i):
        lo = i * tq
        q = q_ref[0, lo:lo + tq, :]
        s_d = jnp.where(diag_mask, _dot_nt(q, k_ref[0, lo:lo + tq, :]), -jnp.inf)
        s_o = _dot_nt(q, k_ref[0, :lo, :]) if i > 0 else None
        return s_d, s_o

    order = list(range(seq // tq))[::-1]
    nxt = scores(order[0])
    for pos, i in enumerate(order):
        lo = i * tq
        s_d, s_o = nxt
        if pos + 1 < len(order):
            nxt = scores(order[pos + 1])
        m = jnp.max(s_d, axis=-1, keepdims=True)
        if i > 0:
            m = jnp.maximum(m, jnp.max(s_o, axis=-1, keepdims=True))
        pv = _dot(jnp.exp2(s_d - m).astype(BF16), vext_ref[lo:lo + tq, :])
        if i > 0:
            pv = pv + _dot(jnp.exp2(s_o - m).astype(BF16), vext_ref[:lo, :])
        o_ref[0, lo:lo + tq, :] = (pv[:, :dv] * (1.0 / pv[:, dv:dv + 1])).astype(o_ref.dtype)


def _mla_attn(q, k, v):
    b, s, _ = q.shape
    return pl.pallas_call(
        _mla_attn_body,
        grid=(b, MLA_HEADS),
        in_specs=[
            pl.BlockSpec((1, s, HEAD_PAD), lambda i, j: (i, 0, j)),
            pl.BlockSpec((1, s, HEAD_PAD), lambda i, j: (i, 0, j)),
            pl.BlockSpec((1, s, MLA_V), lambda i, j: (i, 0, j)),
        ],
        out_specs=pl.BlockSpec((1, s, MLA_V), lambda i, j: (i, 0, j)),
        out_shape=jax.ShapeDtypeStruct((b, s, MLA_HEADS * MLA_V), BF16),
        scratch_shapes=[pltpu.VMEM((s, 2 * MLA_V), BF16)],
        compiler_params=pltpu.CompilerParams(
            dimension_semantics=("parallel", "parallel"), vmem_limit_bytes=V7X_VMEM_LIMIT),
        name="mla_attn",
    )(q, k, v)


def _gla_body(q_ref, k_ref, v_ref, la_ref, zr_ref, g_ref, o_ref, u_ref, st_ref, kd_ref, oraw_ref):
    seq, k_dim = k_ref.shape[1], k_ref.shape[2]
    v_dim = v_ref.shape[2]
    n_chunks = seq // CHUNK
    grp = GLA_GROUP
    r = lax.broadcasted_iota(jnp.int32, (grp, grp), 0)
    c = lax.broadcasted_iota(jnp.int32, (grp, grp), 1)
    later = ((c > r) & (c // CHUNK == r // CHUNK)).astype(BF16)

    b_end = jnp.sum(la_ref[0].reshape(n_chunks, CHUNK, k_dim), axis=1)
    dec = jnp.concatenate([jnp.exp(b_end), jnp.zeros((LANES - n_chunks, k_dim), F32)], axis=0)
    dec_t = dec.T

    for n in range(seq // grp):
        sl = slice(n * grp, (n + 1) * grp)
        g = la_ref[0, sl, :]
        g_hi = g.astype(BF16)
        g_lo = (g - g_hi.astype(F32)).astype(BF16)
        rest2 = _dot(later, jnp.concatenate([g_hi, g_lo], axis=1))
        rest = rest2[:, :k_dim] + rest2[:, k_dim:]
        kd_ref[sl, :] = (k_ref[0, sl, :] * jnp.exp(rest)).astype(BF16)

    for n in range(n_chunks):
        sl = slice(n * CHUNK, (n + 1) * CHUNK)
        u_ref[n] = lax.dot_general(kd_ref[sl, :], v_ref[0, sl, :], (((0,), (0,)), ((), ())),
                                   preferred_element_type=F32)

    state = jnp.zeros((k_dim, v_dim), F32)
    for n in range(n_chunks):
        state = state * dec_t[:, n:n + 1] + u_ref[n]
        st_ref[n] = state.astype(BF16)

    for n in range(n_chunks):
        sl = slice(n * CHUNK, (n + 1) * CHUNK)
        oraw_ref[sl, :] = _dot(q_ref[0, sl, :], st_ref[n])

    for n in range(seq // grp):
        sl = slice(n * grp, (n + 1) * grp)
        zr = zr_ref[0, sl, :].astype(F32)
        o_ref[0, sl, :] = (_rms(oraw_ref[sl, :], g_ref[...])
                           * (zr * jax.nn.sigmoid(zr))).astype(o_ref.dtype)


def _gla(q, k, v, la, zr, g):
    b, s, _ = q.shape
    n_chunks = s // CHUNK
    kspec = pl.BlockSpec((1, s, GLA_DK), lambda i, j: (i, 0, j))
    vspec = pl.BlockSpec((1, s, GLA_DV), lambda i, j: (i, 0, j))
    return pl.pallas_call(
        _gla_body,
        grid=(b, GLA_HEADS),
        in_specs=[kspec, kspec, vspec, kspec, vspec,
                  pl.BlockSpec((1, GLA_DV), lambda i, j: (0, 0))],
        out_specs=vspec,
        out_shape=jax.ShapeDtypeStruct((b, s, GLA_HEADS * GLA_DV), BF16),
        scratch_shapes=[pltpu.VMEM((n_chunks, GLA_DK, GLA_DV), F32),
                        pltpu.VMEM((n_chunks, GLA_DK, GLA_DV), BF16),
                        pltpu.VMEM((s, GLA_DK), BF16),
                        pltpu.VMEM((s, GLA_DV), F32)],
        compiler_params=pltpu.CompilerParams(
            dimension_semantics=("parallel", "parallel"), vmem_limit_bytes=V7X_VMEM_LIMIT),
        name="gla",
    )(q, k, v, la, zr, g)


def _mem_kv_body(m_ref, gn_ref, wk_ref, wv_ref, gk_ref, k_ref, v_ref):
    hm = _rms(m_ref[0], gn_ref[...]).astype(BF16)
    kf = _dot(hm, wk_ref[...])
    for hd in range(MEM_HEADS):
        c = hd * MEM_HEAD_DIM
        k_ref[0, :, c:c + MEM_HEAD_DIM] = _rms(kf[:, c:c + MEM_HEAD_DIM], gk_ref[...]).astype(BF16)
    v_ref[0] = _dot(hm, wv_ref[...]).astype(BF16)


def _mem_kv(mem, gn, wk, wv, gk):
    b, n, d = mem.shape
    w = wk.shape[1]
    ospec = pl.BlockSpec((1, n, w), lambda i: (i, 0, 0))
    return pl.pallas_call(
        _mem_kv_body,
        grid=(b,),
        in_specs=[pl.BlockSpec((1, n, d), lambda i: (i, 0, 0)),
                  _resident(gn.shape), _resident(wk.shape), _resident(wv.shape),
                  _resident(gk.shape)],
        out_specs=[ospec, ospec],
        out_shape=[jax.ShapeDtypeStruct((b, n, w), BF16)] * 2,
        compiler_params=pltpu.CompilerParams(
            dimension_semantics=("parallel",), vmem_limit_bytes=V7X_VMEM_LIMIT),
        name="mem_kv",
    )(mem, gn, wk, wv, gk)


def _out_mem_body(x_ref, om_ref, og_ref, wo_ref, gn_ref, wq_ref, gq_ref, km_ref, vm_ref,
                  wmo_ref, o_ref):
    n_mla = om_ref.shape[2]
    x = x_ref[0] + _dot(om_ref[0], wo_ref[:n_mla, :]) + _dot(og_ref[0], wo_ref[n_mla:, :])
    h = _rms(x, gn_ref[...]).astype(BF16)
    qm = _dot(h, wq_ref[...])
    outs = []
    for hd in range(MEM_HEADS):
        c = hd * MEM_HEAD_DIM
        qn = (_rms(qm[:, c:c + MEM_HEAD_DIM], gq_ref[...]) * (MEM_HEAD_DIM ** -0.5)).astype(BF16)
        s = _dot_nt(qn, km_ref[0, :, c:c + MEM_HEAD_DIM])
        p = jnp.exp(s - jnp.max(s, axis=-1, keepdims=True))
        l = jnp.sum(p, axis=-1, keepdims=True)
        oh = _dot(p.astype(BF16), vm_ref[0, :, c:c + MEM_HEAD_DIM]) * (1.0 / l)
        outs.append(oh.astype(BF16))
    o_ref[0] = x + _dot(jnp.concatenate(outs, axis=1), wmo_ref[...])


def _out_mem(x, om, og, wo, gn, wq, gq, km, vm, wmo):
    b, s, d = x.shape
    tm = OUT_TM
    n_mem, wm = km.shape[1], km.shape[2]
    tok = lambda w: pl.BlockSpec((1, tm, w), lambda i, j: (i, j, 0))
    memspec = pl.BlockSpec((1, n_mem, wm), lambda i, j: (i, 0, 0))
    return pl.pallas_call(
        _out_mem_body,
        grid=(b, s // tm),
        in_specs=[tok(d), tok(om.shape[2]), tok(og.shape[2]),
                  _resident(wo.shape), _resident(gn.shape), _resident(wq.shape),
                  _resident(gq.shape), memspec, memspec, _resident(wmo.shape)],
        out_specs=tok(d),
        out_shape=jax.ShapeDtypeStruct((b, s, d), F32),
        compiler_params=pltpu.CompilerParams(
            dimension_semantics=("parallel", "parallel"), vmem_limit_bytes=V7X_VMEM_LIMIT),
        name="out_mem",
    )(x, om, og, wo, gn, wq, gq, km, vm, wmo)


def _pad_rope(w):
    z = jnp.zeros(w.shape[:-1] + (LANES // 2 - HALF_ROPE,), w.dtype)
    return jnp.concatenate([w[..., :HALF_ROPE], z, w[..., HALF_ROPE:], z], axis=-1)


def _pad_head(w):
    return jnp.concatenate([w[..., :MLA_NOPE], _pad_rope(w[..., MLA_NOPE:])], axis=-1)


def _split_w_in(w_in):
    o = 0
    parts = {}
    for name, size in (("zq", MLA_Q_RANK), ("zkv", MLA_KV_RANK), ("zkr", MLA_ROPE),
                       ("gq", GLA_HEADS * GLA_DK), ("gk", GLA_HEADS * GLA_DK),
                       ("gv", GLA_HEADS * GLA_DV), ("zg", GLA_GATE_RANK),
                       ("zr", GLA_HEADS * GLA_DV)):
        parts[name] = (o, o + size)
        o += size
    w_bf = w_in.astype(BF16)
    cols = lambda a, b: w_bf[:, parts[a][0]:parts[b][1]]
    kr, zg = cols("zkr", "zkr"), cols("zg", "zg")
    zeros = lambda n: jnp.zeros((w_in.shape[0], n), BF16)
    w_krg = jnp.concatenate(
        [kr[:, :HALF_ROPE], zg, zeros(LANES // 2 - HALF_ROPE - GLA_GATE_RANK),
         kr[:, HALF_ROPE:], zeros(LANES // 2 - HALF_ROPE)], axis=1)
    return cols("zq", "zkv"), w_krg, cols("gq", "gv"), cols("zr", "zr")


def kernel(x, mem, positions, ffn1_norm, ffn1_w_gate, ffn1_w_up, ffn1_w_down, mix_norm, w_in, q_a_norm, w_q_up, kv_a_norm, w_kv_up, mla_q_norm, mla_k_norm, gla_w_gate2, gla_b_gate, gla_out_norm, w_out, mem_attn_norm, mem_norm, mem_w_q, mem_w_k, mem_w_v, mem_w_o, mem_q_norm, mem_k_norm, ffn2_norm, ffn2_w_gate, ffn2_w_up, ffn2_w_down):
    b, s, d = x.shape
    m = b * s
    depth = w_in.shape[0]
    row = lambda v: v.reshape(1, -1)
    bf = lambda w: w.astype(BF16)

    half = jnp.arange(HALF_ROPE, dtype=F32)
    inv_freq = ROPE_THETA ** (-half / HALF_ROPE)
    invf = _pad_rope(jnp.concatenate([inv_freq, inv_freq])).reshape(1, LANES)
    pos = positions.reshape(m, 1)

    xf = x.reshape(m, d)
    for l in range(depth):
        xf = _ffn(xf, row(ffn1_norm[l]), ffn1_w_gate[l], ffn1_w_up[l], ffn1_w_down[l])

        wq = _pad_head(w_q_up[l].reshape(MLA_Q_RANK, MLA_HEADS, MLA_QK)).reshape(MLA_Q_RANK, -1)
        wkv = w_kv_up[l].reshape(MLA_KV_RANK, MLA_HEADS, MLA_NOPE + MLA_V)
        wkv = jnp.concatenate([wkv[..., :MLA_NOPE].reshape(MLA_KV_RANK, -1),
                               wkv[..., MLA_NOPE:].reshape(MLA_KV_RANK, -1)], axis=1)
        w2 = jnp.pad(gla_w_gate2[l], ((HALF_ROPE, LANES - HALF_ROPE - GLA_GATE_RANK), (0, 0)))
        q, k, v, gq, gk, gv, la, zr = _mixer_proj(
            xf, pos, row(mix_norm[l]), *_split_w_in(w_in[l]), row(q_a_norm[l]), bf(wq),
            row(kv_a_norm[l]), bf(wkv), row(_pad_head(mla_q_norm[l])), row(_pad_head(mla_k_norm[l])),
            invf, bf(w2), row(gla_b_gate[l]))

        r3 = lambda a: a.reshape(b, s, a.shape[-1])
        o_mla = _mla_attn(r3(q), r3(k), r3(v))
        o_gla = _gla(r3(gq), r3(gk), r3(gv), r3(la), r3(zr), row(gla_out_norm[l]))

        km, vm = _mem_kv(mem, row(mem_norm[l]), bf(mem_w_k[l]), bf(mem_w_v[l]), row(mem_k_norm[l]))
        x3 = _out_mem(r3(xf), o_mla, o_gla, bf(w_out[l]), row(mem_attn_norm[l]), bf(mem_w_q[l]),
                      row(mem_q_norm[l]), km, vm, bf(mem_w_o[l]))

        xf = _ffn(x3.reshape(m, d), row(ffn2_norm[l]), ffn2_w_gate[l], ffn2_w_up[l], ffn2_w_down[l])
    return xf.reshape(b, s, d)
```

```python
import jax
import jax.numpy as jnp
from jax import lax
from jax.experimental import pallas as pl
from jax.experimental.pallas import tpu as pltpu

F32 = jnp.float32
BF16 = jnp.bfloat16

EPS = 1e-6
CHUNK = 64
ROPE_THETA = 10000.0
GLA_TAU = 16.0
LOG2_E = 1.4426950408889634

MLA_HEADS = 8
MLA_NOPE = 128
MLA_ROPE = 64
MLA_QK = MLA_NOPE + MLA_ROPE
MLA_V = 128
MLA_Q_RANK = 512
MLA_KV_RANK = 256
GLA_HEADS = 4
GLA_DK = 128
GLA_DV = 256
GLA_GATE_RANK = 16
MEM_HEADS = 4
MEM_HEAD_DIM = 128

LANES = 128
HEAD_PAD = 2 * LANES
HALF_ROPE = MLA_ROPE // 2

V7X_VMEM_LIMIT = 56 * 1024 * 1024

FFN_TM = 1024
FFN_TF = 512
FFN_DOWN_SPLIT = ((0, 512), (512, 1024), (1024, 1536), (1536, 1792), (1792, 2048))
FFN_NORM_CHUNKS = 8
PROJ_TM = 512
ATTN_TQ = 256
ATTN_HEADS_PER_STEP = 1
GLA_GROUP = 256
OUT_TM = 512
W_IN_ROWS = 256


def _rms(x, g):
    ms = jnp.mean(x * x, axis=-1, keepdims=True)
    return x * lax.rsqrt(ms + EPS) * g


def _dot(a, b):
    return jnp.dot(a, b, preferred_element_type=F32)


def _dot_nt(a, b):
    return lax.dot_general(a, b, (((1,), (1,)), ((), ())), preferred_element_type=F32)


def _resident(shape):
    nd = len(shape)
    return pl.BlockSpec(shape, lambda *_: (0,) * nd, pipeline_mode=pl.Buffered(1))


def _ffn_body(x_hbm, g_ref, wg_ref, wu_ref, wd_ref, o_hbm, xbuf, acc, h_ref, x_sem, o_sem):
    i, f = pl.program_id(0), pl.program_id(1)
    n_i, n_f = pl.num_programs(0), pl.num_programs(1)
    tm, d = acc.shape
    cur = i % 2
    norm_rows = tm // FFN_NORM_CHUNKS

    def rows(tile):
        return pl.ds(pl.multiple_of(tile * tm, tm), tm)

    def x_copy(tile):
        return pltpu.make_async_copy(x_hbm.at[rows(tile), :], xbuf, x_sem)

    def o_copy(tile):
        return pltpu.make_async_copy(acc, o_hbm.at[rows(tile), :], o_sem)

    @pl.when((i == 0) & (f == 0))
    def _():
        x_copy(0).start()
        x_copy(0).wait()
        h_ref[0] = _rms(xbuf[...], g_ref[...]).astype(BF16)

    @pl.when((f == 1) & (i + 1 < n_i))
    def _():
        x_copy(i + 1).wait()

    def body(first):
        h = h_ref[cur]
        chunk = jnp.clip(f - 1, 0, FFN_NORM_CHUNKS - 1)
        rs = pl.ds(pl.multiple_of(chunk * norm_rows, norm_rows), norm_rows)
        h_ref[1 - cur, rs, :] = _rms(xbuf[rs, :], g_ref[...]).astype(BF16)

        gate = _dot(h, wg_ref[...].astype(BF16))
        up = _dot(h, wu_ref[...].astype(BF16))
        a = ((0.5 * gate) * jax.nn.sigmoid(gate) * up).astype(BF16)
        if first:
            @pl.when(i > 0)
            def _():
                o_copy(i - 1).wait()

        for c0, c1 in FFN_DOWN_SPLIT:
            part = _dot(a, wd_ref[:, c0:c1].astype(BF16))
            if first:
                acc[:, c0:c1] = xbuf[:, c0:c1] + part
            else:
                acc[:, c0:c1] += part

    @pl.when(f == 0)
    def _():
        body(True)

        @pl.when(i + 1 < n_i)
        def _():
            x_copy(i + 1).start()

    @pl.when(f > 0)
    def _():
        body(False)

    @pl.when(f == n_f - 1)
    def _():
        o_copy(i).start()

        @pl.when(i == n_i - 1)
        def _():
            o_copy(i).wait()


def _ffn(x, g, wg, wu, wd):
    m, d = x.shape
    f = wg.shape[1]
    assert f // FFN_TF > FFN_NORM_CHUNKS and FFN_TM % (16 * FFN_NORM_CHUNKS) == 0
    assert FFN_DOWN_SPLIT[-1][1] == d
    return pl.pallas_call(
        _ffn_body,
        grid=(m // FFN_TM, f // FFN_TF),
        in_specs=[
            pl.BlockSpec(memory_space=pl.ANY),
            pl.BlockSpec((1, d), lambda i, j: (0, 0)),
            pl.BlockSpec((d, FFN_TF), lambda i, j: (0, j)),
            pl.BlockSpec((d, FFN_TF), lambda i, j: (0, j)),
            pl.BlockSpec((FFN_TF, d), lambda i, j: (j, 0)),
        ],
        out_specs=pl.BlockSpec(memory_space=pl.ANY),
        out_shape=jax.ShapeDtypeStruct((m, d), F32),
        scratch_shapes=[pltpu.VMEM((FFN_TM, d), F32),
                        pltpu.VMEM((FFN_TM, d), F32),
                        pltpu.VMEM((2, FFN_TM, d), BF16),
                        pltpu.SemaphoreType.DMA(()),
                        pltpu.SemaphoreType.DMA(())],
        compiler_params=pltpu.CompilerParams(
            dimension_semantics=("arbitrary", "arbitrary"),
            vmem_limit_bytes=V7X_VMEM_LIMIT),
        name="ffn",
    )(x, g, wg, wu, wd)


def _mixer_proj_body(x_ref, pos_ref, gmix_ref, w_lat_ref, w_krg_ref, w_gla_ref, w_zr_ref,
                     gqa_ref, wq_ref, gkva_ref, wkv_ref, gq_ref, gk_ref, invf_ref, w2_ref, b2_ref,
                     q_out, k_out, v_out, gq_out, gk_out, gv_out, la_out, zr_out):
    h = _rms(x_ref[...], gmix_ref[...]).astype(BF16)
    n_qk = GLA_HEADS * GLA_DK
    zq = _dot(h, w_lat_ref[:, :MLA_Q_RANK])
    zkv = _dot(h, w_lat_ref[:, MLA_Q_RANK:])
    zkrg = _dot(h, w_krg_ref[...])
    lane = lax.broadcasted_iota(jnp.int32, (1, LANES), 1)
    zkr = jnp.where(lane % (LANES // 2) < HALF_ROPE, zkrg, 0.0)
    gq_out[...] = (_dot(h, w_gla_ref[:, :n_qk]) * (GLA_DK ** -0.5)).astype(BF16)
    gk_out[...] = _dot(h, w_gla_ref[:, n_qk:2 * n_qk]).astype(BF16)

    pre = _dot(zkrg.astype(BF16), w2_ref[...]) + b2_ref[...]
    log_sig = jnp.minimum(pre, 0.0) - jnp.log1p(jnp.exp(-jnp.abs(pre)))
    la_out[...] = log_sig * (LOG2_E / GLA_TAU)

    ang = pos_ref[...].astype(F32) * invf_ref[...]
    cos_t = jnp.cos(ang)
    sin_t = jnp.sin(ang) * jnp.where(lane < LANES // 2, -1.0, 1.0)

    def rope(v):
        return v * cos_t + pltpu.roll(v, LANES // 2, axis=1) * sin_t

    inv_qk = 1.0 / MLA_QK

    qa = _rms(zq, gqa_ref[...]).astype(BF16)
    kva = _rms(zkv, gkva_ref[...]).astype(BF16)
    qf = _dot(qa, wq_ref[...])
    kvf = _dot(kva, wkv_ref[...])
    n_v = GLA_HEADS * GLA_DV
    gv_out[:, :n_v // 2] = _dot(h, w_gla_ref[:, 2 * n_qk:2 * n_qk + n_v // 2]).astype(BF16)
    for hd in range(MLA_HEADS):
        c = hd * HEAD_PAD
        blk = qf[:, c:c + HEAD_PAD]
        ss = jnp.sum(blk * blk, axis=-1, keepdims=True)
        rinv = lax.rsqrt(ss * inv_qk + EPS) * (MLA_QK ** -0.5 * LOG2_E)
        q_out[:, c:c + LANES] = (blk[:, :LANES] * rinv * gq_ref[:, :LANES]).astype(BF16)
        q_out[:, c + LANES:c + HEAD_PAD] = rope(
            blk[:, LANES:] * rinv * gq_ref[:, LANES:]).astype(BF16)

    gv_out[:, n_v // 2:] = _dot(h, w_gla_ref[:, 2 * n_qk + n_v // 2:]).astype(BF16)

    ssr = jnp.sum(zkr * zkr, axis=-1, keepdims=True)
    kr_base = rope(zkr * gk_ref[:, LANES:])
    for hd in range(MLA_HEADS):
        kn = kvf[:, hd * MLA_NOPE:(hd + 1) * MLA_NOPE]
        ss = jnp.sum(kn * kn, axis=-1, keepdims=True) + ssr
        rinv = lax.rsqrt(ss * inv_qk + EPS)
        c = hd * HEAD_PAD
        k_out[:, c:c + LANES] = (kn * rinv * gk_ref[:, :LANES]).astype(BF16)
        k_out[:, c + LANES:c + HEAD_PAD] = (kr_base * rinv).astype(BF16)
    v_out[...] = kvf[:, MLA_HEADS * MLA_NOPE:].astype(BF16)
    zr_out[...] = _dot(h, w_zr_ref[...]).astype(BF16)


def _mixer_proj(x, pos, *weights):
    m, d = x.shape
    tm = PROJ_TM
    row = lambda i: (i, 0)
    widths = [
        (MLA_HEADS * HEAD_PAD, BF16),
        (MLA_HEADS * HEAD_PAD, BF16),
        (MLA_HEADS * MLA_V, BF16),
        (GLA_HEADS * GLA_DK, BF16),
        (GLA_HEADS * GLA_DK, BF16),
        (GLA_HEADS * GLA_DV, BF16),
        (GLA_HEADS * GLA_DK, F32),
        (GLA_HEADS * GLA_DV, BF16),
    ]
    return pl.pallas_call(
        _mixer_proj_body,
        grid=(m // tm,),
        in_specs=[pl.BlockSpec((tm, d), row), pl.BlockSpec((tm, 1), row)]
        + [_resident(w.shape) for w in weights],
        out_specs=[pl.BlockSpec((tm, w), row) for w, _ in widths],
        out_shape=[jax.ShapeDtypeStruct((m, w), dt) for w, dt in widths],
        compiler_params=pltpu.CompilerParams(
            dimension_semantics=("parallel",), vmem_limit_bytes=V7X_VMEM_LIMIT),
        name="mixer_proj",
    )(x, pos, *weights)


def _mla_attn_body(q_ref, k_ref, v_ref, o_ref, vext_ref):
    seq = q_ref.shape[1]
    dv = MLA_V
    tq = ATTN_TQ
    r = lax.broadcasted_iota(jnp.int32, (tq, tq), 0) // CHUNK
    c = lax.broadcasted_iota(jnp.int32, (tq, tq), 1) // CHUNK
    diag_mask = c <= r

    for hh in range(ATTN_HEADS_PER_STEP):
        vext_ref[:, 2 * hh * dv:(2 * hh + 1) * dv] = v_ref[0, :, hh * dv:(hh + 1) * dv]
        vext_ref[:, (2 * hh + 1) * dv:(2 * hh + 2) * dv] = jnp.ones((seq, dv), vext_ref.dtype)

    def scores(job):
        hh, i = job
        lo = i * tq
        cols = slice(hh * HEAD_PAD, (hh + 1) * HEAD_PAD)
        q = q_ref[0, lo:lo + tq, cols]
        s_d = jnp.where(diag_mask, _dot_nt(q, k_ref[0, lo:lo + tq, cols]), -jnp.inf)
        s_o = _dot_nt(q, k_ref[0, :lo, cols]) if i > 0 else None
        return s_d, s_o

    jobs = [(hh, i) for i in range(seq // tq)[::-1] for hh in range(ATTN_HEADS_PER_STEP)]
    nxt = scores(jobs[0])
    for pos, (hh, i) in enumerate(jobs):
        lo = i * tq
        s_d, s_o = nxt
        if pos + 1 < len(jobs):
            nxt = scores(jobs[pos + 1])
        m = jnp.max(s_d, axis=-1, keepdims=True)
        if i > 0:
            m = jnp.maximum(m, jnp.max(s_o, axis=-1, keepdims=True))
        vcols = slice(2 * hh * dv, (2 * hh + 2) * dv)
        pv = _dot(jnp.exp2(s_d - m).astype(BF16), vext_ref[lo:lo + tq, vcols])
        if i > 0:
            pv = pv + _dot(jnp.exp2(s_o - m).astype(BF16), vext_ref[:lo, vcols])
        o_ref[0, lo:lo + tq, hh * dv:(hh + 1) * dv] = (
            pv[:, :dv] * (1.0 / pv[:, dv:dv + 1])).astype(o_ref.dtype)


def _mla_attn(q, k, v):
    b, s, _ = q.shape
    hps = ATTN_HEADS_PER_STEP
    qk_spec = pl.BlockSpec((1, s, hps * HEAD_PAD), lambda i, j: (i, 0, j))
    v_spec = pl.BlockSpec((1, s, hps * MLA_V), lambda i, j: (i, 0, j))
    return pl.pallas_call(
        _mla_attn_body,
        grid=(b, MLA_HEADS // hps),
        in_specs=[qk_spec, qk_spec, v_spec],
        out_specs=v_spec,
        out_shape=jax.ShapeDtypeStruct((b, s, MLA_HEADS * MLA_V), BF16),
        scratch_shapes=[pltpu.VMEM((s, hps * 2 * MLA_V), BF16)],
        compiler_params=pltpu.CompilerParams(
            dimension_semantics=("parallel", "parallel"), vmem_limit_bytes=V7X_VMEM_LIMIT),
        name="mla_attn",
    )(q, k, v)


def _gla_body(q_ref, k_ref, v_ref, la_ref, zr_ref, g_ref, o_ref, u_ref, st_ref, kd_ref, oraw_ref):
    seq, k_dim = k_ref.shape[1], k_ref.shape[2]
    v_dim = v_ref.shape[2]
    n_chunks = seq // CHUNK
    grp = GLA_GROUP
    r = lax.broadcasted_iota(jnp.int32, (grp, grp), 0)
    c = lax.broadcasted_iota(jnp.int32, (grp, grp), 1)
    later = ((c > r) & (c // CHUNK == r // CHUNK)).astype(BF16)

    b_end = jnp.sum(la_ref[0].reshape(n_chunks, CHUNK, k_dim), axis=1)
    dec = jnp.concatenate([jnp.exp2(b_end), jnp.zeros((LANES - n_chunks, k_dim), F32)], axis=0)
    dec_t = dec.T

    for n in range(seq // grp):
        sl = slice(n * grp, (n + 1) * grp)
        g = la_ref[0, sl, :]
        g_hi = g.astype(BF16)
        g_lo = (g - g_hi.astype(F32)).astype(BF16)
        rest2 = _dot(later, jnp.concatenate([g_hi, g_lo], axis=1))
        rest = rest2[:, :k_dim] + rest2[:, k_dim:]
        kd_ref[sl, :] = (k_ref[0, sl, :].astype(F32) * jnp.exp2(rest)).astype(BF16)

    for n in range(n_chunks):
        sl = slice(n * CHUNK, (n + 1) * CHUNK)
        u_ref[n] = lax.dot_general(kd_ref[sl, :], v_ref[0, sl, :], (((0,), (0,)), ((), ())),
                                   preferred_element_type=F32)

    state = jnp.zeros((k_dim, v_dim), F32)
    for n in range(n_chunks):
        state = state * dec_t[:, n:n + 1] + u_ref[n]
        st_ref[n] = state.astype(BF16)

    for n in range(n_chunks):
        sl = slice(n * CHUNK, (n + 1) * CHUNK)
        oraw_ref[sl, :] = _dot(q_ref[0, sl, :], st_ref[n])

    for n in range(seq // grp):
        sl = slice(n * grp, (n + 1) * grp)
        zr = zr_ref[0, sl, :].astype(F32)
        o_ref[0, sl, :] = (_rms(oraw_ref[sl, :], g_ref[...])
                           * (zr * jax.nn.sigmoid(zr))).astype(o_ref.dtype)


def _gla(q, k, v, la, zr, g):
    b, s, _ = q.shape
    n_chunks = s // CHUNK
    kspec = pl.BlockSpec((1, s, GLA_DK), lambda i, j: (i, 0, j))
    vspec = pl.BlockSpec((1, s, GLA_DV), lambda i, j: (i, 0, j))
    return pl.pallas_call(
        _gla_body,
        grid=(b, GLA_HEADS),
        in_specs=[kspec, kspec, vspec, kspec, vspec,
                  pl.BlockSpec((1, GLA_DV), lambda i, j: (0, 0))],
        out_specs=vspec,
        out_shape=jax.ShapeDtypeStruct((b, s, GLA_HEADS * GLA_DV), BF16),
        scratch_shapes=[pltpu.VMEM((n_chunks, GLA_DK, GLA_DV), F32),
                        pltpu.VMEM((n_chunks, GLA_DK, GLA_DV), BF16),
                        pltpu.VMEM((s, GLA_DK), BF16),
                        pltpu.VMEM((s, GLA_DV), F32)],
        compiler_params=pltpu.CompilerParams(
            dimension_semantics=("parallel", "parallel"), vmem_limit_bytes=V7X_VMEM_LIMIT),
        name="gla",
    )(q, k, v, la, zr, g)


def _mem_kv_body(m_ref, gn_ref, wk_ref, wv_ref, gk_ref, k_ref, v_ref):
    hm = _rms(m_ref[0], gn_ref[...]).astype(BF16)
    kf = _dot(hm, wk_ref[...])
    for hd in range(MEM_HEADS):
        c = hd * MEM_HEAD_DIM
        k_ref[0, :, c:c + MEM_HEAD_DIM] = _rms(kf[:, c:c + MEM_HEAD_DIM], gk_ref[...]).astype(BF16)
    v_ref[0] = _dot(hm, wv_ref[...]).astype(BF16)


def _mem_kv(mem, gn, wk, wv, gk):
    b, n, d = mem.shape
    w = wk.shape[1]
    ospec = pl.BlockSpec((1, n, w), lambda i: (i, 0, 0))
    return pl.pallas_call(
        _mem_kv_body,
        grid=(b,),
        in_specs=[pl.BlockSpec((1, n, d), lambda i: (i, 0, 0)),
                  _resident(gn.shape), _resident(wk.shape), _resident(wv.shape),
                  _resident(gk.shape)],
        out_specs=[ospec, ospec],
        out_shape=[jax.ShapeDtypeStruct((b, n, w), BF16)] * 2,
        compiler_params=pltpu.CompilerParams(
            dimension_semantics=("parallel",), vmem_limit_bytes=V7X_VMEM_LIMIT),
        name="mem_kv",
    )(mem, gn, wk, wv, gk)


def _out_mem_body(x_ref, om_ref, og_ref, wo_ref, gn_ref, wq_ref, gq_ref, km_ref, vm_ref,
                  wmo_ref, o_ref):
    n_mla = om_ref.shape[2]
    x = x_ref[0] + _dot(om_ref[0], wo_ref[:n_mla, :]) + _dot(og_ref[0], wo_ref[n_mla:, :])
    h = _rms(x, gn_ref[...]).astype(BF16)
    qm = _dot(h, wq_ref[...])
    outs = []
    for hd in range(MEM_HEADS):
        c = hd * MEM_HEAD_DIM
        qn = (_rms(qm[:, c:c + MEM_HEAD_DIM], gq_ref[...]) * (MEM_HEAD_DIM ** -0.5)).astype(BF16)
        s = _dot_nt(qn, km_ref[0, :, c:c + MEM_HEAD_DIM])
        p = jnp.exp(s - jnp.max(s, axis=-1, keepdims=True))
        l = jnp.sum(p, axis=-1, keepdims=True)
        oh = _dot(p.astype(BF16), vm_ref[0, :, c:c + MEM_HEAD_DIM]) * (1.0 / l)
        outs.append(oh.astype(BF16))
    o_ref[0] = x + _dot(jnp.concatenate(outs, axis=1), wmo_ref[...])


def _out_mem(x, om, og, wo, gn, wq, gq, km, vm, wmo):
    b, s, d = x.shape
    tm = OUT_TM
    n_mem, wm = km.shape[1], km.shape[2]
    tok = lambda w: pl.BlockSpec((1, tm, w), lambda i, j: (i, j, 0))
    memspec = pl.BlockSpec((1, n_mem, wm), lambda i, j: (i, 0, 0))
    return pl.pallas_call(
        _out_mem_body,
        grid=(b, s // tm),
        in_specs=[tok(d), tok(om.shape[2]), tok(og.shape[2]),
                  _resident(wo.shape), _resident(gn.shape), _resident(wq.shape),
                  _resident(gq.shape), memspec, memspec, _resident(wmo.shape)],
        out_specs=tok(d),
        out_shape=jax.ShapeDtypeStruct((b, s, d), F32),
        compiler_params=pltpu.CompilerParams(
            dimension_semantics=("parallel", "parallel"), vmem_limit_bytes=V7X_VMEM_LIMIT),
        name="out_mem",
    )(x, om, og, wo, gn, wq, gq, km, vm, wmo)


def _pad_rope(w):
    z = jnp.zeros(w.shape[:-1] + (LANES // 2 - HALF_ROPE,), w.dtype)
    return jnp.concatenate([w[..., :HALF_ROPE], z, w[..., HALF_ROPE:], z], axis=-1)


def _pad_head(w):
    return jnp.concatenate([w[..., :MLA_NOPE], _pad_rope(w[..., MLA_NOPE:])], axis=-1)


def _w_in_columns():
    o = 0
    parts = {}
    for name, size in (("zq", MLA_Q_RANK), ("zkv", MLA_KV_RANK), ("zkr", MLA_ROPE),
                       ("gq", GLA_HEADS * GLA_DK), ("gk", GLA_HEADS * GLA_DK),
                       ("gv", GLA_HEADS * GLA_DV), ("zg", GLA_GATE_RANK),
                       ("zr", GLA_HEADS * GLA_DV)):
        parts[name] = (o, o + size)
        o += size
    return parts


def _split_w_in_body(w_ref, lat_ref, krg_ref, gla_ref, zr_ref):
    parts = _w_in_columns()
    w = w_ref[...]
    cols = lambda a, b: w[:, parts[a][0]:parts[b][1]]
    kr, zg = cols("zkr", "zkr"), cols("zg", "zg")
    zeros = lambda n: jnp.zeros((w.shape[0], n), w.dtype)
    lat_ref[...] = cols("zq", "zkv").astype(BF16)
    krg_ref[...] = jnp.concatenate(
        [kr[:, :HALF_ROPE], zg, zeros(LANES // 2 - HALF_ROPE - GLA_GATE_RANK),
         kr[:, HALF_ROPE:], zeros(LANES // 2 - HALF_ROPE)], axis=1).astype(BF16)
    gla_ref[...] = cols("gq", "gv").astype(BF16)
    zr_ref[...] = cols("zr", "zr").astype(BF16)


def _split_w_in(w_in):
    k, n = w_in.shape
    parts = _w_in_columns()
    widths = [parts["zkv"][1] - parts["zq"][0], LANES,
              parts["gv"][1] - parts["gq"][0], parts["zr"][1] - parts["zr"][0]]
    tr = W_IN_ROWS
    return pl.pallas_call(
        _split_w_in_body,
        grid=(k // tr,),
        in_specs=[pl.BlockSpec((tr, n), lambda i: (i, 0))],
        out_specs=[pl.BlockSpec((tr, w), lambda i: (i, 0)) for w in widths],
        out_shape=[jax.ShapeDtypeStruct((k, w), BF16) for w in widths],
        compiler_params=pltpu.CompilerParams(
            dimension_semantics=("parallel",), vmem_limit_bytes=V7X_VMEM_LIMIT),
        name="split_w_in",
    )(w_in)


def kernel(x, mem, positions, ffn1_norm, ffn1_w_gate, ffn1_w_up, ffn1_w_down, mix_norm, w_in, q_a_norm, w_q_up, kv_a_norm, w_kv_up, mla_q_norm, mla_k_norm, gla_w_gate2, gla_b_gate, gla_out_norm, w_out, mem_attn_norm, mem_norm, mem_w_q, mem_w_k, mem_w_v, mem_w_o, mem_q_norm, mem_k_norm, ffn2_norm, ffn2_w_gate, ffn2_w_up, ffn2_w_down):
    b, s, d = x.shape
    m = b * s
    depth = w_in.shape[0]
    row = lambda v: v.reshape(1, -1)
    bf = lambda w: w.astype(BF16)

    half = jnp.arange(HALF_ROPE, dtype=F32)
    inv_freq = ROPE_THETA ** (-half / HALF_ROPE)
    invf = _pad_rope(jnp.concatenate([inv_freq, inv_freq])).reshape(1, LANES)
    pos = positions.reshape(m, 1)

    xf = x.reshape(m, d)
    for l in range(depth):
        xf = _ffn(xf, row(ffn1_norm[l]), ffn1_w_gate[l], ffn1_w_up[l], ffn1_w_down[l])

        wq = _pad_head(w_q_up[l].reshape(MLA_Q_RANK, MLA_HEADS, MLA_QK)).reshape(MLA_Q_RANK, -1)
        wkv = w_kv_up[l].reshape(MLA_KV_RANK, MLA_HEADS, MLA_NOPE + MLA_V)
        wkv = jnp.concatenate([wkv[..., :MLA_NOPE].reshape(MLA_KV_RANK, -1),
                               wkv[..., MLA_NOPE:].reshape(MLA_KV_RANK, -1)], axis=1)
        w2 = jnp.pad(gla_w_gate2[l], ((HALF_ROPE, LANES - HALF_ROPE - GLA_GATE_RANK), (0, 0)))
        q, k, v, gq, gk, gv, la, zr = _mixer_proj(
            xf, pos, row(mix_norm[l]), *_split_w_in(w_in[l]), row(q_a_norm[l]), bf(wq),
            row(kv_a_norm[l]), bf(wkv), row(_pad_head(mla_q_norm[l])), row(_pad_head(mla_k_norm[l])),
            invf, bf(w2), row(gla_b_gate[l]))

        r3 = lambda a: a.reshape(b, s, a.shape[-1])
        o_mla = _mla_attn(r3(q), r3(k), r3(v))
        o_gla = _gla(r3(gq), r3(gk), r3(gv), r3(la), r3(zr), row(gla_out_norm[l]))

        km, vm = _mem_kv(mem, row(mem_norm[l]), bf(mem_w_k[l]), bf(mem_w_v[l]), row(mem_k_norm[l]))
        x3 = _out_mem(r3(xf), o_mla, o_gla, bf(w_out[l]), row(mem_attn_norm[l]), bf(mem_w_q[l]),
                      row(mem_q_norm[l]), km, vm, bf(mem_w_o[l]))

        xf = _ffn(x3.reshape(m, d), row(ffn2_norm[l]), ffn2_w_gate[l], ffn2_w_up[l], ffn2_w_down[l])
    return xf.reshape(b, s, d)
```

```python
import jax
import jax.numpy as jnp
from jax import lax
from jax.experimental import pallas as pl
from jax.experimental.pallas import tpu as pltpu

F32 = jnp.float32
BF16 = jnp.bfloat16

EPS = 1e-6
CHUNK = 64
ROPE_THETA = 10000.0
GLA_TAU = 16.0
LOG2_E = 1.4426950408889634

MLA_HEADS = 8
MLA_NOPE = 128
MLA_ROPE = 64
MLA_QK = MLA_NOPE + MLA_ROPE
MLA_V = 128
MLA_Q_RANK = 512
MLA_KV_RANK = 256
GLA_HEADS = 4
GLA_DK = 128
GLA_DV = 256
GLA_GATE_RANK = 16
MEM_HEADS = 4
MEM_HEAD_DIM = 128

LANES = 128
HEAD_PAD = 2 * LANES
HALF_ROPE = MLA_ROPE // 2

V7X_VMEM_LIMIT = 56 * 1024 * 1024

FFN_TM = 1024
FFN_TF = 512
FFN_DOWN_SPLIT = ((0, 512), (512, 1024), (1024, 1536), (1536, 1792), (1792, 2048))
FFN_NORM_CHUNKS = 8
PROJ_TM = 512
ATTN_TQ = 256
ATTN_HEADS_PER_STEP = 1
GLA_GROUP = 256
OUT_TM = 512
W_IN_ROWS = 256


def _rms(x, g):
    ms = jnp.mean(x * x, axis=-1, keepdims=True)
    return x * lax.rsqrt(ms + EPS) * g


def _dot(a, b):
    return jnp.dot(a, b, preferred_element_type=F32)


def _dot_nt(a, b):
    return lax.dot_general(a, b, (((1,), (1,)), ((), ())), preferred_element_type=F32)


def _resident(shape):
    nd = len(shape)
    return pl.BlockSpec(shape, lambda *_: (0,) * nd, pipeline_mode=pl.Buffered(1))


def _ffn_body(x_hbm, g_ref, wg_ref, wu_ref, wd_ref, o_hbm, xbuf, acc, h_ref, x_sem, o_sem):
    i, f = pl.program_id(0), pl.program_id(1)
    n_i, n_f = pl.num_programs(0), pl.num_programs(1)
    tm, d = acc.shape
    cur = i % 2
    norm_rows = tm // FFN_NORM_CHUNKS

    def rows(tile):
        return pl.ds(pl.multiple_of(tile * tm, tm), tm)

    def x_copy(tile):
        return pltpu.make_async_copy(x_hbm.at[rows(tile), :], xbuf, x_sem)

    def o_copy(tile):
        return pltpu.make_async_copy(acc, o_hbm.at[rows(tile), :], o_sem)

    @pl.when((i == 0) & (f == 0))
    def _():
        x_copy(0).start()
        x_copy(0).wait()
        h_ref[0] = _rms(xbuf[...], g_ref[...]).astype(BF16)

    @pl.when((f == 1) & (i + 1 < n_i))
    def _():
        x_copy(i + 1).wait()

    def body(first):
        h = h_ref[cur]
        chunk = jnp.clip(f - 1, 0, FFN_NORM_CHUNKS - 1)
        rs = pl.ds(pl.multiple_of(chunk * norm_rows, norm_rows), norm_rows)
        h_ref[1 - cur, rs, :] = _rms(xbuf[rs, :], g_ref[...]).astype(BF16)

        gate = _dot(h, wg_ref[...].astype(BF16))
        up = _dot(h, wu_ref[...].astype(BF16))
        a = ((0.5 * gate) * jax.nn.sigmoid(gate) * up).astype(BF16)
        if first:
            @pl.when(i > 0)
            def _():
                o_copy(i - 1).wait()

        for c0, c1 in FFN_DOWN_SPLIT:
            part = _dot(a, wd_ref[:, c0:c1].astype(BF16))
            if first:
                acc[:, c0:c1] = xbuf[:, c0:c1] + part
            else:
                acc[:, c0:c1] += part

    @pl.when(f == 0)
    def _():
        body(True)

        @pl.when(i + 1 < n_i)
        def _():
            x_copy(i + 1).start()

    @pl.when(f > 0)
    def _():
        body(False)

    @pl.when(f == n_f - 1)
    def _():
        o_copy(i).start()

        @pl.when(i == n_i - 1)
        def _():
            o_copy(i).wait()


def _ffn(x, g, wg, wu, wd):
    m, d = x.shape
    f = wg.shape[1]
    assert f // FFN_TF > FFN_NORM_CHUNKS and FFN_TM % (16 * FFN_NORM_CHUNKS) == 0
    assert FFN_DOWN_SPLIT[-1][1] == d
    return pl.pallas_call(
        _ffn_body,
        grid=(m // FFN_TM, f // FFN_TF),
        in_specs=[
            pl.BlockSpec(memory_space=pl.ANY),
            pl.BlockSpec((1, d), lambda i, j: (0, 0)),
            pl.BlockSpec((d, FFN_TF), lambda i, j: (0, j)),
            pl.BlockSpec((d, FFN_TF), lambda i, j: (0, j)),
            pl.BlockSpec((FFN_TF, d), lambda i, j: (j, 0)),
        ],
        out_specs=pl.BlockSpec(memory_space=pl.ANY),
        out_shape=jax.ShapeDtypeStruct((m, d), F32),
        scratch_shapes=[pltpu.VMEM((FFN_TM, d), F32),
                        pltpu.VMEM((FFN_TM, d), F32),
                        pltpu.VMEM((2, FFN_TM, d), BF16),
                        pltpu.SemaphoreType.DMA(()),
                        pltpu.SemaphoreType.DMA(())],
        compiler_params=pltpu.CompilerParams(
            dimension_semantics=("arbitrary", "arbitrary"),
            vmem_limit_bytes=V7X_VMEM_LIMIT),
        name="ffn",
    )(x, g, wg, wu, wd)


def _mixer_proj_body(x_ref, pos_ref, gmix_ref, w_lat_ref, w_krg_ref, w_gla_ref, w_zr_ref,
                     gqa_ref, wq_ref, gkva_ref, wkv_ref, gq_ref, gk_ref, invf_ref, w2_ref, b2_ref,
                     q_out, k_out, v_out, gq_out, gk_out, gv_out, la_out, zr_out):
    h = _rms(x_ref[...], gmix_ref[...]).astype(BF16)
    n_qk = GLA_HEADS * GLA_DK
    zq = _dot(h, w_lat_ref[:, :MLA_Q_RANK])
    zkv = _dot(h, w_lat_ref[:, MLA_Q_RANK:])
    zkrg = _dot(h, w_krg_ref[...])
    lane = lax.broadcasted_iota(jnp.int32, (1, LANES), 1)
    zkr = jnp.where(lane % (LANES // 2) < HALF_ROPE, zkrg, 0.0)
    gq_out[...] = (_dot(h, w_gla_ref[:, :n_qk]) * (GLA_DK ** -0.5)).astype(BF16)
    gk_out[...] = _dot(h, w_gla_ref[:, n_qk:2 * n_qk]).astype(BF16)

    pre = _dot(zkrg.astype(BF16), w2_ref[...]) + b2_ref[...]
    log_sig = jnp.minimum(pre, 0.0) - jnp.log1p(jnp.exp(-jnp.abs(pre)))
    la_out[...] = log_sig * (LOG2_E / GLA_TAU)

    ang = pos_ref[...].astype(F32) * invf_ref[...]
    cos_t = jnp.cos(ang)
    sin_t = jnp.sin(ang) * jnp.where(lane < LANES // 2, -1.0, 1.0)

    def rope(v):
        return v * cos_t + pltpu.roll(v, LANES // 2, axis=1) * sin_t

    inv_qk = 1.0 / MLA_QK

    qa = _rms(zq, gqa_ref[...]).astype(BF16)
    kva = _rms(zkv, gkva_ref[...]).astype(BF16)
    qf = _dot(qa, wq_ref[...])
    kvf = _dot(kva, wkv_ref[...])
    n_v = GLA_HEADS * GLA_DV
    gv_out[:, :n_v // 2] = _dot(h, w_gla_ref[:, 2 * n_qk:2 * n_qk + n_v // 2]).astype(BF16)
    for hd in range(MLA_HEADS):
        c = hd * HEAD_PAD
        blk = qf[:, c:c + HEAD_PAD]
        ss = jnp.sum(blk * blk, axis=-1, keepdims=True)
        rinv = lax.rsqrt(ss * inv_qk + EPS) * (MLA_QK ** -0.5 * LOG2_E)
        q_out[:, c:c + LANES] = (blk[:, :LANES] * rinv * gq_ref[:, :LANES]).astype(BF16)
        q_out[:, c + LANES:c + HEAD_PAD] = rope(
            blk[:, LANES:] * rinv * gq_ref[:, LANES:]).astype(BF16)

    gv_out[:, n_v // 2:] = _dot(h, w_gla_ref[:, 2 * n_qk + n_v // 2:]).astype(BF16)

    ssr = jnp.sum(zkr * zkr, axis=-1, keepdims=True)
    kr_base = rope(zkr * gk_ref[:, LANES:])
    for hd in range(MLA_HEADS):
        kn = kvf[:, hd * MLA_NOPE:(hd + 1) * MLA_NOPE]
        ss = jnp.sum(kn * kn, axis=-1, keepdims=True) + ssr
        rinv = lax.rsqrt(ss * inv_qk + EPS)
        c = hd * HEAD_PAD
        k_out[:, c:c + LANES] = (kn * rinv * gk_ref[:, :LANES]).astype(BF16)
        k_out[:, c + LANES:c + HEAD_PAD] = (kr_base * rinv).astype(BF16)
    v_out[...] = kvf[:, MLA_HEADS * MLA_NOPE:].astype(BF16)
    zr_out[...] = _dot(h, w_zr_ref[...]).astype(BF16)


def _mixer_proj(x, pos, *weights):
    m, d = x.shape
    tm = PROJ_TM
    row = lambda i: (i, 0)
    widths = [
        (MLA_HEADS * HEAD_PAD, BF16),
        (MLA_HEADS * HEAD_PAD, BF16),
        (MLA_HEADS * MLA_V, BF16),
        (GLA_HEADS * GLA_DK, BF16),
        (GLA_HEADS * GLA_DK, BF16),
        (GLA_HEADS * GLA_DV, BF16),
        (GLA_HEADS * GLA_DK, F32),
        (GLA_HEADS * GLA_DV, BF16),
    ]
    return pl.pallas_call(
        _mixer_proj_body,
        grid=(m // tm,),
        in_specs=[pl.BlockSpec((tm, d), row), pl.BlockSpec((tm, 1), row)]
        + [_resident(w.shape) for w in weights],
        out_specs=[pl.BlockSpec((tm, w), row) for w, _ in widths],
        out_shape=[jax.ShapeDtypeStruct((m, w), dt) for w, dt in widths],
        compiler_params=pltpu.CompilerParams(
            dimension_semantics=("parallel",), vmem_limit_bytes=V7X_VMEM_LIMIT),
        name="mixer_proj",
    )(x, pos, *weights)


def _mla_attn_body(q_ref, k_ref, v_ref, o_ref, vext_ref):
    seq = q_ref.shape[1]
    dv = MLA_V
    tq = ATTN_TQ
    r = lax.broadcasted_iota(jnp.int32, (tq, tq), 0) // CHUNK
    c = lax.broadcasted_iota(jnp.int32, (tq, tq), 1) // CHUNK
    diag_mask = c <= r

    for hh in range(ATTN_HEADS_PER_STEP):
        vext_ref[:, 2 * hh * dv:(2 * hh + 1) * dv] = v_ref[0, :, hh * dv:(hh + 1) * dv]
        vext_ref[:, (2 * hh + 1) * dv:(2 * hh + 2) * dv] = jnp.ones((seq, dv), vext_ref.dtype)

    def scores(job):
        hh, i = job
        lo = i * tq
        cols = slice(hh * HEAD_PAD, (hh + 1) * HEAD_PAD)
        q = q_ref[0, lo:lo + tq, cols]
        s_d = jnp.where(diag_mask, _dot_nt(q, k_ref[0, lo:lo + tq, cols]), -jnp.inf)
        s_o = _dot_nt(q, k_ref[0, :lo, cols]) if i > 0 else None
        return s_d, s_o

    jobs = [(hh, i) for i in range(seq // tq)[::-1] for hh in range(ATTN_HEADS_PER_STEP)]
    nxt = scores(jobs[0])
    for pos, (hh, i) in enumerate(jobs):
        lo = i * tq
        s_d, s_o = nxt
        if pos + 1 < len(jobs):
            nxt = scores(jobs[pos + 1])
        m = jnp.max(s_d, axis=-1, keepdims=True)
        if i > 0:
            m = jnp.maximum(m, jnp.max(s_o, axis=-1, keepdims=True))
        vcols = slice(2 * hh * dv, (2 * hh + 2) * dv)
        pv = _dot(jnp.exp2(s_d - m).astype(BF16), vext_ref[lo:lo + tq, vcols])
        if i > 0:
            pv = pv + _dot(jnp.exp2(s_o - m).astype(BF16), vext_ref[:lo, vcols])
        o_ref[0, lo:lo + tq, hh * dv:(hh + 1) * dv] = (
            pv[:, :dv] * (1.0 / pv[:, dv:dv + 1])).astype(o_ref.dtype)


def _mla_attn(q, k, v):
    b, s, _ = q.shape
    hps = ATTN_HEADS_PER_STEP
    qk_spec = pl.BlockSpec((1, s, hps * HEAD_PAD), lambda i, j: (i, 0, j))
    v_spec = pl.BlockSpec((1, s, hps * MLA_V), lambda i, j: (i, 0, j))
    return pl.pallas_call(
        _mla_attn_body,
        grid=(b, MLA_HEADS // hps),
        in_specs=[qk_spec, qk_spec, v_spec],
        out_specs=v_spec,
        out_shape=jax.ShapeDtypeStruct((b, s, MLA_HEADS * MLA_V), BF16),
        scratch_shapes=[pltpu.VMEM((s, hps * 2 * MLA_V), BF16)],
        compiler_params=pltpu.CompilerParams(
            dimension_semantics=("parallel", "parallel"), vmem_limit_bytes=V7X_VMEM_LIMIT),
        name="mla_attn",
    )(q, k, v)


def _gla_body(q_ref, k_ref, v_ref, la_ref, zr_ref, g_ref, o_ref, u_ref, st_ref, kd_ref, oraw_ref):
    seq, k_dim = k_ref.shape[1], k_ref.shape[2]
    v_dim = v_ref.shape[2]
    n_chunks = seq // CHUNK
    grp = GLA_GROUP
    r = lax.broadcasted_iota(jnp.int32, (grp, grp), 0)
    c = lax.broadcasted_iota(jnp.int32, (grp, grp), 1)
    later = ((c > r) & (c // CHUNK == r // CHUNK)).astype(BF16)

    b_end = jnp.sum(la_ref[0].reshape(n_chunks, CHUNK, k_dim), axis=1)
    dec = jnp.concatenate([jnp.exp2(b_end), jnp.zeros((LANES - n_chunks, k_dim), F32)], axis=0)
    dec_t = dec.T

    for n in range(seq // grp):
        sl = slice(n * grp, (n + 1) * grp)
        g = la_ref[0, sl, :]
        g_hi = g.astype(BF16)
        g_lo = (g - g_hi.astype(F32)).astype(BF16)
        rest2 = _dot(later, jnp.concatenate([g_hi, g_lo], axis=1))
        rest = rest2[:, :k_dim] + rest2[:, k_dim:]
        kd_ref[sl, :] = (k_ref[0, sl, :].astype(F32) * jnp.exp2(rest)).astype(BF16)

    for n in range(n_chunks):
        sl = slice(n * CHUNK, (n + 1) * CHUNK)
        u_ref[n] = lax.dot_general(kd_ref[sl, :], v_ref[0, sl, :], (((0,), (0,)), ((), ())),
                                   preferred_element_type=F32)

    state = jnp.zeros((k_dim, v_dim), F32)
    for n in range(n_chunks):
        state = state * dec_t[:, n:n + 1] + u_ref[n]
        st_ref[n] = state.astype(BF16)

    for n in range(n_chunks):
        sl = slice(n * CHUNK, (n + 1) * CHUNK)
        oraw_ref[sl, :] = _dot(q_ref[0, sl, :], st_ref[n])

    for n in range(seq // grp):
        sl = slice(n * grp, (n + 1) * grp)
        zr = zr_ref[0, sl, :].astype(F32)
        o_ref[0, sl, :] = (_rms(oraw_ref[sl, :], g_ref[...])
                           * (zr * jax.nn.sigmoid(zr))).astype(o_ref.dtype)


def _gla(q, k, v, la, zr, g):
    b, s, _ = q.shape
    n_chunks = s // CHUNK
    kspec = pl.BlockSpec((1, s, GLA_DK), lambda i, j: (i, 0, j))
    vspec = pl.BlockSpec((1, s, GLA_DV), lambda i, j: (i, 0, j))
    return pl.pallas_call(
        _gla_body,
        grid=(b, GLA_HEADS),
        in_specs=[kspec, kspec, vspec, kspec, vspec,
                  pl.BlockSpec((1, GLA_DV), lambda i, j: (0, 0))],
        out_specs=vspec,
        out_shape=jax.ShapeDtypeStruct((b, s, GLA_HEADS * GLA_DV), BF16),
        scratch_shapes=[pltpu.VMEM((n_chunks, GLA_DK, GLA_DV), F32),
                        pltpu.VMEM((n_chunks, GLA_DK, GLA_DV), BF16),
                        pltpu.VMEM((s, GLA_DK), BF16),
                        pltpu.VMEM((s, GLA_DV), F32)],
        compiler_params=pltpu.CompilerParams(
            dimension_semantics=("parallel", "parallel"), vmem_limit_bytes=V7X_VMEM_LIMIT),
        name="gla",
    )(q, k, v, la, zr, g)


def _mem_kv_body(m_ref, gn_ref, wk_ref, wv_ref, gk_ref, k_ref, v_ref):
    hm = _rms(m_ref[0], gn_ref[...]).astype(BF16)
    kf = _dot(hm, wk_ref[...])
    for hd in range(MEM_HEADS):
        c = hd * MEM_HEAD_DIM
        k_ref[0, :, c:c + MEM_HEAD_DIM] = _rms(kf[:, c:c + MEM_HEAD_DIM], gk_ref[...]).astype(BF16)
    v_ref[0] = _dot(hm, wv_ref[...]).astype(BF16)


def _mem_kv(mem, gn, wk, wv, gk):
    b, n, d = mem.shape
    w = wk.shape[1]
    ospec = pl.BlockSpec((1, n, w), lambda i: (i, 0, 0))
    return pl.pallas_call(
        _mem_kv_body,
        grid=(b,),
        in_specs=[pl.BlockSpec((1, n, d), lambda i: (i, 0, 0)),
                  _resident(gn.shape), _resident(wk.shape), _resident(wv.shape),
                  _resident(gk.shape)],
        out_specs=[ospec, ospec],
        out_shape=[jax.ShapeDtypeStruct((b, n, w), BF16)] * 2,
        compiler_params=pltpu.CompilerParams(
            dimension_semantics=("parallel",), vmem_limit_bytes=V7X_VMEM_LIMIT),
        name="mem_kv",
    )(mem, gn, wk, wv, gk)


def _out_mem_body(x_ref, om_ref, og_ref, wo_ref, gn_ref, wq_ref, gq_ref, km_ref, vm_ref,
                  wmo_ref, o_ref):
    n_mla = om_ref.shape[2]
    x = x_ref[0] + _dot(om_ref[0], wo_ref[:n_mla, :]) + _dot(og_ref[0], wo_ref[n_mla:, :])
    h = _rms(x, gn_ref[...]).astype(BF16)
    qm = _dot(h, wq_ref[...])
    outs = []
    for hd in range(MEM_HEADS):
        c = hd * MEM_HEAD_DIM
        qn = (_rms(qm[:, c:c + MEM_HEAD_DIM], gq_ref[...]) * (MEM_HEAD_DIM ** -0.5)).astype(BF16)
        s = _dot_nt(qn, km_ref[0, :, c:c + MEM_HEAD_DIM])
        p = jnp.exp(s - jnp.max(s, axis=-1, keepdims=True))
        l = jnp.sum(p, axis=-1, keepdims=True)
        oh = _dot(p.astype(BF16), vm_ref[0, :, c:c + MEM_HEAD_DIM]) * (1.0 / l)
        outs.append(oh.astype(BF16))
    o_ref[0] = x + _dot(jnp.concatenate(outs, axis=1), wmo_ref[...])


def _out_mem(x, om, og, wo, gn, wq, gq, km, vm, wmo):
    b, s, d = x.shape
    tm = OUT_TM
    n_mem, wm = km.shape[1], km.shape[2]
    tok = lambda w: pl.BlockSpec((1, tm, w), lambda i, j: (i, j, 0))
    memspec = pl.BlockSpec((1, n_mem, wm), lambda i, j: (i, 0, 0))
    return pl.pallas_call(
        _out_mem_body,
        grid=(b, s // tm),
        in_specs=[tok(d), tok(om.shape[2]), tok(og.shape[2]),
                  _resident(wo.shape), _resident(gn.shape), _resident(wq.shape),
                  _resident(gq.shape), memspec, memspec, _resident(wmo.shape)],
        out_specs=tok(d),
        out_shape=jax.ShapeDtypeStruct((b, s, d), F32),
        compiler_params=pltpu.CompilerParams(
            dimension_semantics=("parallel", "parallel"), vmem_limit_bytes=V7X_VMEM_LIMIT),
        name="out_mem",
    )(x, om, og, wo, gn, wq, gq, km, vm, wmo)


def _pad_rope(w):
    z = jnp.zeros(w.shape[:-1] + (LANES // 2 - HALF_ROPE,), w.dtype)
    return jnp.concatenate([w[..., :HALF_ROPE], z, w[..., HALF_ROPE:], z], axis=-1)


def _pad_head(w):
    return jnp.concatenate([w[..., :MLA_NOPE], _pad_rope(w[..., MLA_NOPE:])], axis=-1)


def _w_in_columns():
    o = 0
    parts = {}
    for name, size in (("zq", MLA_Q_RANK), ("zkv", MLA_KV_RANK), ("zkr", MLA_ROPE),
                       ("gq", GLA_HEADS * GLA_DK), ("gk", GLA_HEADS * GLA_DK),
                       ("gv", GLA_HEADS * GLA_DV), ("zg", GLA_GATE_RANK),
                       ("zr", GLA_HEADS * GLA_DV)):
        parts[name] = (o, o + size)
        o += size
    return parts


def _split_w_in_body(wt_ref, lat_ref, krg_ref, gla_ref, zr_ref):
    parts = _w_in_columns()
    rows = lambda a, b: wt_ref[parts[a][0]:parts[b][1], :]
    kr, zg = rows("zkr", "zkr"), rows("zg", "zg")
    zeros = lambda n: jnp.zeros((n, wt_ref.shape[1]), F32)
    krg_t = jnp.concatenate(
        [kr[:HALF_ROPE], zg, zeros(LANES // 2 - HALF_ROPE - GLA_GATE_RANK),
         kr[HALF_ROPE:], zeros(LANES // 2 - HALF_ROPE)], axis=0)
    lat_ref[...] = rows("zq", "zkv").T.astype(BF16)
    krg_ref[...] = krg_t.T.astype(BF16)
    gla_ref[...] = rows("gq", "gv").T.astype(BF16)
    zr_ref[...] = rows("zr", "zr").T.astype(BF16)


def _split_w_in(w_in, layer):
    wt = jnp.swapaxes(w_in, 1, 2)
    _, n, k = wt.shape
    parts = _w_in_columns()
    widths = [parts["zkv"][1] - parts["zq"][0], LANES,
              parts["gv"][1] - parts["gq"][0], parts["zr"][1] - parts["zr"][0]]
    tr = W_IN_ROWS
    return pl.pallas_call(
        _split_w_in_body,
        grid=(k // tr,),
        in_specs=[pl.BlockSpec((None, n, tr), lambda i: (layer, 0, i))],
        out_specs=[pl.BlockSpec((tr, w), lambda i: (i, 0)) for w in widths],
        out_shape=[jax.ShapeDtypeStruct((k, w), BF16) for w in widths],
        compiler_params=pltpu.CompilerParams(
            dimension_semantics=("parallel",), vmem_limit_bytes=V7X_VMEM_LIMIT),
        name="split_w_in",
    )(wt)


def kernel(x, mem, positions, ffn1_norm, ffn1_w_gate, ffn1_w_up, ffn1_w_down, mix_norm, w_in, q_a_norm, w_q_up, kv_a_norm, w_kv_up, mla_q_norm, mla_k_norm, gla_w_gate2, gla_b_gate, gla_out_norm, w_out, mem_attn_norm, mem_norm, mem_w_q, mem_w_k, mem_w_v, mem_w_o, mem_q_norm, mem_k_norm, ffn2_norm, ffn2_w_gate, ffn2_w_up, ffn2_w_down):
    b, s, d = x.shape
    m = b * s
    depth = w_in.shape[0]
    row = lambda v: v.reshape(1, -1)
    bf = lambda w: w.astype(BF16)

    half = jnp.arange(HALF_ROPE, dtype=F32)
    inv_freq = ROPE_THETA ** (-half / HALF_ROPE)
    invf = _pad_rope(jnp.concatenate([inv_freq, inv_freq])).reshape(1, LANES)
    pos = positions.reshape(m, 1)

    xf = x.reshape(m, d)
    for l in range(depth):
        xf = _ffn(xf, row(ffn1_norm[l]), ffn1_w_gate[l], ffn1_w_up[l], ffn1_w_down[l])

        wq = _pad_head(w_q_up[l].reshape(MLA_Q_RANK, MLA_HEADS, MLA_QK)).reshape(MLA_Q_RANK, -1)
        wkv = w_kv_up[l].reshape(MLA_KV_RANK, MLA_HEADS, MLA_NOPE + MLA_V)
        wkv = jnp.concatenate([wkv[..., :MLA_NOPE].reshape(MLA_KV_RANK, -1),
                               wkv[..., MLA_NOPE:].reshape(MLA_KV_RANK, -1)], axis=1)
        w2 = jnp.pad(gla_w_gate2[l], ((HALF_ROPE, LANES - HALF_ROPE - GLA_GATE_RANK), (0, 0)))
        q, k, v, gq, gk, gv, la, zr = _mixer_proj(
            xf, pos, row(mix_norm[l]), *_split_w_in(w_in, l), row(q_a_norm[l]), bf(wq),
            row(kv_a_norm[l]), bf(wkv), row(_pad_head(mla_q_norm[l])), row(_pad_head(mla_k_norm[l])),
            invf, bf(w2), row(gla_b_gate[l]))

        r3 = lambda a: a.reshape(b, s, a.shape[-1])
        o_mla = _mla_attn(r3(q), r3(k), r3(v))
        o_gla = _gla(r3(gq), r3(gk), r3(gv), r3(la), r3(zr), row(gla_out_norm[l]))

        km, vm = _mem_kv(mem, row(mem_norm[l]), bf(mem_w_k[l]), bf(mem_w_v[l]), row(mem_k_norm[l]))
        x3 = _out_mem(r3(xf), o_mla, o_gla, bf(w_out[l]), row(mem_attn_norm[l]), bf(mem_w_q[l]),
                      row(mem_q_norm[l]), km, vm, bf(mem_w_o[l]))

        xf = _ffn(x3.reshape(m, d), row(ffn2_norm[l]), ffn2_w_gate[l], ffn2_w_up[l], ffn2_w_down[l])
    return xf.reshape(b, s, d)
```

```python
import jax
import jax.numpy as jnp
from jax import lax
from jax.experimental import pallas as pl
from jax.experimental.pallas import tpu as pltpu

F32 = jnp.float32
BF16 = jnp.bfloat16

EPS = 1e-6
CHUNK = 64
ROPE_THETA = 10000.0
GLA_TAU = 16.0
LOG2_E = 1.4426950408889634

MLA_HEADS = 8
MLA_NOPE = 128
MLA_ROPE = 64
MLA_QK = MLA_NOPE + MLA_ROPE
MLA_V = 128
MLA_Q_RANK = 512
MLA_KV_RANK = 256
GLA_HEADS = 4
GLA_DK = 128
GLA_DV = 256
GLA_GATE_RANK = 16
MEM_HEADS = 4
MEM_HEAD_DIM = 128

LANES = 128
HEAD_PAD = 2 * LANES
HALF_ROPE = MLA_ROPE // 2

V7X_VMEM_LIMIT = 56 * 1024 * 1024

FFN_TM = 1024
FFN_TF = 512
FFN_DOWN_SPLIT = ((0, 512), (512, 1024), (1024, 1536), (1536, 1792), (1792, 2048))
FFN_NORM_CHUNKS = 8
PROJ_TM = 512
ATTN_TQ = 256
ATTN_HEADS_PER_STEP = 1
GLA_GROUP = 256
OUT_TM = 512
W_IN_ROWS = 256


def _rms(x, g):
    ms = jnp.mean(x * x, axis=-1, keepdims=True)
    return x * lax.rsqrt(ms + EPS) * g


def _dot(a, b):
    return jnp.dot(a, b, preferred_element_type=F32)


def _dot_nt(a, b):
    return lax.dot_general(a, b, (((1,), (1,)), ((), ())), preferred_element_type=F32)


def _resident(shape):
    nd = len(shape)
    return pl.BlockSpec(shape, lambda *_: (0,) * nd, pipeline_mode=pl.Buffered(1))


def _ffn_body(x_hbm, g_ref, wg_ref, wu_ref, wd_ref, o_hbm, xbuf, acc, h_ref, x_sem, o_sem):
    i, f = pl.program_id(0), pl.program_id(1)
    n_i, n_f = pl.num_programs(0), pl.num_programs(1)
    tm, d = acc.shape
    cur = i % 2
    norm_rows = tm // FFN_NORM_CHUNKS

    def rows(tile):
        return pl.ds(pl.multiple_of(tile * tm, tm), tm)

    def x_copy(tile, chunk):
        r0 = pl.multiple_of(chunk * norm_rows, norm_rows)
        return pltpu.make_async_copy(
            x_hbm.at[pl.ds(pl.multiple_of(tile * tm, tm) + r0, norm_rows), :],
            xbuf.at[pl.ds(r0, norm_rows), :], x_sem.at[chunk])

    def o_copy(tile):
        return pltpu.make_async_copy(acc, o_hbm.at[rows(tile), :], o_sem)

    @pl.when((i == 0) & (f == 0))
    def _():
        for c in range(FFN_NORM_CHUNKS):
            x_copy(0, c).start()
        for c in range(FFN_NORM_CHUNKS):
            x_copy(0, c).wait()
        h_ref[0] = _rms(xbuf[...], g_ref[...]).astype(BF16)

    @pl.when((f >= 1) & (f <= FFN_NORM_CHUNKS) & (i + 1 < n_i))
    def _():
        x_copy(i + 1, f - 1).wait()

    def body(first):
        h = h_ref[cur]
        chunk = jnp.clip(f - 1, 0, FFN_NORM_CHUNKS - 1)
        rs = pl.ds(pl.multiple_of(chunk * norm_rows, norm_rows), norm_rows)
        h_ref[1 - cur, rs, :] = _rms(xbuf[rs, :], g_ref[...]).astype(BF16)

        gate = _dot(h, wg_ref[...].astype(BF16))
        up = _dot(h, wu_ref[...].astype(BF16))
        a = ((0.5 * gate) * jax.nn.sigmoid(gate) * up).astype(BF16)
        if first:
            @pl.when(i > 0)
            def _():
                o_copy(i - 1).wait()

        for c0, c1 in FFN_DOWN_SPLIT:
            part = _dot(a, wd_ref[:, c0:c1].astype(BF16))
            if first:
                acc[:, c0:c1] = xbuf[:, c0:c1] + part
            else:
                acc[:, c0:c1] += part

    @pl.when(f == 0)
    def _():
        body(True)

        @pl.when(i + 1 < n_i)
        def _():
            for c in range(FFN_NORM_CHUNKS):
                x_copy(i + 1, c).start()

    @pl.when(f > 0)
    def _():
        body(False)

    @pl.when(f == n_f - 1)
    def _():
        o_copy(i).start()

        @pl.when(i == n_i - 1)
        def _():
            o_copy(i).wait()


def _ffn(x, g, wg, wu, wd):
    m, d = x.shape
    f = wg.shape[1]
    assert f // FFN_TF > FFN_NORM_CHUNKS and FFN_TM % (16 * FFN_NORM_CHUNKS) == 0
    assert FFN_DOWN_SPLIT[-1][1] == d
    return pl.pallas_call(
        _ffn_body,
        grid=(m // FFN_TM, f // FFN_TF),
        in_specs=[
            pl.BlockSpec(memory_space=pl.ANY),
            pl.BlockSpec((1, d), lambda i, j: (0, 0)),
            pl.BlockSpec((d, FFN_TF), lambda i, j: (0, j)),
            pl.BlockSpec((d, FFN_TF), lambda i, j: (0, j)),
            pl.BlockSpec((FFN_TF, d), lambda i, j: (j, 0)),
        ],
        out_specs=pl.BlockSpec(memory_space=pl.ANY),
        out_shape=jax.ShapeDtypeStruct((m, d), F32),
        scratch_shapes=[pltpu.VMEM((FFN_TM, d), F32),
                        pltpu.VMEM((FFN_TM, d), F32),
                        pltpu.VMEM((2, FFN_TM, d), BF16),
                        pltpu.SemaphoreType.DMA((FFN_NORM_CHUNKS,)),
                        pltpu.SemaphoreType.DMA(())],
        compiler_params=pltpu.CompilerParams(
            dimension_semantics=("arbitrary", "arbitrary"),
            vmem_limit_bytes=V7X_VMEM_LIMIT),
        name="ffn",
    )(x, g, wg, wu, wd)


def _mixer_proj_body(x_ref, pos_ref, gmix_ref, w_lat_ref, w_krg_ref, w_gla_ref, w_zr_ref,
                     gqa_ref, wq_ref, gkva_ref, wkv_ref, gq_ref, gk_ref, invf_ref, w2_ref, b2_ref,
                     q_out, k_out, v_out, gq_out, gk_out, gv_out, la_out, zr_out):
    h = _rms(x_ref[...], gmix_ref[...]).astype(BF16)
    n_qk = GLA_HEADS * GLA_DK
    zq = _dot(h, w_lat_ref[:, :MLA_Q_RANK])
    zkv = _dot(h, w_lat_ref[:, MLA_Q_RANK:])
    zkrg = _dot(h, w_krg_ref[...])
    lane = lax.broadcasted_iota(jnp.int32, (1, LANES), 1)
    zkr = jnp.where(lane % (LANES // 2) < HALF_ROPE, zkrg, 0.0)
    gq_out[...] = (_dot(h, w_gla_ref[:, :n_qk]) * (GLA_DK ** -0.5)).astype(BF16)
    gk_out[...] = _dot(h, w_gla_ref[:, n_qk:2 * n_qk]).astype(BF16)

    pre = _dot(zkrg.astype(BF16), w2_ref[...]) + b2_ref[...]
    log_sig = jnp.minimum(pre, 0.0) - jnp.log1p(jnp.exp(-jnp.abs(pre)))
    la_out[...] = log_sig * (LOG2_E / GLA_TAU)

    ang = pos_ref[...].astype(F32) * invf_ref[...]
    cos_t = jnp.cos(ang)
    sin_t = jnp.sin(ang) * jnp.where(lane < LANES // 2, -1.0, 1.0)

    def rope(v):
        return v * cos_t + pltpu.roll(v, LANES // 2, axis=1) * sin_t

    inv_qk = 1.0 / MLA_QK

    qa = _rms(zq, gqa_ref[...]).astype(BF16)
    kva = _rms(zkv, gkva_ref[...]).astype(BF16)
    qf = _dot(qa, wq_ref[...])
    kvf = _dot(kva, wkv_ref[...])
    n_v = GLA_HEADS * GLA_DV
    gv_out[:, :n_v // 2] = _dot(h, w_gla_ref[:, 2 * n_qk:2 * n_qk + n_v // 2]).astype(BF16)
    for hd in range(MLA_HEADS):
        c = hd * HEAD_PAD
        blk = qf[:, c:c + HEAD_PAD]
        ss = jnp.sum(blk * blk, axis=-1, keepdims=True)
        rinv = lax.rsqrt(ss * inv_qk + EPS) * (MLA_QK ** -0.5 * LOG2_E)
        q_out[:, c:c + LANES] = (blk[:, :LANES] * rinv * gq_ref[:, :LANES]).astype(BF16)
        q_out[:, c + LANES:c + HEAD_PAD] = rope(
            blk[:, LANES:] * rinv * gq_ref[:, LANES:]).astype(BF16)

    gv_out[:, n_v // 2:] = _dot(h, w_gla_ref[:, 2 * n_qk + n_v // 2:]).astype(BF16)

    ssr = jnp.sum(zkr * zkr, axis=-1, keepdims=True)
    kr_base = rope(zkr * gk_ref[:, LANES:])
    for hd in range(MLA_HEADS):
        kn = kvf[:, hd * MLA_NOPE:(hd + 1) * MLA_NOPE]
        ss = jnp.sum(kn * kn, axis=-1, keepdims=True) + ssr
        rinv = lax.rsqrt(ss * inv_qk + EPS)
        c = hd * HEAD_PAD
        k_out[:, c:c + LANES] = (kn * rinv * gk_ref[:, :LANES]).astype(BF16)
        k_out[:, c + LANES:c + HEAD_PAD] = (kr_base * rinv).astype(BF16)
    v_out[...] = kvf[:, MLA_HEADS * MLA_NOPE:].astype(BF16)
    zr_out[...] = _dot(h, w_zr_ref[...]).astype(BF16)


def _mixer_proj(x, pos, *weights):
    m, d = x.shape
    tm = PROJ_TM
    row = lambda i: (i, 0)
    widths = [
        (MLA_HEADS * HEAD_PAD, BF16),
        (MLA_HEADS * HEAD_PAD, BF16),
        (MLA_HEADS * MLA_V, BF16),
        (GLA_HEADS * GLA_DK, BF16),
        (GLA_HEADS * GLA_DK, BF16),
        (GLA_HEADS * GLA_DV, BF16),
        (GLA_HEADS * GLA_DK, F32),
        (GLA_HEADS * GLA_DV, BF16),
    ]
    return pl.pallas_call(
        _mixer_proj_body,
        grid=(m // tm,),
        in_specs=[pl.BlockSpec((tm, d), row), pl.BlockSpec((tm, 1), row)]
        + [_resident(w.shape) for w in weights],
        out_specs=[pl.BlockSpec((tm, w), row) for w, _ in widths],
        out_shape=[jax.ShapeDtypeStruct((m, w), dt) for w, dt in widths],
        compiler_params=pltpu.CompilerParams(
            dimension_semantics=("parallel",), vmem_limit_bytes=V7X_VMEM_LIMIT),
        name="mixer_proj",
    )(x, pos, *weights)


def _mla_attn_body(q_ref, k_ref, v_ref, o_ref, vext_ref):
    seq = q_ref.shape[1]
    dv = MLA_V
    tq = ATTN_TQ
    r = lax.broadcasted_iota(jnp.int32, (tq, tq), 0) // CHUNK
    c = lax.broadcasted_iota(jnp.int32, (tq, tq), 1) // CHUNK
    diag_mask = c <= r

    for hh in range(ATTN_HEADS_PER_STEP):
        vext_ref[:, 2 * hh * dv:(2 * hh + 1) * dv] = v_ref[0, :, hh * dv:(hh + 1) * dv]
        vext_ref[:, (2 * hh + 1) * dv:(2 * hh + 2) * dv] = jnp.ones((seq, dv), vext_ref.dtype)

    def scores(job):
        hh, i = job
        lo = i * tq
        cols = slice(hh * HEAD_PAD, (hh + 1) * HEAD_PAD)
        q = q_ref[0, lo:lo + tq, cols]
        s_d = jnp.where(diag_mask, _dot_nt(q, k_ref[0, lo:lo + tq, cols]), -jnp.inf)
        s_o = _dot_nt(q, k_ref[0, :lo, cols]) if i > 0 else None
        return s_d, s_o

    jobs = [(hh, i) for i in range(seq // tq)[::-1] for hh in range(ATTN_HEADS_PER_STEP)]
    nxt = scores(jobs[0])
    for pos, (hh, i) in enumerate(jobs):
        lo = i * tq
        s_d, s_o = nxt
        if pos + 1 < len(jobs):
            nxt = scores(jobs[pos + 1])
        m = jnp.max(s_d, axis=-1, keepdims=True)
        if i > 0:
            m = jnp.maximum(m, jnp.max(s_o, axis=-1, keepdims=True))
        vcols = slice(2 * hh * dv, (2 * hh + 2) * dv)
        pv = _dot(jnp.exp2(s_d - m).astype(BF16), vext_ref[lo:lo + tq, vcols])
        if i > 0:
            pv = pv + _dot(jnp.exp2(s_o - m).astype(BF16), vext_ref[:lo, vcols])
        o_ref[0, lo:lo + tq, hh * dv:(hh + 1) * dv] = (
            pv[:, :dv] * (1.0 / pv[:, dv:dv + 1])).astype(o_ref.dtype)


def _mla_attn(q, k, v):
    b, s, _ = q.shape
    hps = ATTN_HEADS_PER_STEP
    qk_spec = pl.BlockSpec((1, s, hps * HEAD_PAD), lambda i, j: (i, 0, j))
    v_spec = pl.BlockSpec((1, s, hps * MLA_V), lambda i, j: (i, 0, j))
    return pl.pallas_call(
        _mla_attn_body,
        grid=(b, MLA_HEADS // hps),
        in_specs=[qk_spec, qk_spec, v_spec],
        out_specs=v_spec,
        out_shape=jax.ShapeDtypeStruct((b, s, MLA_HEADS * MLA_V), BF16),
        scratch_shapes=[pltpu.VMEM((s, hps * 2 * MLA_V), BF16)],
        compiler_params=pltpu.CompilerParams(
            dimension_semantics=("parallel", "parallel"), vmem_limit_bytes=V7X_VMEM_LIMIT),
        name="mla_attn",
    )(q, k, v)


def _gla_body(q_ref, k_ref, v_ref, la_ref, zr_ref, g_ref, o_ref, u_ref, st_ref, kd_ref, oraw_ref):
    seq, k_dim = k_ref.shape[1], k_ref.shape[2]
    v_dim = v_ref.shape[2]
    n_chunks = seq // CHUNK
    grp = GLA_GROUP
    r = lax.broadcasted_iota(jnp.int32, (grp, grp), 0)
    c = lax.broadcasted_iota(jnp.int32, (grp, grp), 1)
    later = ((c > r) & (c // CHUNK == r // CHUNK)).astype(BF16)

    b_end = jnp.sum(la_ref[0].reshape(n_chunks, CHUNK, k_dim), axis=1)
    dec = jnp.concatenate([jnp.exp2(b_end), jnp.zeros((LANES - n_chunks, k_dim), F32)], axis=0)
    dec_t = dec.T

    for n in range(seq // grp):
        sl = slice(n * grp, (n + 1) * grp)
        g = la_ref[0, sl, :]
        g_hi = g.astype(BF16)
        g_lo = (g - g_hi.astype(F32)).astype(BF16)
        rest2 = _dot(later, jnp.concatenate([g_hi, g_lo], axis=1))
        rest = rest2[:, :k_dim] + rest2[:, k_dim:]
        kd_ref[sl, :] = (k_ref[0, sl, :].astype(F32) * jnp.exp2(rest)).astype(BF16)

    for n in range(n_chunks):
        sl = slice(n * CHUNK, (n + 1) * CHUNK)
        u_ref[n] = lax.dot_general(kd_ref[sl, :], v_ref[0, sl, :], (((0,), (0,)), ((), ())),
                                   preferred_element_type=F32)

    state = jnp.zeros((k_dim, v_dim), F32)
    for n in range(n_chunks):
        state = state * dec_t[:, n:n + 1] + u_ref[n]
        st_ref[n] = state.astype(BF16)

    for n in range(n_chunks):
        sl = slice(n * CHUNK, (n + 1) * CHUNK)
        oraw_ref[sl, :] = _dot(q_ref[0, sl, :], st_ref[n])

    for n in range(seq // grp):
        sl = slice(n * grp, (n + 1) * grp)
        zr = zr_ref[0, sl, :].astype(F32)
        o_ref[0, sl, :] = (_rms(oraw_ref[sl, :], g_ref[...])
                           * (zr * jax.nn.sigmoid(zr))).astype(o_ref.dtype)


def _gla(q, k, v, la, zr, g):
    b, s, _ = q.shape
    n_chunks = s // CHUNK
    kspec = pl.BlockSpec((1, s, GLA_DK), lambda i, j: (i, 0, j))
    vspec = pl.BlockSpec((1, s, GLA_DV), lambda i, j: (i, 0, j))
    return pl.pallas_call(
        _gla_body,
        grid=(b, GLA_HEADS),
        in_specs=[kspec, kspec, vspec, kspec, vspec,
                  pl.BlockSpec((1, GLA_DV), lambda i, j: (0, 0))],
        out_specs=vspec,
        out_shape=jax.ShapeDtypeStruct((b, s, GLA_HEADS * GLA_DV), BF16),
        scratch_shapes=[pltpu.VMEM((n_chunks, GLA_DK, GLA_DV), F32),
                        pltpu.VMEM((n_chunks, GLA_DK, GLA_DV), BF16),
                        pltpu.VMEM((s, GLA_DK), BF16),
                        pltpu.VMEM((s, GLA_DV), F32)],
        compiler_params=pltpu.CompilerParams(
            dimension_semantics=("parallel", "parallel"), vmem_limit_bytes=V7X_VMEM_LIMIT),
        name="gla",
    )(q, k, v, la, zr, g)


def _mem_kv_body(m_ref, gn_ref, wk_ref, wv_ref, gk_ref, k_ref, v_ref):
    hm = _rms(m_ref[0], gn_ref[...]).astype(BF16)
    kf = _dot(hm, wk_ref[...])
    for hd in range(MEM_HEADS):
        c = hd * MEM_HEAD_DIM
        k_ref[0, :, c:c + MEM_HEAD_DIM] = _rms(kf[:, c:c + MEM_HEAD_DIM], gk_ref[...]).astype(BF16)
    v_ref[0] = _dot(hm, wv_ref[...]).astype(BF16)


def _mem_kv(mem, gn, wk, wv, gk):
    b, n, d = mem.shape
    w = wk.shape[1]
    ospec = pl.BlockSpec((1, n, w), lambda i: (i, 0, 0))
    return pl.pallas_call(
        _mem_kv_body,
        grid=(b,),
        in_specs=[pl.BlockSpec((1, n, d), lambda i: (i, 0, 0)),
                  _resident(gn.shape), _resident(wk.shape), _resident(wv.shape),
                  _resident(gk.shape)],
        out_specs=[ospec, ospec],
        out_shape=[jax.ShapeDtypeStruct((b, n, w), BF16)] * 2,
        compiler_params=pltpu.CompilerParams(
            dimension_semantics=("parallel",), vmem_limit_bytes=V7X_VMEM_LIMIT),
        name="mem_kv",
    )(mem, gn, wk, wv, gk)


def _out_mem_body(x_ref, om_ref, og_ref, wo_ref, gn_ref, wq_ref, gq_ref, km_ref, vm_ref,
                  wmo_ref, o_ref):
    n_mla = om_ref.shape[2]
    x = x_ref[0] + _dot(om_ref[0], wo_ref[:n_mla, :]) + _dot(og_ref[0], wo_ref[n_mla:, :])
    h = _rms(x, gn_ref[...]).astype(BF16)
    qm = _dot(h, wq_ref[...])
    outs = []
    for hd in range(MEM_HEADS):
        c = hd * MEM_HEAD_DIM
        qn = (_rms(qm[:, c:c + MEM_HEAD_DIM], gq_ref[...]) * (MEM_HEAD_DIM ** -0.5)).astype(BF16)
        s = _dot_nt(qn, km_ref[0, :, c:c + MEM_HEAD_DIM])
        p = jnp.exp(s - jnp.max(s, axis=-1, keepdims=True))
        l = jnp.sum(p, axis=-1, keepdims=True)
        oh = _dot(p.astype(BF16), vm_ref[0, :, c:c + MEM_HEAD_DIM]) * (1.0 / l)
        outs.append(oh.astype(BF16))
    o_ref[0] = x + _dot(jnp.concatenate(outs, axis=1), wmo_ref[...])


def _out_mem(x, om, og, wo, gn, wq, gq, km, vm, wmo):
    b, s, d = x.shape
    tm = OUT_TM
    n_mem, wm = km.shape[1], km.shape[2]
    tok = lambda w: pl.BlockSpec((1, tm, w), lambda i, j: (i, j, 0))
    memspec = pl.BlockSpec((1, n_mem, wm), lambda i, j: (i, 0, 0))
    return pl.pallas_call(
        _out_mem_body,
        grid=(b, s // tm),
        in_specs=[tok(d), tok(om.shape[2]), tok(og.shape[2]),
                  _resident(wo.shape), _resident(gn.shape), _resident(wq.shape),
                  _resident(gq.shape), memspec, memspec, _resident(wmo.shape)],
        out_specs=tok(d),
        out_shape=jax.ShapeDtypeStruct((b, s, d), F32),
        compiler_params=pltpu.CompilerParams(
            dimension_semantics=("parallel", "parallel"), vmem_limit_bytes=V7X_VMEM_LIMIT),
        name="out_mem",
    )(x, om, og, wo, gn, wq, gq, km, vm, wmo)


def _pad_rope(w):
    z = jnp.zeros(w.shape[:-1] + (LANES // 2 - HALF_ROPE,), w.dtype)
    return jnp.concatenate([w[..., :HALF_ROPE], z, w[..., HALF_ROPE:], z], axis=-1)


def _pad_head(w):
    return jnp.concatenate([w[..., :MLA_NOPE], _pad_rope(w[..., MLA_NOPE:])], axis=-1)


def _w_in_columns():
    o = 0
    parts = {}
    for name, size in (("zq", MLA_Q_RANK), ("zkv", MLA_KV_RANK), ("zkr", MLA_ROPE),
                       ("gq", GLA_HEADS * GLA_DK), ("gk", GLA_HEADS * GLA_DK),
                       ("gv", GLA_HEADS * GLA_DV), ("zg", GLA_GATE_RANK),
                       ("zr", GLA_HEADS * GLA_DV)):
        parts[name] = (o, o + size)
        o += size
    return parts


def _split_w_in_body(wt_ref, lat_ref, krg_ref, gla_ref, zr_ref):
    parts = _w_in_columns()
    rows = lambda a, b: wt_ref[parts[a][0]:parts[b][1], :]
    kr, zg = rows("zkr", "zkr"), rows("zg", "zg")
    zeros = lambda n: jnp.zeros((n, wt_ref.shape[1]), F32)
    krg_t = jnp.concatenate(
        [kr[:HALF_ROPE], zg, zeros(LANES // 2 - HALF_ROPE - GLA_GATE_RANK),
         kr[HALF_ROPE:], zeros(LANES // 2 - HALF_ROPE)], axis=0)
    lat_ref[...] = rows("zq", "zkv").T.astype(BF16)
    krg_ref[...] = krg_t.T.astype(BF16)
    gla_ref[...] = rows("gq", "gv").T.astype(BF16)
    zr_ref[...] = rows("zr", "zr").T.astype(BF16)


def _split_w_in(w_in, layer):
    wt = jnp.swapaxes(w_in, 1, 2)
    _, n, k = wt.shape
    parts = _w_in_columns()
    widths = [parts["zkv"][1] - parts["zq"][0], LANES,
              parts["gv"][1] - parts["gq"][0], parts["zr"][1] - parts["zr"][0]]
    tr = W_IN_ROWS
    return pl.pallas_call(
        _split_w_in_body,
        grid=(k // tr,),
        in_specs=[pl.BlockSpec((None, n, tr), lambda i: (layer, 0, i))],
        out_specs=[pl.BlockSpec((tr, w), lambda i: (i, 0)) for w in widths],
        out_shape=[jax.ShapeDtypeStruct((k, w), BF16) for w in widths],
        compiler_params=pltpu.CompilerParams(
            dimension_semantics=("parallel",), vmem_limit_bytes=V7X_VMEM_LIMIT),
        name="split_w_in",
    )(wt)


def kernel(x, mem, positions, ffn1_norm, ffn1_w_gate, ffn1_w_up, ffn1_w_down, mix_norm, w_in, q_a_norm, w_q_up, kv_a_norm, w_kv_up, mla_q_norm, mla_k_norm, gla_w_gate2, gla_b_gate, gla_out_norm, w_out, mem_attn_norm, mem_norm, mem_w_q, mem_w_k, mem_w_v, mem_w_o, mem_q_norm, mem_k_norm, ffn2_norm, ffn2_w_gate, ffn2_w_up, ffn2_w_down):
    b, s, d = x.shape
    m = b * s
    depth = w_in.shape[0]
    row = lambda v: v.reshape(1, -1)
    bf = lambda w: w.astype(BF16)

    half = jnp.arange(HALF_ROPE, dtype=F32)
    inv_freq = ROPE_THETA ** (-half / HALF_ROPE)
    invf = _pad_rope(jnp.concatenate([inv_freq, inv_freq])).reshape(1, LANES)
    pos = positions.reshape(m, 1)

    xf = x.reshape(m, d)
    for l in range(depth):
        xf = _ffn(xf, row(ffn1_norm[l]), ffn1_w_gate[l], ffn1_w_up[l], ffn1_w_down[l])

        wq = _pad_head(w_q_up[l].reshape(MLA_Q_RANK, MLA_HEADS, MLA_QK)).reshape(MLA_Q_RANK, -1)
        wkv = w_kv_up[l].reshape(MLA_KV_RANK, MLA_HEADS, MLA_NOPE + MLA_V)
        wkv = jnp.concatenate([wkv[..., :MLA_NOPE].reshape(MLA_KV_RANK, -1),
                               wkv[..., MLA_NOPE:].reshape(MLA_KV_RANK, -1)], axis=1)
        w2 = jnp.pad(gla_w_gate2[l], ((HALF_ROPE, LANES - HALF_ROPE - GLA_GATE_RANK), (0, 0)))
        q, k, v, gq, gk, gv, la, zr = _mixer_proj(
            xf, pos, row(mix_norm[l]), *_split_w_in(w_in, l), row(q_a_norm[l]), bf(wq),
            row(kv_a_norm[l]), bf(wkv), row(_pad_head(mla_q_norm[l])), row(_pad_head(mla_k_norm[l])),
            invf, bf(w2), row(gla_b_gate[l]))

        r3 = lambda a: a.reshape(b, s, a.shape[-1])
        o_mla = _mla_attn(r3(q), r3(k), r3(v))
        o_gla = _gla(r3(gq), r3(gk), r3(gv), r3(la), r3(zr), row(gla_out_norm[l]))

        km, vm = _mem_kv(mem, row(mem_norm[l]), bf(mem_w_k[l]), bf(mem_w_v[l]), row(mem_k_norm[l]))
        x3 = _out_mem(r3(xf), o_mla, o_gla, bf(w_out[l]), row(mem_attn_norm[l]), bf(mem_w_q[l]),
                      row(mem_q_norm[l]), km, vm, bf(mem_w_o[l]))

        xf = _ffn(x3.reshape(m, d), row(ffn2_norm[l]), ffn2_w_gate[l], ffn2_w_up[l], ffn2_w_down[l])
    return xf.reshape(b, s, d)
```

```python
import jax
import jax.numpy as jnp
from jax import lax
from jax.experimental import pallas as pl
from jax.experimental.pallas import tpu as pltpu

F32 = jnp.float32
BF16 = jnp.bfloat16

EPS = 1e-6
CHUNK = 64
ROPE_THETA = 10000.0
GLA_TAU = 16.0
LOG2_E = 1.4426950408889634

MLA_HEADS = 8
MLA_NOPE = 128
MLA_ROPE = 64
MLA_QK = MLA_NOPE + MLA_ROPE
MLA_V = 128
MLA_Q_RANK = 512
MLA_KV_RANK = 256
GLA_HEADS = 4
GLA_DK = 128
GLA_DV = 256
GLA_GATE_RANK = 16
MEM_HEADS = 4
MEM_HEAD_DIM = 128

LANES = 128
HEAD_PAD = 2 * LANES
HALF_ROPE = MLA_ROPE // 2
MLA_HEAD_W = 2 * HEAD_PAD + MLA_V
GLA_HEAD_W = 2 * GLA_DK + 2 * GLA_DV

V7X_VMEM_LIMIT = 56 * 1024 * 1024

FFN_TM = 1024
FFN_TF = 512
FFN_DOWN_SPLIT = ((0, 512), (512, 1024), (1024, 1536), (1536, 1792), (1792, 2048))
FFN_NORM_CHUNKS = 8
PROJ_TM = 512
ATTN_TQ = 256
GLA_GROUP = 256
OUT_TM = 512
W_IN_ROWS = 256


def _rms(x, g):
    ms = jnp.mean(x * x, axis=-1, keepdims=True)
    return x * lax.rsqrt(ms + EPS) * g


def _dot(a, b):
    return jnp.dot(a, b, preferred_element_type=F32)


def _dot_nt(a, b):
    return lax.dot_general(a, b, (((1,), (1,)), ((), ())), preferred_element_type=F32)


def _resident(shape):
    nd = len(shape)
    return pl.BlockSpec(shape, lambda *_: (0,) * nd, pipeline_mode=pl.Buffered(1))


def _ffn_body(x_hbm, g_ref, wg_ref, wu_ref, wd_ref, o_hbm, xbuf, acc, h_ref, x_sem, o_sem):
    i, f = pl.program_id(0), pl.program_id(1)
    n_i, n_f = pl.num_programs(0), pl.num_programs(1)
    tm, d = acc.shape
    cur = i % 2
    norm_rows = tm // FFN_NORM_CHUNKS

    def rows(tile):
        return pl.ds(pl.multiple_of(tile * tm, tm), tm)

    def x_copy(tile, chunk):
        r0 = pl.multiple_of(chunk * norm_rows, norm_rows)
        return pltpu.make_async_copy(
            x_hbm.at[pl.ds(pl.multiple_of(tile * tm, tm) + r0, norm_rows), :],
            xbuf.at[pl.ds(r0, norm_rows), :], x_sem.at[chunk])

    def o_copy(tile):
        return pltpu.make_async_copy(acc, o_hbm.at[rows(tile), :], o_sem)

    @pl.when((i == 0) & (f == 0))
    def _():
        for c in range(FFN_NORM_CHUNKS):
            x_copy(0, c).start()
        for c in range(FFN_NORM_CHUNKS):
            x_copy(0, c).wait()
        h_ref[0] = _rms(xbuf[...], g_ref[...]).astype(BF16)

    @pl.when((f >= 1) & (f <= FFN_NORM_CHUNKS) & (i + 1 < n_i))
    def _():
        x_copy(i + 1, f - 1).wait()

    def body(first):
        h = h_ref[cur]
        chunk = jnp.clip(f - 1, 0, FFN_NORM_CHUNKS - 1)
        rs = pl.ds(pl.multiple_of(chunk * norm_rows, norm_rows), norm_rows)
        h_ref[1 - cur, rs, :] = _rms(xbuf[rs, :], g_ref[...]).astype(BF16)

        gate = _dot(h, wg_ref[...].astype(BF16))
        up = _dot(h, wu_ref[...].astype(BF16))
        a = ((0.5 * gate) * jax.nn.sigmoid(gate) * up).astype(BF16)
        if first:
            @pl.when(i > 0)
            def _():
                o_copy(i - 1).wait()

        for c0, c1 in FFN_DOWN_SPLIT:
            part = _dot(a, wd_ref[:, c0:c1].astype(BF16))
            if first:
                acc[:, c0:c1] = xbuf[:, c0:c1] + part
            else:
                acc[:, c0:c1] += part

    @pl.when(f == 0)
    def _():
        body(True)

        @pl.when(i + 1 < n_i)
        def _():
            for c in range(FFN_NORM_CHUNKS):
                x_copy(i + 1, c).start()

    @pl.when(f > 0)
    def _():
        body(False)

    @pl.when(f == n_f - 1)
    def _():
        o_copy(i).start()

        @pl.when(i == n_i - 1)
        def _():
            o_copy(i).wait()


def _ffn(x, g, wg, wu, wd):
    m, d = x.shape
    f = wg.shape[1]
    assert f // FFN_TF > FFN_NORM_CHUNKS and FFN_TM % (16 * FFN_NORM_CHUNKS) == 0
    assert FFN_DOWN_SPLIT[-1][1] == d
    return pl.pallas_call(
        _ffn_body,
        grid=(m // FFN_TM, f // FFN_TF),
        in_specs=[
            pl.BlockSpec(memory_space=pl.ANY),
            pl.BlockSpec((1, d), lambda i, j: (0, 0)),
            pl.BlockSpec((d, FFN_TF), lambda i, j: (0, j)),
            pl.BlockSpec((d, FFN_TF), lambda i, j: (0, j)),
            pl.BlockSpec((FFN_TF, d), lambda i, j: (j, 0)),
        ],
        out_specs=pl.BlockSpec(memory_space=pl.ANY),
        out_shape=jax.ShapeDtypeStruct((m, d), F32),
        scratch_shapes=[pltpu.VMEM((FFN_TM, d), F32),
                        pltpu.VMEM((FFN_TM, d), F32),
                        pltpu.VMEM((2, FFN_TM, d), BF16),
                        pltpu.SemaphoreType.DMA((FFN_NORM_CHUNKS,)),
                        pltpu.SemaphoreType.DMA(())],
        compiler_params=pltpu.CompilerParams(
            dimension_semantics=("arbitrary", "arbitrary"),
            vmem_limit_bytes=V7X_VMEM_LIMIT),
        name="ffn",
    )(x, g, wg, wu, wd)


def _mixer_proj_body(x_ref, pos_ref, gmix_ref, w_lat_ref, w_krg_ref, w_gla_ref,
                     gqa_ref, wq_ref, gkva_ref, wkv_ref, gq_ref, gk_ref, invf_ref, w2_ref, b2_ref,
                     mla_out, gla_out, la_out):
    h = _rms(x_ref[...], gmix_ref[...]).astype(BF16)
    col = lax.broadcasted_iota(jnp.int32, (1, GLA_HEAD_W), 1)
    gla_scale = jnp.where(col < GLA_DK, GLA_DK ** -0.5, 1.0)

    def gla_head(hd):
        cs = slice(hd * GLA_HEAD_W, (hd + 1) * GLA_HEAD_W)
        gla_out[:, cs] = (_dot(h, w_gla_ref[:, cs]) * gla_scale).astype(BF16)

    zq = _dot(h, w_lat_ref[:, :MLA_Q_RANK])
    zkv = _dot(h, w_lat_ref[:, MLA_Q_RANK:])
    zkrg = _dot(h, w_krg_ref[...])
    lane = lax.broadcasted_iota(jnp.int32, (1, LANES), 1)
    zkr = jnp.where(lane % (LANES // 2) < HALF_ROPE, zkrg, 0.0)
    gla_head(0)
    gla_head(1)

    pre = _dot(zkrg.astype(BF16), w2_ref[...]) + b2_ref[...]
    log_sig = jnp.minimum(pre, 0.0) - jnp.log1p(jnp.exp(-jnp.abs(pre)))
    la_out[...] = log_sig * (LOG2_E / GLA_TAU)

    ang = pos_ref[...].astype(F32) * invf_ref[...]
    cos_t = jnp.cos(ang)
    sin_t = jnp.sin(ang) * jnp.where(lane < LANES // 2, -1.0, 1.0)

    def rope(v):
        return v * cos_t + pltpu.roll(v, LANES // 2, axis=1) * sin_t

    inv_qk = 1.0 / MLA_QK

    qa = _rms(zq, gqa_ref[...]).astype(BF16)
    kva = _rms(zkv, gkva_ref[...]).astype(BF16)
    qf = _dot(qa, wq_ref[...])
    kvf = _dot(kva, wkv_ref[...])
    gla_head(2)
    for hd in range(MLA_HEADS):
        c = hd * HEAD_PAD
        o = hd * MLA_HEAD_W
        blk = qf[:, c:c + HEAD_PAD]
        ss = jnp.sum(blk * blk, axis=-1, keepdims=True)
        rinv = lax.rsqrt(ss * inv_qk + EPS) * (MLA_QK ** -0.5 * LOG2_E)
        mla_out[:, o:o + LANES] = (blk[:, :LANES] * rinv * gq_ref[:, :LANES]).astype(BF16)
        mla_out[:, o + LANES:o + HEAD_PAD] = rope(
            blk[:, LANES:] * rinv * gq_ref[:, LANES:]).astype(BF16)

    gla_head(3)

    ssr = jnp.sum(zkr * zkr, axis=-1, keepdims=True)
    kr_base = rope(zkr * gk_ref[:, LANES:])
    for hd in range(MLA_HEADS):
        kn = kvf[:, hd * MLA_NOPE:(hd + 1) * MLA_NOPE]
        ss = jnp.sum(kn * kn, axis=-1, keepdims=True) + ssr
        rinv = lax.rsqrt(ss * inv_qk + EPS)
        o = hd * MLA_HEAD_W + HEAD_PAD
        mla_out[:, o:o + LANES] = (kn * rinv * gk_ref[:, :LANES]).astype(BF16)
        mla_out[:, o + LANES:o + HEAD_PAD] = (kr_base * rinv).astype(BF16)
        v0 = (MLA_HEADS + hd) * MLA_NOPE
        mla_out[:, o + HEAD_PAD:o + HEAD_PAD + MLA_V] = kvf[:, v0:v0 + MLA_V].astype(BF16)


def _mixer_proj(x, pos, *weights):
    m, d = x.shape
    tm = PROJ_TM
    row = lambda i: (i, 0)
    widths = [
        (MLA_HEADS * MLA_HEAD_W, BF16),
        (GLA_HEADS * GLA_HEAD_W, BF16),
        (GLA_HEADS * GLA_DK, F32),
    ]
    return pl.pallas_call(
        _mixer_proj_body,
        grid=(m // tm,),
        in_specs=[pl.BlockSpec((tm, d), row), pl.BlockSpec((tm, 1), row)]
        + [_resident(w.shape) for w in weights],
        out_specs=[pl.BlockSpec((tm, w), row) for w, _ in widths],
        out_shape=[jax.ShapeDtypeStruct((m, w), dt) for w, dt in widths],
        compiler_params=pltpu.CompilerParams(
            dimension_semantics=("parallel",), vmem_limit_bytes=V7X_VMEM_LIMIT),
        name="mixer_proj",
    )(x, pos, *weights)


def _mla_attn_body(qkv_ref, o_ref, vext_ref):
    seq = qkv_ref.shape[1]
    dv = MLA_V
    tq = ATTN_TQ
    q_cols, k_cols = slice(0, HEAD_PAD), slice(HEAD_PAD, 2 * HEAD_PAD)
    r = lax.broadcasted_iota(jnp.int32, (tq, tq), 0) // CHUNK
    c = lax.broadcasted_iota(jnp.int32, (tq, tq), 1) // CHUNK
    diag_mask = c <= r

    vext_ref[:, :dv] = qkv_ref[0, :, 2 * HEAD_PAD:]
    vext_ref[:, dv:] = jnp.ones((seq, dv), vext_ref.dtype)

    def scores(i):
        lo = i * tq
        q = qkv_ref[0, lo:lo + tq, q_cols]
        s_d = jnp.where(diag_mask, _dot_nt(q, qkv_ref[0, lo:lo + tq, k_cols]), -jnp.inf)
        s_o = _dot_nt(q, qkv_ref[0, :lo, k_cols]) if i > 0 else None
        return s_d, s_o

    order = list(range(seq // tq))[::-1]
    nxt = scores(order[0])
    for pos, i in enumerate(order):
        lo = i * tq
        s_d, s_o = nxt
        if pos + 1 < len(order):
            nxt = scores(order[pos + 1])
        m = jnp.max(s_d, axis=-1, keepdims=True)
        if i > 0:
            m = jnp.maximum(m, jnp.max(s_o, axis=-1, keepdims=True))
        pv = _dot(jnp.exp2(s_d - m).astype(BF16), vext_ref[lo:lo + tq, :])
        if i > 0:
            pv = pv + _dot(jnp.exp2(s_o - m).astype(BF16), vext_ref[:lo, :])
        o_ref[0, lo:lo + tq, :] = (pv[:, :dv] * (1.0 / pv[:, dv:dv + 1])).astype(o_ref.dtype)


def _mla_attn(qkv):
    b, s, _ = qkv.shape
    return pl.pallas_call(
        _mla_attn_body,
        grid=(b, MLA_HEADS),
        in_specs=[pl.BlockSpec((1, s, MLA_HEAD_W), lambda i, j: (i, 0, j))],
        out_specs=pl.BlockSpec((1, s, MLA_V), lambda i, j: (i, 0, j)),
        out_shape=jax.ShapeDtypeStruct((b, s, MLA_HEADS * MLA_V), BF16),
        scratch_shapes=[pltpu.VMEM((s, 2 * MLA_V), BF16)],
        compiler_params=pltpu.CompilerParams(
            dimension_semantics=("parallel", "parallel"), vmem_limit_bytes=V7X_VMEM_LIMIT),
        name="mla_attn",
    )(qkv)


def _gla_body(in_ref, la_ref, g_ref, o_ref, u_ref, st_ref, kd_ref, oraw_ref):
    seq, k_dim, v_dim = in_ref.shape[1], GLA_DK, GLA_DV
    q_cols, k_cols = slice(0, k_dim), slice(k_dim, 2 * k_dim)
    v_cols = slice(2 * k_dim, 2 * k_dim + v_dim)
    zr_cols = slice(2 * k_dim + v_dim, 2 * k_dim + 2 * v_dim)
    n_chunks = seq // CHUNK
    grp = GLA_GROUP
    r = lax.broadcasted_iota(jnp.int32, (grp, grp), 0)
    c = lax.broadcasted_iota(jnp.int32, (grp, grp), 1)
    later = ((c > r) & (c // CHUNK == r // CHUNK)).astype(BF16)

    b_end = jnp.sum(la_ref[0].reshape(n_chunks, CHUNK, k_dim), axis=1)
    dec = jnp.concatenate([jnp.exp2(b_end), jnp.zeros((LANES - n_chunks, k_dim), F32)], axis=0)
    dec_t = dec.T

    for n in range(seq // grp):
        sl = slice(n * grp, (n + 1) * grp)
        g = la_ref[0, sl, :]
        g_hi = g.astype(BF16)
        g_lo = (g - g_hi.astype(F32)).astype(BF16)
        rest2 = _dot(later, jnp.concatenate([g_hi, g_lo], axis=1))
        rest = rest2[:, :k_dim] + rest2[:, k_dim:]
        kd_ref[sl, :] = (in_ref[0, sl, k_cols].astype(F32) * jnp.exp2(rest)).astype(BF16)

    for n in range(n_chunks):
        sl = slice(n * CHUNK, (n + 1) * CHUNK)
        u_ref[n] = lax.dot_general(kd_ref[sl, :], in_ref[0, sl, v_cols], (((0,), (0,)), ((), ())),
                                   preferred_element_type=F32)

    state = jnp.zeros((k_dim, v_dim), F32)
    for n in range(n_chunks):
        state = state * dec_t[:, n:n + 1] + u_ref[n]
        st_ref[n] = state.astype(BF16)

    for n in range(n_chunks):
        sl = slice(n * CHUNK, (n + 1) * CHUNK)
        oraw_ref[sl, :] = _dot(in_ref[0, sl, q_cols], st_ref[n])

    for n in range(seq // grp):
        sl = slice(n * grp, (n + 1) * grp)
        zr = in_ref[0, sl, zr_cols].astype(F32)
        o_ref[0, sl, :] = (_rms(oraw_ref[sl, :], g_ref[...])
                           * (zr * jax.nn.sigmoid(zr))).astype(o_ref.dtype)


def _gla(qkvz, la, g):
    b, s, _ = qkvz.shape
    n_chunks = s // CHUNK
    head = lambda w: pl.BlockSpec((1, s, w), lambda i, j: (i, 0, j))
    return pl.pallas_call(
        _gla_body,
        grid=(b, GLA_HEADS),
        in_specs=[head(GLA_HEAD_W), head(GLA_DK),
                  pl.BlockSpec((1, GLA_DV), lambda i, j: (0, 0))],
        out_specs=head(GLA_DV),
        out_shape=jax.ShapeDtypeStruct((b, s, GLA_HEADS * GLA_DV), BF16),
        scratch_shapes=[pltpu.VMEM((n_chunks, GLA_DK, GLA_DV), F32),
                        pltpu.VMEM((n_chunks, GLA_DK, GLA_DV), BF16),
                        pltpu.VMEM((s, GLA_DK), BF16),
                        pltpu.VMEM((s, GLA_DV), F32)],
        compiler_params=pltpu.CompilerParams(
            dimension_semantics=("parallel", "parallel"), vmem_limit_bytes=V7X_VMEM_LIMIT),
        name="gla",
    )(qkvz, la, g)


def _mem_kv_body(m_ref, gn_ref, wk_ref, wv_ref, gk_ref, k_ref, v_ref):
    hm = _rms(m_ref[0], gn_ref[...]).astype(BF16)
    kf = _dot(hm, wk_ref[...])
    for hd in range(MEM_HEADS):
        c = hd * MEM_HEAD_DIM
        k_ref[0, :, c:c + MEM_HEAD_DIM] = _rms(kf[:, c:c + MEM_HEAD_DIM], gk_ref[...]).astype(BF16)
    v_ref[0] = _dot(hm, wv_ref[...]).astype(BF16)


def _mem_kv(mem, gn, wk, wv, gk):
    b, n, d = mem.shape
    w = wk.shape[1]
    ospec = pl.BlockSpec((1, n, w), lambda i: (i, 0, 0))
    return pl.pallas_call(
        _mem_kv_body,
        grid=(b,),
        in_specs=[pl.BlockSpec((1, n, d), lambda i: (i, 0, 0)),
                  _resident(gn.shape), _resident(wk.shape), _resident(wv.shape),
                  _resident(gk.shape)],
        out_specs=[ospec, ospec],
        out_shape=[jax.ShapeDtypeStruct((b, n, w), BF16)] * 2,
        compiler_params=pltpu.CompilerParams(
            dimension_semantics=("parallel",), vmem_limit_bytes=V7X_VMEM_LIMIT),
        name="mem_kv",
    )(mem, gn, wk, wv, gk)


def _out_mem_body(x_ref, om_ref, og_ref, wo_ref, gn_ref, wq_ref, gq_ref, km_ref, vm_ref,
                  wmo_ref, o_ref):
    n_mla = om_ref.shape[2]
    x = x_ref[0] + _dot(om_ref[0], wo_ref[:n_mla, :]) + _dot(og_ref[0], wo_ref[n_mla:, :])
    h = _rms(x, gn_ref[...]).astype(BF16)
    qm = _dot(h, wq_ref[...])
    outs = []
    for hd in range(MEM_HEADS):
        c = hd * MEM_HEAD_DIM
        qn = (_rms(qm[:, c:c + MEM_HEAD_DIM], gq_ref[...]) * (MEM_HEAD_DIM ** -0.5)).astype(BF16)
        s = _dot_nt(qn, km_ref[0, :, c:c + MEM_HEAD_DIM])
        p = jnp.exp(s - jnp.max(s, axis=-1, keepdims=True))
        l = jnp.sum(p, axis=-1, keepdims=True)
        oh = _dot(p.astype(BF16), vm_ref[0, :, c:c + MEM_HEAD_DIM]) * (1.0 / l)
        outs.append(oh.astype(BF16))
    o_ref[0] = x + _dot(jnp.concatenate(outs, axis=1), wmo_ref[...])


def _out_mem(x, om, og, wo, gn, wq, gq, km, vm, wmo):
    b, s, d = x.shape
    tm = OUT_TM
    n_mem, wm = km.shape[1], km.shape[2]
    tok = lambda w: pl.BlockSpec((1, tm, w), lambda i, j: (i, j, 0))
    memspec = pl.BlockSpec((1, n_mem, wm), lambda i, j: (i, 0, 0))
    return pl.pallas_call(
        _out_mem_body,
        grid=(b, s // tm),
        in_specs=[tok(d), tok(om.shape[2]), tok(og.shape[2]),
                  _resident(wo.shape), _resident(gn.shape), _resident(wq.shape),
                  _resident(gq.shape), memspec, memspec, _resident(wmo.shape)],
        out_specs=tok(d),
        out_shape=jax.ShapeDtypeStruct((b, s, d), F32),
        compiler_params=pltpu.CompilerParams(
            dimension_semantics=("parallel", "parallel"), vmem_limit_bytes=V7X_VMEM_LIMIT),
        name="out_mem",
    )(x, om, og, wo, gn, wq, gq, km, vm, wmo)


def _pad_rope(w):
    z = jnp.zeros(w.shape[:-1] + (LANES // 2 - HALF_ROPE,), w.dtype)
    return jnp.concatenate([w[..., :HALF_ROPE], z, w[..., HALF_ROPE:], z], axis=-1)


def _pad_head(w):
    return jnp.concatenate([w[..., :MLA_NOPE], _pad_rope(w[..., MLA_NOPE:])], axis=-1)


def _w_in_columns():
    o = 0
    parts = {}
    for name, size in (("zq", MLA_Q_RANK), ("zkv", MLA_KV_RANK), ("zkr", MLA_ROPE),
                       ("gq", GLA_HEADS * GLA_DK), ("gk", GLA_HEADS * GLA_DK),
                       ("gv", GLA_HEADS * GLA_DV), ("zg", GLA_GATE_RANK),
                       ("zr", GLA_HEADS * GLA_DV)):
        parts[name] = (o, o + size)
        o += size
    return parts


def _split_w_in_body(wt_ref, lat_ref, krg_ref, gla_ref):
    parts = _w_in_columns()
    rows = lambda a, b: wt_ref[parts[a][0]:parts[b][1], :]
    kr, zg = rows("zkr", "zkr"), rows("zg", "zg")
    zeros = lambda n: jnp.zeros((n, wt_ref.shape[1]), F32)
    krg_t = jnp.concatenate(
        [kr[:HALF_ROPE], zg, zeros(LANES // 2 - HALF_ROPE - GLA_GATE_RANK),
         kr[HALF_ROPE:], zeros(LANES // 2 - HALF_ROPE)], axis=0)
    lat_ref[...] = rows("zq", "zkv").T.astype(BF16)
    krg_ref[...] = krg_t.T.astype(BF16)
    for hd in range(GLA_HEADS):
        o = hd * GLA_HEAD_W
        for name, width in (("gq", GLA_DK), ("gk", GLA_DK), ("gv", GLA_DV), ("zr", GLA_DV)):
            r0 = parts[name][0] + hd * width
            gla_ref[:, o:o + width] = wt_ref[r0:r0 + width, :].T.astype(BF16)
            o += width


def _split_w_in(w_in, layer):
    wt = jnp.swapaxes(w_in, 1, 2)
    _, n, k = wt.shape
    parts = _w_in_columns()
    widths = [parts["zkv"][1] - parts["zq"][0], LANES, GLA_HEADS * GLA_HEAD_W]
    tr = W_IN_ROWS
    return pl.pallas_call(
        _split_w_in_body,
        grid=(k // tr,),
        in_specs=[pl.BlockSpec((None, n, tr), lambda i: (layer, 0, i))],
        out_specs=[pl.BlockSpec((tr, w), lambda i: (i, 0)) for w in widths],
        out_shape=[jax.ShapeDtypeStruct((k, w), BF16) for w in widths],
        compiler_params=pltpu.CompilerParams(
            dimension_semantics=("parallel",), vmem_limit_bytes=V7X_VMEM_LIMIT),
        name="split_w_in",
    )(wt)


def kernel(x, mem, positions, ffn1_norm, ffn1_w_gate, ffn1_w_up, ffn1_w_down, mix_norm, w_in, q_a_norm, w_q_up, kv_a_norm, w_kv_up, mla_q_norm, mla_k_norm, gla_w_gate2, gla_b_gate, gla_out_norm, w_out, mem_attn_norm, mem_norm, mem_w_q, mem_w_k, mem_w_v, mem_w_o, mem_q_norm, mem_k_norm, ffn2_norm, ffn2_w_gate, ffn2_w_up, ffn2_w_down):
    b, s, d = x.shape
    m = b * s
    depth = w_in.shape[0]
    row = lambda v: v.reshape(1, -1)
    bf = lambda w: w.astype(BF16)

    half = jnp.arange(HALF_ROPE, dtype=F32)
    inv_freq = ROPE_THETA ** (-half / HALF_ROPE)
    invf = _pad_rope(jnp.concatenate([inv_freq, inv_freq])).reshape(1, LANES)
    pos = positions.reshape(m, 1)

    xf = x.reshape(m, d)
    for l in range(depth):
        xf = _ffn(xf, row(ffn1_norm[l]), ffn1_w_gate[l], ffn1_w_up[l], ffn1_w_down[l])

        wq = _pad_head(w_q_up[l].reshape(MLA_Q_RANK, MLA_HEADS, MLA_QK)).reshape(MLA_Q_RANK, -1)
        wkv = w_kv_up[l].reshape(MLA_KV_RANK, MLA_HEADS, MLA_NOPE + MLA_V)
        wkv = jnp.concatenate([wkv[..., :MLA_NOPE].reshape(MLA_KV_RANK, -1),
                               wkv[..., MLA_NOPE:].reshape(MLA_KV_RANK, -1)], axis=1)
        w2 = jnp.pad(gla_w_gate2[l], ((HALF_ROPE, LANES - HALF_ROPE - GLA_GATE_RANK), (0, 0)))
        qkv, qkvz, la = _mixer_proj(
            xf, pos, row(mix_norm[l]), *_split_w_in(w_in, l), row(q_a_norm[l]), bf(wq),
            row(kv_a_norm[l]), bf(wkv), row(_pad_head(mla_q_norm[l])), row(_pad_head(mla_k_norm[l])),
            invf, bf(w2), row(gla_b_gate[l]))

        r3 = lambda a: a.reshape(b, s, a.shape[-1])
        o_mla = _mla_attn(r3(qkv))
        o_gla = _gla(r3(qkvz), r3(la), row(gla_out_norm[l]))

        km, vm = _mem_kv(mem, row(mem_norm[l]), bf(mem_w_k[l]), bf(mem_w_v[l]), row(mem_k_norm[l]))
        x3 = _out_mem(r3(xf), o_mla, o_gla, bf(w_out[l]), row(mem_attn_norm[l]), bf(mem_w_q[l]),
                      row(mem_q_norm[l]), km, vm, bf(mem_w_o[l]))

        xf = _ffn(x3.reshape(m, d), row(ffn2_norm[l]), ffn2_w_gate[l], ffn2_w_up[l], ffn2_w_down[l])
    return xf.reshape(b, s, d)
```

```python
import jax
import jax.numpy as jnp
from jax import lax
from jax.experimental import pallas as pl
from jax.experimental.pallas import tpu as pltpu

F32 = jnp.float32
BF16 = jnp.bfloat16

EPS = 1e-6
CHUNK = 64
ROPE_THETA = 10000.0
GLA_TAU = 16.0
LOG2_E = 1.4426950408889634

MLA_HEADS = 8
MLA_NOPE = 128
MLA_ROPE = 64
MLA_QK = MLA_NOPE + MLA_ROPE
MLA_V = 128
MLA_Q_RANK = 512
MLA_KV_RANK = 256
GLA_HEADS = 4
GLA_DK = 128
GLA_DV = 256
GLA_GATE_RANK = 16
MEM_HEADS = 4
MEM_HEAD_DIM = 128

LANES = 128
HEAD_PAD = 2 * LANES
HALF_ROPE = MLA_ROPE // 2
MLA_HEAD_W = 2 * HEAD_PAD + MLA_V
GLA_HEAD_W = 2 * GLA_DK + 2 * GLA_DV

V7X_VMEM_LIMIT = 56 * 1024 * 1024

FFN_TM = 1024
FFN_TF = 512
FFN_DOWN_SPLIT = ((0, 512), (512, 1024), (1024, 1536), (1536, 1792), (1792, 2048))
FFN_NORM_CHUNKS = 8
PROJ_TM = 512
ATTN_TQ = 256
GLA_GROUP = 256
OUT_TM = 512
W_IN_ROWS = 256


def _rms(x, g):
    ms = jnp.mean(x * x, axis=-1, keepdims=True)
    return x * lax.rsqrt(ms + EPS) * g


def _dot(a, b):
    return jnp.dot(a, b, preferred_element_type=F32)


def _dot_nt(a, b):
    return lax.dot_general(a, b, (((1,), (1,)), ((), ())), preferred_element_type=F32)


def _resident(shape):
    nd = len(shape)
    return pl.BlockSpec(shape, lambda *_: (0,) * nd, pipeline_mode=pl.Buffered(1))


def _ffn_body(x_hbm, g_ref, wg_ref, wu_ref, wd_ref, o_hbm, xbuf, acc, h_ref, x_sem, o_sem):
    i, f = pl.program_id(0), pl.program_id(1)
    n_i, n_f = pl.num_programs(0), pl.num_programs(1)
    tm, d = acc.shape
    cur = i % 2
    norm_rows = tm // FFN_NORM_CHUNKS

    def rows(tile):
        return pl.ds(pl.multiple_of(tile * tm, tm), tm)

    def x_copy(tile, chunk):
        r0 = pl.multiple_of(chunk * norm_rows, norm_rows)
        return pltpu.make_async_copy(
            x_hbm.at[pl.ds(pl.multiple_of(tile * tm, tm) + r0, norm_rows), :],
            xbuf.at[pl.ds(r0, norm_rows), :], x_sem.at[chunk])

    def o_copy(tile):
        return pltpu.make_async_copy(acc, o_hbm.at[rows(tile), :], o_sem)

    @pl.when((i == 0) & (f == 0))
    def _():
        for c in range(FFN_NORM_CHUNKS):
            x_copy(0, c).start()
        for c in range(FFN_NORM_CHUNKS):
            x_copy(0, c).wait()
        h_ref[0] = _rms(xbuf[...], g_ref[...]).astype(BF16)

    @pl.when((f >= 1) & (f <= FFN_NORM_CHUNKS) & (i + 1 < n_i))
    def _():
        x_copy(i + 1, f - 1).wait()

    def body(first):
        h = h_ref[cur]
        chunk = jnp.clip(f - 1, 0, FFN_NORM_CHUNKS - 1)
        rs = pl.ds(pl.multiple_of(chunk * norm_rows, norm_rows), norm_rows)
        h_ref[1 - cur, rs, :] = _rms(xbuf[rs, :], g_ref[...]).astype(BF16)

        gate = _dot(h, wg_ref[...].astype(BF16))
        up = _dot(h, wu_ref[...].astype(BF16))
        a = ((0.5 * gate) * jax.nn.sigmoid(gate) * up).astype(BF16)
        if first:
            @pl.when(i > 0)
            def _():
                o_copy(i - 1).wait()

        for c0, c1 in FFN_DOWN_SPLIT:
            part = _dot(a, wd_ref[:, c0:c1].astype(BF16))
            if first:
                acc[:, c0:c1] = xbuf[:, c0:c1] + part
            else:
                acc[:, c0:c1] += part

    @pl.when(f == 0)
    def _():
        body(True)

        @pl.when(i + 1 < n_i)
        def _():
            for c in range(FFN_NORM_CHUNKS):
                x_copy(i + 1, c).start()

    @pl.when(f > 0)
    def _():
        body(False)

    @pl.when(f == n_f - 1)
    def _():
        o_copy(i).start()

        @pl.when(i == n_i - 1)
        def _():
            o_copy(i).wait()


def _ffn(x, g, wg, wu, wd):
    m, d = x.shape
    f = wg.shape[1]
    assert f // FFN_TF > FFN_NORM_CHUNKS and FFN_TM % (16 * FFN_NORM_CHUNKS) == 0
    assert FFN_DOWN_SPLIT[-1][1] == d
    return pl.pallas_call(
        _ffn_body,
        grid=(m // FFN_TM, f // FFN_TF),
        in_specs=[
            pl.BlockSpec(memory_space=pl.ANY),
            pl.BlockSpec((1, d), lambda i, j: (0, 0)),
            pl.BlockSpec((d, FFN_TF), lambda i, j: (0, j)),
            pl.BlockSpec((d, FFN_TF), lambda i, j: (0, j)),
            pl.BlockSpec((FFN_TF, d), lambda i, j: (j, 0)),
        ],
        out_specs=pl.BlockSpec(memory_space=pl.ANY),
        out_shape=jax.ShapeDtypeStruct((m, d), F32),
        scratch_shapes=[pltpu.VMEM((FFN_TM, d), F32),
                        pltpu.VMEM((FFN_TM, d), F32),
                        pltpu.VMEM((2, FFN_TM, d), BF16),
                        pltpu.SemaphoreType.DMA((FFN_NORM_CHUNKS,)),
                        pltpu.SemaphoreType.DMA(())],
        compiler_params=pltpu.CompilerParams(
            dimension_semantics=("arbitrary", "arbitrary"),
            vmem_limit_bytes=V7X_VMEM_LIMIT),
        name="ffn",
    )(x, g, wg, wu, wd)


def _mixer_proj_body(x_ref, pos_ref, gmix_ref, w_lat_ref, w_krg_ref, w_gla_ref,
                     gqa_ref, wq_ref, gkva_ref, wkv_ref, gq_ref, gk_ref, invf_ref, w2_ref, b2_ref,
                     mla_out, gla_out, la_out):
    h = _rms(x_ref[...], gmix_ref[...]).astype(BF16)
    col = lax.broadcasted_iota(jnp.int32, (1, GLA_HEAD_W), 1)
    gla_scale = jnp.where(col < GLA_DK, GLA_DK ** -0.5, 1.0)

    def gla_head(hd):
        cs = slice(hd * GLA_HEAD_W, (hd + 1) * GLA_HEAD_W)
        gla_out[:, cs] = (_dot(h, w_gla_ref[:, cs]) * gla_scale).astype(BF16)

    zq = _dot(h, w_lat_ref[:, :MLA_Q_RANK])
    zkv = _dot(h, w_lat_ref[:, MLA_Q_RANK:])
    zkrg = _dot(h, w_krg_ref[...])
    lane = lax.broadcasted_iota(jnp.int32, (1, LANES), 1)
    zkr = jnp.where(lane % (LANES // 2) < HALF_ROPE, zkrg, 0.0)
    gla_head(0)
    gla_head(1)

    pre = _dot(zkrg.astype(BF16), w2_ref[...]) + b2_ref[...]
    log_sig = jnp.minimum(pre, 0.0) - jnp.log(1.0 + jnp.exp(-jnp.abs(pre)))
    la_out[...] = log_sig * (LOG2_E / GLA_TAU)

    ang = pos_ref[...].astype(F32) * invf_ref[...]
    cos_t = jnp.cos(ang)
    sin_t = jnp.sin(ang) * jnp.where(lane < LANES // 2, -1.0, 1.0)

    def rope(v):
        return v * cos_t + pltpu.roll(v, LANES // 2, axis=1) * sin_t

    inv_qk = 1.0 / MLA_QK

    qa = _rms(zq, gqa_ref[...]).astype(BF16)
    kva = _rms(zkv, gkva_ref[...]).astype(BF16)
    qf = _dot(qa, wq_ref[...])
    kvf = _dot(kva, wkv_ref[...])
    gla_head(2)
    for hd in range(MLA_HEADS):
        c = hd * HEAD_PAD
        o = hd * MLA_HEAD_W
        blk = qf[:, c:c + HEAD_PAD]
        ss = jnp.sum(blk * blk, axis=-1, keepdims=True)
        rinv = lax.rsqrt(ss * inv_qk + EPS) * (MLA_QK ** -0.5 * LOG2_E)
        mla_out[:, o:o + LANES] = (blk[:, :LANES] * rinv * gq_ref[:, :LANES]).astype(BF16)
        mla_out[:, o + LANES:o + HEAD_PAD] = rope(
            blk[:, LANES:] * rinv * gq_ref[:, LANES:]).astype(BF16)

    gla_head(3)

    ssr = jnp.sum(zkr * zkr, axis=-1, keepdims=True)
    kr_base = rope(zkr * gk_ref[:, LANES:])
    for hd in range(MLA_HEADS):
        kn = kvf[:, hd * MLA_NOPE:(hd + 1) * MLA_NOPE]
        ss = jnp.sum(kn * kn, axis=-1, keepdims=True) + ssr
        rinv = lax.rsqrt(ss * inv_qk + EPS)
        o = hd * MLA_HEAD_W + HEAD_PAD
        mla_out[:, o:o + LANES] = (kn * rinv * gk_ref[:, :LANES]).astype(BF16)
        mla_out[:, o + LANES:o + HEAD_PAD] = (kr_base * rinv).astype(BF16)
        v0 = (MLA_HEADS + hd) * MLA_NOPE
        mla_out[:, o + HEAD_PAD:o + HEAD_PAD + MLA_V] = kvf[:, v0:v0 + MLA_V].astype(BF16)


def _mixer_proj(x, pos, *weights):
    m, d = x.shape
    tm = PROJ_TM
    row = lambda i: (i, 0)
    widths = [
        (MLA_HEADS * MLA_HEAD_W, BF16),
        (GLA_HEADS * GLA_HEAD_W, BF16),
        (GLA_HEADS * GLA_DK, F32),
    ]
    return pl.pallas_call(
        _mixer_proj_body,
        grid=(m // tm,),
        in_specs=[pl.BlockSpec((tm, d), row), pl.BlockSpec((tm, 1), row)]
        + [_resident(w.shape) for w in weights],
        out_specs=[pl.BlockSpec((tm, w), row) for w, _ in widths],
        out_shape=[jax.ShapeDtypeStruct((m, w), dt) for w, dt in widths],
        compiler_params=pltpu.CompilerParams(
            dimension_semantics=("parallel",), vmem_limit_bytes=V7X_VMEM_LIMIT),
        name="mixer_proj",
    )(x, pos, *weights)


def _mla_attn_body(qkv_ref, o_ref, vext_ref):
    seq = qkv_ref.shape[1]
    dv = MLA_V
    tq = ATTN_TQ
    q_cols, k_cols = slice(0, HEAD_PAD), slice(HEAD_PAD, 2 * HEAD_PAD)
    r = lax.broadcasted_iota(jnp.int32, (tq, tq), 0) // CHUNK
    c = lax.broadcasted_iota(jnp.int32, (tq, tq), 1) // CHUNK
    diag_mask = c <= r

    vext_ref[:, :dv] = qkv_ref[0, :, 2 * HEAD_PAD:]
    vext_ref[:, dv:] = jnp.ones((seq, dv), vext_ref.dtype)

    def scores(i):
        lo = i * tq
        q = qkv_ref[0, lo:lo + tq, q_cols]
        s_d = jnp.where(diag_mask, _dot_nt(q, qkv_ref[0, lo:lo + tq, k_cols]), -jnp.inf)
        s_o = _dot_nt(q, qkv_ref[0, :lo, k_cols]) if i > 0 else None
        m = jnp.max(s_d, axis=-1, keepdims=True)
        if i > 0:
            m = jnp.maximum(m, jnp.max(s_o, axis=-1, keepdims=True))
        return s_d, s_o, m

    order = list(range(seq // tq))[::-1]
    ahead = 2
    pending = [scores(i) for i in order[:ahead]]
    for pos, i in enumerate(order):
        lo = i * tq
        s_d, s_o, m = pending.pop(0)
        if pos + ahead < len(order):
            pending.append(scores(order[pos + ahead]))
        pv = _dot(jnp.exp2(s_d - m).astype(BF16), vext_ref[lo:lo + tq, :])
        if i > 0:
            pv = pv + _dot(jnp.exp2(s_o - m).astype(BF16), vext_ref[:lo, :])
        o_ref[0, lo:lo + tq, :] = (pv[:, :dv] * (1.0 / pv[:, dv:dv + 1])).astype(o_ref.dtype)


def _mla_attn(qkv):
    b, s, _ = qkv.shape
    return pl.pallas_call(
        _mla_attn_body,
        grid=(b, MLA_HEADS),
        in_specs=[pl.BlockSpec((1, s, MLA_HEAD_W), lambda i, j: (i, 0, j))],
        out_specs=pl.BlockSpec((1, s, MLA_V), lambda i, j: (i, 0, j)),
        out_shape=jax.ShapeDtypeStruct((b, s, MLA_HEADS * MLA_V), BF16),
        scratch_shapes=[pltpu.VMEM((s, 2 * MLA_V), BF16)],
        compiler_params=pltpu.CompilerParams(
            dimension_semantics=("parallel", "parallel"), vmem_limit_bytes=V7X_VMEM_LIMIT),
        name="mla_attn",
    )(qkv)


def _gla_body(in_ref, la_ref, g_ref, o_ref, u_ref, st_ref, kd_ref, oraw_ref):
    seq, k_dim, v_dim = in_ref.shape[1], GLA_DK, GLA_DV
    q_cols, k_cols = slice(0, k_dim), slice(k_dim, 2 * k_dim)
    v_cols = slice(2 * k_dim, 2 * k_dim + v_dim)
    zr_cols = slice(2 * k_dim + v_dim, 2 * k_dim + 2 * v_dim)
    n_chunks = seq // CHUNK
    grp = GLA_GROUP
    r = lax.broadcasted_iota(jnp.int32, (grp, grp), 0)
    c = lax.broadcasted_iota(jnp.int32, (grp, grp), 1)
    later = ((c > r) & (c // CHUNK == r // CHUNK)).astype(BF16)

    b_end = jnp.sum(la_ref[0].reshape(n_chunks, CHUNK, k_dim), axis=1)
    dec = jnp.concatenate([jnp.exp2(b_end), jnp.zeros((LANES - n_chunks, k_dim), F32)], axis=0)
    dec_t = dec.T

    for n in range(seq // grp):
        sl = slice(n * grp, (n + 1) * grp)
        g = la_ref[0, sl, :]
        g_hi = g.astype(BF16)
        g_lo = (g - g_hi.astype(F32)).astype(BF16)
        rest2 = _dot(later, jnp.concatenate([g_hi, g_lo], axis=1))
        rest = rest2[:, :k_dim] + rest2[:, k_dim:]
        kd_ref[sl, :] = (in_ref[0, sl, k_cols].astype(F32) * jnp.exp2(rest)).astype(BF16)

    for n in range(n_chunks):
        sl = slice(n * CHUNK, (n + 1) * CHUNK)
        u_ref[n] = lax.dot_general(kd_ref[sl, :], in_ref[0, sl, v_cols], (((0,), (0,)), ((), ())),
                                   preferred_element_type=F32)

    state = jnp.zeros((k_dim, v_dim), F32)
    for n in range(n_chunks):
        state = state * dec_t[:, n:n + 1] + u_ref[n]
        st_ref[n] = state.astype(BF16)

    for n in range(seq // grp):
        for c in range(n * grp // CHUNK, (n + 1) * grp // CHUNK):
            cs = slice(c * CHUNK, (c + 1) * CHUNK)
            oraw_ref[cs, :] = _dot(in_ref[0, cs, q_cols], st_ref[c])
        sl = slice(n * grp, (n + 1) * grp)
        zr = in_ref[0, sl, zr_cols].astype(F32)
        o_ref[0, sl, :] = (_rms(oraw_ref[sl, :], g_ref[...])
                           * (zr * jax.nn.sigmoid(zr))).astype(o_ref.dtype)


def _gla(qkvz, la, g):
    b, s, _ = qkvz.shape
    n_chunks = s // CHUNK
    head = lambda w: pl.BlockSpec((1, s, w), lambda i, j: (i, 0, j))
    return pl.pallas_call(
        _gla_body,
        grid=(b, GLA_HEADS),
        in_specs=[head(GLA_HEAD_W), head(GLA_DK),
                  pl.BlockSpec((1, GLA_DV), lambda i, j: (0, 0))],
        out_specs=head(GLA_DV),
        out_shape=jax.ShapeDtypeStruct((b, s, GLA_HEADS * GLA_DV), BF16),
        scratch_shapes=[pltpu.VMEM((n_chunks, GLA_DK, GLA_DV), F32),
                        pltpu.VMEM((n_chunks, GLA_DK, GLA_DV), BF16),
                        pltpu.VMEM((s, GLA_DK), BF16),
                        pltpu.VMEM((s, GLA_DV), F32)],
        compiler_params=pltpu.CompilerParams(
            dimension_semantics=("parallel", "parallel"), vmem_limit_bytes=V7X_VMEM_LIMIT),
        name="gla",
    )(qkvz, la, g)


def _mem_kv_body(m_ref, gn_ref, wk_ref, wv_ref, gk_ref, k_ref, v_ref):
    hm = _rms(m_ref[0], gn_ref[...]).astype(BF16)
    kf = _dot(hm, wk_ref[...])
    for hd in range(MEM_HEADS):
        c = hd * MEM_HEAD_DIM
        k_ref[0, :, c:c + MEM_HEAD_DIM] = _rms(kf[:, c:c + MEM_HEAD_DIM], gk_ref[...]).astype(BF16)
    v_ref[0] = _dot(hm, wv_ref[...]).astype(BF16)


def _mem_kv(mem, gn, wk, wv, gk):
    b, n, d = mem.shape
    w = wk.shape[1]
    ospec = pl.BlockSpec((1, n, w), lambda i: (i, 0, 0))
    return pl.pallas_call(
        _mem_kv_body,
        grid=(b,),
        in_specs=[pl.BlockSpec((1, n, d), lambda i: (i, 0, 0)),
                  _resident(gn.shape), _resident(wk.shape), _resident(wv.shape),
                  _resident(gk.shape)],
        out_specs=[ospec, ospec],
        out_shape=[jax.ShapeDtypeStruct((b, n, w), BF16)] * 2,
        compiler_params=pltpu.CompilerParams(
            dimension_semantics=("parallel",), vmem_limit_bytes=V7X_VMEM_LIMIT),
        name="mem_kv",
    )(mem, gn, wk, wv, gk)


def _out_mem_body(x_ref, om_ref, og_ref, wo_ref, gn_ref, wq_ref, gq_ref, km_ref, vm_ref,
                  wmo_ref, o_ref):
    n_mla = om_ref.shape[2]
    x = x_ref[0] + _dot(om_ref[0], wo_ref[:n_mla, :]) + _dot(og_ref[0], wo_ref[n_mla:, :])
    h = _rms(x, gn_ref[...]).astype(BF16)
    qm = _dot(h, wq_ref[...])
    heads = [slice(hd * MEM_HEAD_DIM, (hd + 1) * MEM_HEAD_DIM) for hd in range(MEM_HEADS)]
    scores = []
    for c in heads:
        qn = (_rms(qm[:, c], gq_ref[...]) * (MEM_HEAD_DIM ** -0.5)).astype(BF16)
        scores.append(_dot_nt(qn, km_ref[0, :, c]))
    outs = []
    for c, s in zip(heads, scores):
        p = jnp.exp(s - jnp.max(s, axis=-1, keepdims=True))
        l = jnp.sum(p, axis=-1, keepdims=True)
        oh = _dot(p.astype(BF16), vm_ref[0, :, c]) * (1.0 / l)
        outs.append(oh.astype(BF16))
    o_ref[0] = x + _dot(jnp.concatenate(outs, axis=1), wmo_ref[...])


def _out_mem(x, om, og, wo, gn, wq, gq, km, vm, wmo):
    b, s, d = x.shape
    tm = OUT_TM
    n_mem, wm = km.shape[1], km.shape[2]
    tok = lambda w: pl.BlockSpec((1, tm, w), lambda i, j: (i, j, 0))
    memspec = pl.BlockSpec((1, n_mem, wm), lambda i, j: (i, 0, 0))
    return pl.pallas_call(
        _out_mem_body,
        grid=(b, s // tm),
        in_specs=[tok(d), tok(om.shape[2]), tok(og.shape[2]),
                  _resident(wo.shape), _resident(gn.shape), _resident(wq.shape),
                  _resident(gq.shape), memspec, memspec, _resident(wmo.shape)],
        out_specs=tok(d),
        out_shape=jax.ShapeDtypeStruct((b, s, d), F32),
        compiler_params=pltpu.CompilerParams(
            dimension_semantics=("parallel", "parallel"), vmem_limit_bytes=V7X_VMEM_LIMIT),
        name="out_mem",
    )(x, om, og, wo, gn, wq, gq, km, vm, wmo)


def _pad_rope(w):
    z = jnp.zeros(w.shape[:-1] + (LANES // 2 - HALF_ROPE,), w.dtype)
    return jnp.concatenate([w[..., :HALF_ROPE], z, w[..., HALF_ROPE:], z], axis=-1)


def _pad_head(w):
    return jnp.concatenate([w[..., :MLA_NOPE], _pad_rope(w[..., MLA_NOPE:])], axis=-1)


def _w_in_columns():
    o = 0
    parts = {}
    for name, size in (("zq", MLA_Q_RANK), ("zkv", MLA_KV_RANK), ("zkr", MLA_ROPE),
                       ("gq", GLA_HEADS * GLA_DK), ("gk", GLA_HEADS * GLA_DK),
                       ("gv", GLA_HEADS * GLA_DV), ("zg", GLA_GATE_RANK),
                       ("zr", GLA_HEADS * GLA_DV)):
        parts[name] = (o, o + size)
        o += size
    return parts


def _split_w_in_body(wt_ref, lat_ref, krg_ref, gla_ref):
    parts = _w_in_columns()
    rows = lambda a, b: wt_ref[parts[a][0]:parts[b][1], :]
    kr, zg = rows("zkr", "zkr"), rows("zg", "zg")
    zeros = lambda n: jnp.zeros((n, wt_ref.shape[1]), F32)
    krg_t = jnp.concatenate(
        [kr[:HALF_ROPE], zg, zeros(LANES // 2 - HALF_ROPE - GLA_GATE_RANK),
         kr[HALF_ROPE:], zeros(LANES // 2 - HALF_ROPE)], axis=0)
    lat_ref[...] = rows("zq", "zkv").T.astype(BF16)
    krg_ref[...] = krg_t.T.astype(BF16)
    for hd in range(GLA_HEADS):
        o = hd * GLA_HEAD_W
        for name, width in (("gq", GLA_DK), ("gk", GLA_DK), ("gv", GLA_DV), ("zr", GLA_DV)):
            r0 = parts[name][0] + hd * width
            gla_ref[:, o:o + width] = wt_ref[r0:r0 + width, :].T.astype(BF16)
            o += width


def _split_w_in(w_in, layer):
    wt = jnp.swapaxes(w_in, 1, 2)
    _, n, k = wt.shape
    parts = _w_in_columns()
    widths = [parts["zkv"][1] - parts["zq"][0], LANES, GLA_HEADS * GLA_HEAD_W]
    tr = W_IN_ROWS
    return pl.pallas_call(
        _split_w_in_body,
        grid=(k // tr,),
        in_specs=[pl.BlockSpec((None, n, tr), lambda i: (layer, 0, i))],
        out_specs=[pl.BlockSpec((tr, w), lambda i: (i, 0)) for w in widths],
        out_shape=[jax.ShapeDtypeStruct((k, w), BF16) for w in widths],
        compiler_params=pltpu.CompilerParams(
            dimension_semantics=("parallel",), vmem_limit_bytes=V7X_VMEM_LIMIT),
        name="split_w_in",
    )(wt)


def kernel(x, mem, positions, ffn1_norm, ffn1_w_gate, ffn1_w_up, ffn1_w_down, mix_norm, w_in, q_a_norm, w_q_up, kv_a_norm, w_kv_up, mla_q_norm, mla_k_norm, gla_w_gate2, gla_b_gate, gla_out_norm, w_out, mem_attn_norm, mem_norm, mem_w_q, mem_w_k, mem_w_v, mem_w_o, mem_q_norm, mem_k_norm, ffn2_norm, ffn2_w_gate, ffn2_w_up, ffn2_w_down):
    b, s, d = x.shape
    m = b * s
    depth = w_in.shape[0]
    row = lambda v: v.reshape(1, -1)
    bf = lambda w: w.astype(BF16)

    half = jnp.arange(HALF_ROPE, dtype=F32)
    inv_freq = ROPE_THETA ** (-half / HALF_ROPE)
    invf = _pad_rope(jnp.concatenate([inv_freq, inv_freq])).reshape(1, LANES)
    pos = positions.reshape(m, 1)

    xf = x.reshape(m, d)
    for l in range(depth):
        xf = _ffn(xf, row(ffn1_norm[l]), ffn1_w_gate[l], ffn1_w_up[l], ffn1_w_down[l])

        wq = _pad_head(w_q_up[l].reshape(MLA_Q_RANK, MLA_HEADS, MLA_QK)).reshape(MLA_Q_RANK, -1)
        wkv = w_kv_up[l].reshape(MLA_KV_RANK, MLA_HEADS, MLA_NOPE + MLA_V)
        wkv = jnp.concatenate([wkv[..., :MLA_NOPE].reshape(MLA_KV_RANK, -1),
                               wkv[..., MLA_NOPE:].reshape(MLA_KV_RANK, -1)], axis=1)
        w2 = jnp.pad(gla_w_gate2[l], ((HALF_ROPE, LANES - HALF_ROPE - GLA_GATE_RANK), (0, 0)))
        qkv, qkvz, la = _mixer_proj(
            xf, pos, row(mix_norm[l]), *_split_w_in(w_in, l), row(q_a_norm[l]), bf(wq),
            row(kv_a_norm[l]), bf(wkv), row(_pad_head(mla_q_norm[l])), row(_pad_head(mla_k_norm[l])),
            invf, bf(w2), row(gla_b_gate[l]))

        r3 = lambda a: a.reshape(b, s, a.shape[-1])
        o_mla = _mla_attn(r3(qkv))
        o_gla = _gla(r3(qkvz), r3(la), row(gla_out_norm[l]))

        km, vm = _mem_kv(mem, row(mem_norm[l]), bf(mem_w_k[l]), bf(mem_w_v[l]), row(mem_k_norm[l]))
        x3 = _out_mem(r3(xf), o_mla, o_gla, bf(w_out[l]), row(mem_attn_norm[l]), bf(mem_w_q[l]),
                      row(mem_q_norm[l]), km, vm, bf(mem_w_o[l]))

        xf = _ffn(x3.reshape(m, d), row(ffn2_norm[l]), ffn2_w_gate[l], ffn2_w_up[l], ffn2_w_down[l])
    return xf.reshape(b, s, d)
```

```python
import jax
import jax.numpy as jnp
from jax import lax
from jax.experimental import pallas as pl
from jax.experimental.pallas import tpu as pltpu

F32 = jnp.float32
BF16 = jnp.bfloat16

EPS = 1e-6
CHUNK = 64
ROPE_THETA = 10000.0
GLA_TAU = 16.0
LOG2_E = 1.4426950408889634

MLA_HEADS = 8
MLA_NOPE = 128
MLA_ROPE = 64
MLA_QK = MLA_NOPE + MLA_ROPE
MLA_V = 128
MLA_Q_RANK = 512
MLA_KV_RANK = 256
GLA_HEADS = 4
GLA_DK = 128
GLA_DV = 256
GLA_GATE_RANK = 16
MEM_HEADS = 4
MEM_HEAD_DIM = 128

LANES = 128
HEAD_PAD = 2 * LANES
HALF_ROPE = MLA_ROPE // 2
MLA_HEAD_W = 2 * HEAD_PAD + MLA_V
GLA_HEAD_W = 2 * GLA_DK + 2 * GLA_DV

V7X_VMEM_LIMIT = 56 * 1024 * 1024

FFN_TM = 1024
FFN_TF = 512
FFN_DOWN_SPLIT = tuple((c, c + 512) for c in range(0, 2048, 512))
FFN_NORM_CHUNKS = 8
PROJ_TM = 512
ATTN_TQ = 256
GLA_GROUP = 256
OUT_TM = 512
W_IN_ROWS = 256


def _rms(x, g):
    ms = jnp.mean(x * x, axis=-1, keepdims=True)
    return x * lax.rsqrt(ms + EPS) * g


def _dot(a, b):
    return jnp.dot(a, b, preferred_element_type=F32)


def _dot_nt(a, b):
    return lax.dot_general(a, b, (((1,), (1,)), ((), ())), preferred_element_type=F32)


def _resident(shape):
    nd = len(shape)
    return pl.BlockSpec(shape, lambda *_: (0,) * nd, pipeline_mode=pl.Buffered(1))


def _ffn_body(x_hbm, g_ref, wg_ref, wu_ref, wd_ref, o_hbm, xbuf, acc, h_ref, x_sem, o_sem):
    i, f = pl.program_id(0), pl.program_id(1)
    n_i, n_f = pl.num_programs(0), pl.num_programs(1)
    tm, d = acc.shape
    cur = i % 2
    norm_rows = tm // FFN_NORM_CHUNKS

    def rows(tile):
        return pl.ds(pl.multiple_of(tile * tm, tm), tm)

    def x_copy(tile, chunk):
        r0 = pl.multiple_of(chunk * norm_rows, norm_rows)
        return pltpu.make_async_copy(
            x_hbm.at[pl.ds(pl.multiple_of(tile * tm, tm) + r0, norm_rows), :],
            xbuf.at[pl.ds(r0, norm_rows), :], x_sem.at[chunk])

    def o_copy(tile):
        return pltpu.make_async_copy(acc, o_hbm.at[rows(tile), :], o_sem)

    @pl.when((i == 0) & (f == 0))
    def _():
        for c in range(FFN_NORM_CHUNKS):
            x_copy(0, c).start()
        for c in range(FFN_NORM_CHUNKS):
            x_copy(0, c).wait()
        h_ref[0] = _rms(xbuf[...], g_ref[...]).astype(BF16)

    @pl.when((f >= 1) & (f <= FFN_NORM_CHUNKS) & (i + 1 < n_i))
    def _():
        x_copy(i + 1, f - 1).wait()

    def body(first):
        h = h_ref[cur]
        chunk = jnp.clip(f - 1, 0, FFN_NORM_CHUNKS - 1)
        rs = pl.ds(pl.multiple_of(chunk * norm_rows, norm_rows), norm_rows)
        h_ref[1 - cur, rs, :] = _rms(xbuf[rs, :], g_ref[...]).astype(BF16)

        half = FFN_TF // 2
        acts = []
        for k0 in (0, half):
            gate = _dot(h, wg_ref[:, k0:k0 + half].astype(BF16))
            up = _dot(h, wu_ref[:, k0:k0 + half].astype(BF16))
            acts.append(((0.5 * gate) * jax.nn.sigmoid(gate) * up).astype(BF16))
        if first:
            @pl.when(i > 0)
            def _():
                o_copy(i - 1).wait()

        for c0, c1 in FFN_DOWN_SPLIT:
            part = (_dot(acts[0], wd_ref[:half, c0:c1].astype(BF16))
                    + _dot(acts[1], wd_ref[half:, c0:c1].astype(BF16)))
            if first:
                acc[:, c0:c1] = xbuf[:, c0:c1] + part
            else:
                acc[:, c0:c1] += part

    @pl.when(f == 0)
    def _():
        body(True)

        @pl.when(i + 1 < n_i)
        def _():
            for c in range(FFN_NORM_CHUNKS):
                x_copy(i + 1, c).start()

    @pl.when(f > 0)
    def _():
        body(False)

    @pl.when(f == n_f - 1)
    def _():
        o_copy(i).start()

        @pl.when(i == n_i - 1)
        def _():
            o_copy(i).wait()


def _ffn(x, g, wg, wu, wd):
    m, d = x.shape
    f = wg.shape[1]
    assert f // FFN_TF > FFN_NORM_CHUNKS and FFN_TM % (16 * FFN_NORM_CHUNKS) == 0
    assert FFN_DOWN_SPLIT[-1][1] == d
    return pl.pallas_call(
        _ffn_body,
        grid=(m // FFN_TM, f // FFN_TF),
        in_specs=[
            pl.BlockSpec(memory_space=pl.ANY),
            pl.BlockSpec((1, d), lambda i, j: (0, 0)),
            pl.BlockSpec((d, FFN_TF), lambda i, j: (0, j)),
            pl.BlockSpec((d, FFN_TF), lambda i, j: (0, j)),
            pl.BlockSpec((FFN_TF, d), lambda i, j: (j, 0)),
        ],
        out_specs=pl.BlockSpec(memory_space=pl.ANY),
        out_shape=jax.ShapeDtypeStruct((m, d), F32),
        scratch_shapes=[pltpu.VMEM((FFN_TM, d), F32),
                        pltpu.VMEM((FFN_TM, d), F32),
                        pltpu.VMEM((2, FFN_TM, d), BF16),
                        pltpu.SemaphoreType.DMA((FFN_NORM_CHUNKS,)),
                        pltpu.SemaphoreType.DMA(())],
        compiler_params=pltpu.CompilerParams(
            dimension_semantics=("arbitrary", "arbitrary"),
            vmem_limit_bytes=V7X_VMEM_LIMIT),
        name="ffn",
    )(x, g, wg, wu, wd)


def _mixer_proj_body(x_ref, pos_ref, gmix_ref, w_lat_ref, w_krg_ref, w_gla_ref,
                     gqa_ref, wq_ref, gkva_ref, wkv_ref, gq_ref, gk_ref, invf_ref, w2_ref, b2_ref,
                     mla_out, gla_out, la_out):
    h = _rms(x_ref[...], gmix_ref[...]).astype(BF16)
    col = lax.broadcasted_iota(jnp.int32, (1, GLA_HEAD_W), 1)
    gla_scale = jnp.where(col < GLA_DK, GLA_DK ** -0.5, 1.0)

    def gla_head(hd):
        cs = slice(hd * GLA_HEAD_W, (hd + 1) * GLA_HEAD_W)
        gla_out[:, cs] = (_dot(h, w_gla_ref[:, cs]) * gla_scale).astype(BF16)

    zq = _dot(h, w_lat_ref[:, :MLA_Q_RANK])
    zkv = _dot(h, w_lat_ref[:, MLA_Q_RANK:])
    zkrg = _dot(h, w_krg_ref[...])
    lane = lax.broadcasted_iota(jnp.int32, (1, LANES), 1)
    zkr = jnp.where(lane % (LANES // 2) < HALF_ROPE, zkrg, 0.0)
    gla_head(0)
    gla_head(1)

    pre = _dot(zkrg.astype(BF16), w2_ref[...]) + b2_ref[...]
    log_sig = jnp.minimum(pre, 0.0) - jnp.log(1.0 + jnp.exp(-jnp.abs(pre)))
    la_out[...] = log_sig * (LOG2_E / GLA_TAU)

    ang = pos_ref[...].astype(F32) * invf_ref[...]
    cos_t = jnp.cos(ang)
    sin_t = jnp.sin(ang) * jnp.where(lane < LANES // 2, -1.0, 1.0)

    def rope(v):
        return v * cos_t + pltpu.roll(v, LANES // 2, axis=1) * sin_t

    inv_qk = 1.0 / MLA_QK

    qa = _rms(zq, gqa_ref[...]).astype(BF16)
    kva = _rms(zkv, gkva_ref[...]).astype(BF16)
    qf = _dot(qa, wq_ref[...])
    kvf = _dot(kva, wkv_ref[...])
    gla_head(2)
    for hd in range(MLA_HEADS):
        c = hd * HEAD_PAD
        o = hd * MLA_HEAD_W
        blk = qf[:, c:c + HEAD_PAD]
        ss = jnp.sum(blk * blk, axis=-1, keepdims=True)
        rinv = lax.rsqrt(ss * inv_qk + EPS) * (MLA_QK ** -0.5 * LOG2_E)
        mla_out[:, o:o + LANES] = (blk[:, :LANES] * rinv * gq_ref[:, :LANES]).astype(BF16)
        mla_out[:, o + LANES:o + HEAD_PAD] = rope(
            blk[:, LANES:] * rinv * gq_ref[:, LANES:]).astype(BF16)

    gla_head(3)

    ssr = jnp.sum(zkr * zkr, axis=-1, keepdims=True)
    kr_base = rope(zkr * gk_ref[:, LANES:])
    for hd in range(MLA_HEADS):
        kn = kvf[:, hd * MLA_NOPE:(hd + 1) * MLA_NOPE]
        ss = jnp.sum(kn * kn, axis=-1, keepdims=True) + ssr
        rinv = lax.rsqrt(ss * inv_qk + EPS)
        o = hd * MLA_HEAD_W + HEAD_PAD
        mla_out[:, o:o + LANES] = (kn * rinv * gk_ref[:, :LANES]).astype(BF16)
        mla_out[:, o + LANES:o + HEAD_PAD] = (kr_base * rinv).astype(BF16)
        v0 = (MLA_HEADS + hd) * MLA_NOPE
        mla_out[:, o + HEAD_PAD:o + HEAD_PAD + MLA_V] = kvf[:, v0:v0 + MLA_V].astype(BF16)


def _mixer_proj(x, pos, *weights):
    m, d = x.shape
    tm = PROJ_TM
    row = lambda i: (i, 0)
    widths = [
        (MLA_HEADS * MLA_HEAD_W, BF16),
        (GLA_HEADS * GLA_HEAD_W, BF16),
        (GLA_HEADS * GLA_DK, F32),
    ]
    return pl.pallas_call(
        _mixer_proj_body,
        grid=(m // tm,),
        in_specs=[pl.BlockSpec((tm, d), row), pl.BlockSpec((tm, 1), row)]
        + [_resident(w.shape) for w in weights],
        out_specs=[pl.BlockSpec((tm, w), row) for w, _ in widths],
        out_shape=[jax.ShapeDtypeStruct((m, w), dt) for w, dt in widths],
        compiler_params=pltpu.CompilerParams(
            dimension_semantics=("parallel",), vmem_limit_bytes=V7X_VMEM_LIMIT),
        name="mixer_proj",
    )(x, pos, *weights)


def _mla_attn_body(qkv_ref, o_ref, vext_ref):
    seq = qkv_ref.shape[1]
    dv = MLA_V
    tq = ATTN_TQ
    q_cols, k_cols = slice(0, HEAD_PAD), slice(HEAD_PAD, 2 * HEAD_PAD)
    r = lax.broadcasted_iota(jnp.int32, (tq, tq), 0) // CHUNK
    c = lax.broadcasted_iota(jnp.int32, (tq, tq), 1) // CHUNK
    diag_mask = c <= r

    vext_ref[:, :dv] = qkv_ref[0, :, 2 * HEAD_PAD:]
    vext_ref[:, dv:] = jnp.ones((seq, dv), vext_ref.dtype)

    def scores(i):
        lo = i * tq
        q = qkv_ref[0, lo:lo + tq, q_cols]
        s_d = jnp.where(diag_mask, _dot_nt(q, qkv_ref[0, lo:lo + tq, k_cols]), -jnp.inf)
        s_o = _dot_nt(q, qkv_ref[0, :lo, k_cols]) if i > 0 else None
        m = jnp.max(s_d, axis=-1, keepdims=True)
        if i > 0:
            m = jnp.maximum(m, jnp.max(s_o, axis=-1, keepdims=True))
        return s_d, s_o, m

    order = list(range(seq // tq))[::-1]
    ahead = 2
    pending = [scores(i) for i in order[:ahead]]
    for pos, i in enumerate(order):
        lo = i * tq
        s_d, s_o, m = pending.pop(0)
        if pos + ahead < len(order):
            pending.append(scores(order[pos + ahead]))
        pv = _dot(jnp.exp2(s_d - m).astype(BF16), vext_ref[lo:lo + tq, :])
        if i > 0:
            pv = pv + _dot(jnp.exp2(s_o - m).astype(BF16), vext_ref[:lo, :])
        o_ref[0, lo:lo + tq, :] = (pv[:, :dv] * (1.0 / pv[:, dv:dv + 1])).astype(o_ref.dtype)


def _mla_attn(qkv):
    b, s, _ = qkv.shape
    return pl.pallas_call(
        _mla_attn_body,
        grid=(b, MLA_HEADS),
        in_specs=[pl.BlockSpec((1, s, MLA_HEAD_W), lambda i, j: (i, 0, j))],
        out_specs=pl.BlockSpec((1, s, MLA_V), lambda i, j: (i, 0, j)),
        out_shape=jax.ShapeDtypeStruct((b, s, MLA_HEADS * MLA_V), BF16),
        scratch_shapes=[pltpu.VMEM((s, 2 * MLA_V), BF16)],
        compiler_params=pltpu.CompilerParams(
            dimension_semantics=("parallel", "parallel"), vmem_limit_bytes=V7X_VMEM_LIMIT),
        name="mla_attn",
    )(qkv)


def _gla_body(in_ref, la_ref, g_ref, o_ref, u_ref, st_ref, kd_ref, oraw_ref):
    seq, k_dim, v_dim = in_ref.shape[1], GLA_DK, GLA_DV
    q_cols, k_cols = slice(0, k_dim), slice(k_dim, 2 * k_dim)
    v_cols = slice(2 * k_dim, 2 * k_dim + v_dim)
    zr_cols = slice(2 * k_dim + v_dim, 2 * k_dim + 2 * v_dim)
    n_chunks = seq // CHUNK
    grp = GLA_GROUP
    r = lax.broadcasted_iota(jnp.int32, (grp, grp), 0)
    c = lax.broadcasted_iota(jnp.int32, (grp, grp), 1)
    later = ((c > r) & (c // CHUNK == r // CHUNK)).astype(BF16)

    b_end = jnp.sum(la_ref[0].reshape(n_chunks, CHUNK, k_dim), axis=1)
    dec = jnp.concatenate([jnp.exp2(b_end), jnp.zeros((LANES - n_chunks, k_dim), F32)], axis=0)
    dec_t = dec.T

    for n in range(seq // grp):
        sl = slice(n * grp, (n + 1) * grp)
        g = la_ref[0, sl, :]
        g_hi = g.astype(BF16)
        g_lo = (g - g_hi.astype(F32)).astype(BF16)
        rest2 = _dot(later, jnp.concatenate([g_hi, g_lo], axis=1))
        rest = rest2[:, :k_dim] + rest2[:, k_dim:]
        kd_ref[sl, :] = (in_ref[0, sl, k_cols].astype(F32) * jnp.exp2(rest)).astype(BF16)

    for n in range(n_chunks):
        sl = slice(n * CHUNK, (n + 1) * CHUNK)
        u_ref[n] = lax.dot_general(kd_ref[sl, :], in_ref[0, sl, v_cols], (((0,), (0,)), ((), ())),
                                   preferred_element_type=F32)

    state = jnp.zeros((k_dim, v_dim), F32)
    for n in range(n_chunks):
        state = state * dec_t[:, n:n + 1] + u_ref[n]
        st_ref[n] = state.astype(BF16)

    for n in range(seq // grp):
        for c in range(n * grp // CHUNK, (n + 1) * grp // CHUNK):
            cs = slice(c * CHUNK, (c + 1) * CHUNK)
            oraw_ref[cs, :] = _dot(in_ref[0, cs, q_cols], st_ref[c])
        sl = slice(n * grp, (n + 1) * grp)
        zr = in_ref[0, sl, zr_cols].astype(F32)
        o_ref[0, sl, :] = (_rms(oraw_ref[sl, :], g_ref[...])
                           * (zr * jax.nn.sigmoid(zr))).astype(o_ref.dtype)


def _gla(qkvz, la, g):
    b, s, _ = qkvz.shape
    n_chunks = s // CHUNK
    head = lambda w: pl.BlockSpec((1, s, w), lambda i, j: (i, 0, j))
    return pl.pallas_call(
        _gla_body,
        grid=(b, GLA_HEADS),
        in_specs=[head(GLA_HEAD_W), head(GLA_DK),
                  pl.BlockSpec((1, GLA_DV), lambda i, j: (0, 0))],
        out_specs=head(GLA_DV),
        out_shape=jax.ShapeDtypeStruct((b, s, GLA_HEADS * GLA_DV), BF16),
        scratch_shapes=[pltpu.VMEM((n_chunks, GLA_DK, GLA_DV), F32),
                        pltpu.VMEM((n_chunks, GLA_DK, GLA_DV), BF16),
                        pltpu.VMEM((s, GLA_DK), BF16),
                        pltpu.VMEM((s, GLA_DV), F32)],
        compiler_params=pltpu.CompilerParams(
            dimension_semantics=("parallel", "parallel"), vmem_limit_bytes=V7X_VMEM_LIMIT),
        name="gla",
    )(qkvz, la, g)


def _mem_kv_body(m_ref, gn_ref, wk_ref, wv_ref, gk_ref, k_ref, v_ref):
    hm = _rms(m_ref[0], gn_ref[...]).astype(BF16)
    kf = _dot(hm, wk_ref[...])
    for hd in range(MEM_HEADS):
        c = hd * MEM_HEAD_DIM
        k_ref[0, :, c:c + MEM_HEAD_DIM] = _rms(kf[:, c:c + MEM_HEAD_DIM], gk_ref[...]).astype(BF16)
    v_ref[0] = _dot(hm, wv_ref[...]).astype(BF16)


def _mem_kv(mem, gn, wk, wv, gk):
    b, n, d = mem.shape
    w = wk.shape[1]
    ospec = pl.BlockSpec((1, n, w), lambda i: (i, 0, 0))
    return pl.pallas_call(
        _mem_kv_body,
        grid=(b,),
        in_specs=[pl.BlockSpec((1, n, d), lambda i: (i, 0, 0)),
                  _resident(gn.shape), _resident(wk.shape), _resident(wv.shape),
                  _resident(gk.shape)],
        out_specs=[ospec, ospec],
        out_shape=[jax.ShapeDtypeStruct((b, n, w), BF16)] * 2,
        compiler_params=pltpu.CompilerParams(
            dimension_semantics=("parallel",), vmem_limit_bytes=V7X_VMEM_LIMIT),
        name="mem_kv",
    )(mem, gn, wk, wv, gk)


def _out_mem_body(x_ref, om_ref, og_ref, wo_ref, gn_ref, wq_ref, gq_ref, km_ref, vm_ref,
                  wmo_ref, o_ref):
    n_mla = om_ref.shape[2]
    x = x_ref[0] + _dot(om_ref[0], wo_ref[:n_mla, :]) + _dot(og_ref[0], wo_ref[n_mla:, :])
    h = _rms(x, gn_ref[...]).astype(BF16)
    qm = _dot(h, wq_ref[...])
    heads = [slice(hd * MEM_HEAD_DIM, (hd + 1) * MEM_HEAD_DIM) for hd in range(MEM_HEADS)]
    scores = []
    for c in heads:
        qn = (_rms(qm[:, c], gq_ref[...]) * (MEM_HEAD_DIM ** -0.5)).astype(BF16)
        scores.append(_dot_nt(qn, km_ref[0, :, c]))
    outs = []
    for c, s in zip(heads, scores):
        p = jnp.exp(s - jnp.max(s, axis=-1, keepdims=True))
        l = jnp.sum(p, axis=-1, keepdims=True)
        oh = _dot(p.astype(BF16), vm_ref[0, :, c]) * (1.0 / l)
        outs.append(oh.astype(BF16))
    o_ref[0] = x + _dot(jnp.concatenate(outs, axis=1), wmo_ref[...])


def _out_mem(x, om, og, wo, gn, wq, gq, km, vm, wmo):
    b, s, d = x.shape
    tm = OUT_TM
    n_mem, wm = km.shape[1], km.shape[2]
    tok = lambda w: pl.BlockSpec((1, tm, w), lambda i, j: (i, j, 0))
    memspec = pl.BlockSpec((1, n_mem, wm), lambda i, j: (i, 0, 0))
    return pl.pallas_call(
        _out_mem_body,
        grid=(b, s // tm),
        in_specs=[tok(d), tok(om.shape[2]), tok(og.shape[2]),
                  _resident(wo.shape), _resident(gn.shape), _resident(wq.shape),
                  _resident(gq.shape), memspec, memspec, _resident(wmo.shape)],
        out_specs=tok(d),
        out_shape=jax.ShapeDtypeStruct((b, s, d), F32),
        compiler_params=pltpu.CompilerParams(
            dimension_semantics=("parallel", "parallel"), vmem_limit_bytes=V7X_VMEM_LIMIT),
        name="out_mem",
    )(x, om, og, wo, gn, wq, gq, km, vm, wmo)


def _pad_rope(w):
    z = jnp.zeros(w.shape[:-1] + (LANES // 2 - HALF_ROPE,), w.dtype)
    return jnp.concatenate([w[..., :HALF_ROPE], z, w[..., HALF_ROPE:], z], axis=-1)


def _pad_head(w):
    return jnp.concatenate([w[..., :MLA_NOPE], _pad_rope(w[..., MLA_NOPE:])], axis=-1)


def _w_in_columns():
    o = 0
    parts = {}
    for name, size in (("zq", MLA_Q_RANK), ("zkv", MLA_KV_RANK), ("zkr", MLA_ROPE),
                       ("gq", GLA_HEADS * GLA_DK), ("gk", GLA_HEADS * GLA_DK),
                       ("gv", GLA_HEADS * GLA_DV), ("zg", GLA_GATE_RANK),
                       ("zr", GLA_HEADS * GLA_DV)):
        parts[name] = (o, o + size)
        o += size
    return parts


def _split_w_in_body(wt_ref, lat_ref, krg_ref, gla_ref):
    parts = _w_in_columns()
    rows = lambda a, b: wt_ref[parts[a][0]:parts[b][1], :]
    kr, zg = rows("zkr", "zkr"), rows("zg", "zg")
    zeros = lambda n: jnp.zeros((n, wt_ref.shape[1]), F32)
    krg_t = jnp.concatenate(
        [kr[:HALF_ROPE], zg, zeros(LANES // 2 - HALF_ROPE - GLA_GATE_RANK),
         kr[HALF_ROPE:], zeros(LANES // 2 - HALF_ROPE)], axis=0)
    lat_ref[...] = rows("zq", "zkv").T.astype(BF16)
    krg_ref[...] = krg_t.T.astype(BF16)
    for hd in range(GLA_HEADS):
        o = hd * GLA_HEAD_W
        for name, width in (("gq", GLA_DK), ("gk", GLA_DK), ("gv", GLA_DV), ("zr", GLA_DV)):
            r0 = parts[name][0] + hd * width
            gla_ref[:, o:o + width] = wt_ref[r0:r0 + width, :].T.astype(BF16)
            o += width


def _split_w_in(w_in, layer):
    wt = jnp.swapaxes(w_in, 1, 2)
    _, n, k = wt.shape
    parts = _w_in_columns()
    widths = [parts["zkv"][1] - parts["zq"][0], LANES, GLA_HEADS * GLA_HEAD_W]
    tr = W_IN_ROWS
    return pl.pallas_call(
        _split_w_in_body,
        grid=(k // tr,),
        in_specs=[pl.BlockSpec((None, n, tr), lambda i: (layer, 0, i))],
        out_specs=[pl.BlockSpec((tr, w), lambda i: (i, 0)) for w in widths],
        out_shape=[jax.ShapeDtypeStruct((k, w), BF16) for w in widths],
        compiler_params=pltpu.CompilerParams(
            dimension_semantics=("parallel",), vmem_limit_bytes=V7X_VMEM_LIMIT),
        name="split_w_in",
    )(wt)


def kernel(x, mem, positions, ffn1_norm, ffn1_w_gate, ffn1_w_up, ffn1_w_down, mix_norm, w_in, q_a_norm, w_q_up, kv_a_norm, w_kv_up, mla_q_norm, mla_k_norm, gla_w_gate2, gla_b_gate, gla_out_norm, w_out, mem_attn_norm, mem_norm, mem_w_q, mem_w_k, mem_w_v, mem_w_o, mem_q_norm, mem_k_norm, ffn2_norm, ffn2_w_gate, ffn2_w_up, ffn2_w_down):
    b, s, d = x.shape
    m = b * s
    depth = w_in.shape[0]
    row = lambda v: v.reshape(1, -1)
    bf = lambda w: w.astype(BF16)

    half = jnp.arange(HALF_ROPE, dtype=F32)
    inv_freq = ROPE_THETA ** (-half / HALF_ROPE)
    invf = _pad_rope(jnp.concatenate([inv_freq, inv_freq])).reshape(1, LANES)
    pos = positions.reshape(m, 1)

    xf = x.reshape(m, d)
    for l in range(depth):
        xf = _ffn(xf, row(ffn1_norm[l]), ffn1_w_gate[l], ffn1_w_up[l], ffn1_w_down[l])

        wq = _pad_head(w_q_up[l].reshape(MLA_Q_RANK, MLA_HEADS, MLA_QK)).reshape(MLA_Q_RANK, -1)
        wkv = w_kv_up[l].reshape(MLA_KV_RANK, MLA_HEADS, MLA_NOPE + MLA_V)
        wkv = jnp.concatenate([wkv[..., :MLA_NOPE].reshape(MLA_KV_RANK, -1),
                               wkv[..., MLA_NOPE:].reshape(MLA_KV_RANK, -1)], axis=1)
        w2 = jnp.pad(gla_w_gate2[l], ((HALF_ROPE, LANES - HALF_ROPE - GLA_GATE_RANK), (0, 0)))
        qkv, qkvz, la = _mixer_proj(
            xf, pos, row(mix_norm[l]), *_split_w_in(w_in, l), row(q_a_norm[l]), bf(wq),
            row(kv_a_norm[l]), bf(wkv), row(_pad_head(mla_q_norm[l])), row(_pad_head(mla_k_norm[l])),
            invf, bf(w2), row(gla_b_gate[l]))

        r3 = lambda a: a.reshape(b, s, a.shape[-1])
        o_mla = _mla_attn(r3(qkv))
        o_gla = _gla(r3(qkvz), r3(la), row(gla_out_norm[l]))

        km, vm = _mem_kv(mem, row(mem_norm[l]), bf(mem_w_k[l]), bf(mem_w_v[l]), row(mem_k_norm[l]))
        x3 = _out_mem(r3(xf), o_mla, o_gla, bf(w_out[l]), row(mem_attn_norm[l]), bf(mem_w_q[l]),
                      row(mem_q_norm[l]), km, vm, bf(mem_w_o[l]))

        xf = _ffn(x3.reshape(m, d), row(ffn2_norm[l]), ffn2_w_gate[l], ffn2_w_up[l], ffn2_w_down[l])
    return xf.reshape(b, s, d)
```

```python
import jax
import jax.numpy as jnp
from jax import lax
from jax.experimental import pallas as pl
from jax.experimental.pallas import tpu as pltpu

F32 = jnp.float32
BF16 = jnp.bfloat16

EPS = 1e-6
CHUNK = 64
ROPE_THETA = 10000.0
GLA_TAU = 16.0
LOG2_E = 1.4426950408889634

MLA_HEADS = 8
MLA_NOPE = 128
MLA_ROPE = 64
MLA_QK = MLA_NOPE + MLA_ROPE
MLA_V = 128
MLA_Q_RANK = 512
MLA_KV_RANK = 256
GLA_HEADS = 4
GLA_DK = 128
GLA_DV = 256
GLA_GATE_RANK = 16
MEM_HEADS = 4
MEM_HEAD_DIM = 128

LANES = 128
HEAD_PAD = 2 * LANES
HALF_ROPE = MLA_ROPE // 2
MLA_HEAD_W = 2 * HEAD_PAD + MLA_V
GLA_HEAD_W = 2 * GLA_DK + 2 * GLA_DV

V7X_VMEM_LIMIT = 56 * 1024 * 1024

FFN_TM = 1024
FFN_TF = 512
FFN_DOWN_SPLIT = tuple((c, c + 512) for c in range(0, 2048, 512))
FFN_NORM_CHUNKS = 8
PROJ_TM = 512
ATTN_TQ = 256
GLA_GROUP = 256
OUT_TM = 512
W_IN_ROWS = 256


def _rms(x, g):
    ms = jnp.mean(x * x, axis=-1, keepdims=True)
    return x * lax.rsqrt(ms + EPS) * g


def _dot(a, b):
    return jnp.dot(a, b, preferred_element_type=F32)


def _dot_nt(a, b):
    return lax.dot_general(a, b, (((1,), (1,)), ((), ())), preferred_element_type=F32)


def _resident(shape):
    nd = len(shape)
    return pl.BlockSpec(shape, lambda *_: (0,) * nd, pipeline_mode=pl.Buffered(1))


def _ffn_body(x_hbm, g_ref, wg_ref, wu_ref, wd_ref, o_hbm, xbuf, acc, h_ref, x_sem, o_sem):
    i, f = pl.program_id(0), pl.program_id(1)
    n_i, n_f = pl.num_programs(0), pl.num_programs(1)
    tm, d = acc.shape
    cur = i % 2
    norm_rows = tm // FFN_NORM_CHUNKS

    def rows(tile):
        return pl.ds(pl.multiple_of(tile * tm, tm), tm)

    def x_copy(tile, chunk):
        r0 = pl.multiple_of(chunk * norm_rows, norm_rows)
        return pltpu.make_async_copy(
            x_hbm.at[pl.ds(pl.multiple_of(tile * tm, tm) + r0, norm_rows), :],
            xbuf.at[pl.ds(r0, norm_rows), :], x_sem.at[chunk])

    def o_copy(tile):
        return pltpu.make_async_copy(acc, o_hbm.at[rows(tile), :], o_sem)

    @pl.when((i == 0) & (f == 0))
    def _():
        for c in range(FFN_NORM_CHUNKS):
            x_copy(0, c).start()
        for c in range(FFN_NORM_CHUNKS):
            x_copy(0, c).wait()
        h_ref[0] = _rms(xbuf[...], g_ref[...]).astype(BF16)

    @pl.when((f >= 1) & (f <= FFN_NORM_CHUNKS) & (i + 1 < n_i))
    def _():
        x_copy(i + 1, f - 1).wait()

    def body(first):
        h = h_ref[cur]
        chunk = jnp.clip(f - 1, 0, FFN_NORM_CHUNKS - 1)
        rs = pl.ds(pl.multiple_of(chunk * norm_rows, norm_rows), norm_rows)
        h_ref[1 - cur, rs, :] = _rms(xbuf[rs, :], g_ref[...]).astype(BF16)

        half = FFN_TF // 2
        acts = []
        for k0 in (0, half):
            gate = _dot(h, wg_ref[:, k0:k0 + half].astype(BF16))
            up = _dot(h, wu_ref[:, k0:k0 + half].astype(BF16))
            acts.append(((0.5 * gate) * jax.nn.sigmoid(gate) * up).astype(BF16))
        if first:
            @pl.when(i > 0)
            def _():
                o_copy(i - 1).wait()

        for c0, c1 in FFN_DOWN_SPLIT:
            part = (_dot(acts[0], wd_ref[:half, c0:c1].astype(BF16))
                    + _dot(acts[1], wd_ref[half:, c0:c1].astype(BF16)))
            if first:
                acc[:, c0:c1] = xbuf[:, c0:c1] + part
            else:
                acc[:, c0:c1] += part

    @pl.when(f == 0)
    def _():
        body(True)

        @pl.when(i + 1 < n_i)
        def _():
            for c in range(FFN_NORM_CHUNKS):
                x_copy(i + 1, c).start()

    @pl.when(f > 0)
    def _():
        body(False)

    @pl.when(f == n_f - 1)
    def _():
        o_copy(i).start()

        @pl.when(i == n_i - 1)
        def _():
            o_copy(i).wait()


def _ffn(x, g, wg, wu, wd):
    m, d = x.shape
    f = wg.shape[1]
    assert f // FFN_TF > FFN_NORM_CHUNKS and FFN_TM % (16 * FFN_NORM_CHUNKS) == 0
    assert FFN_DOWN_SPLIT[-1][1] == d
    return pl.pallas_call(
        _ffn_body,
        grid=(m // FFN_TM, f // FFN_TF),
        in_specs=[
            pl.BlockSpec(memory_space=pl.ANY),
            pl.BlockSpec((1, d), lambda i, j: (0, 0)),
            pl.BlockSpec((d, FFN_TF), lambda i, j: (0, j)),
            pl.BlockSpec((d, FFN_TF), lambda i, j: (0, j)),
            pl.BlockSpec((FFN_TF, d), lambda i, j: (j, 0)),
        ],
        out_specs=pl.BlockSpec(memory_space=pl.ANY),
        out_shape=jax.ShapeDtypeStruct((m, d), F32),
        scratch_shapes=[pltpu.VMEM((FFN_TM, d), F32),
                        pltpu.VMEM((FFN_TM, d), F32),
                        pltpu.VMEM((2, FFN_TM, d), BF16),
                        pltpu.SemaphoreType.DMA((FFN_NORM_CHUNKS,)),
                        pltpu.SemaphoreType.DMA(())],
        compiler_params=pltpu.CompilerParams(
            dimension_semantics=("arbitrary", "arbitrary"),
            vmem_limit_bytes=V7X_VMEM_LIMIT),
        name="ffn",
    )(x, g, wg, wu, wd)


def _mixer_proj_body(x_ref, pos_ref, gmix_ref, w_lat_ref, w_krg_ref, w_gla_ref,
                     gqa_ref, wq_ref, gkva_ref, wkv_ref, gq_ref, gk_ref, invf_ref, w2_ref, b2_ref,
                     mla_out, gla_out, la_out):
    h = _rms(x_ref[...], gmix_ref[...]).astype(BF16)
    col = lax.broadcasted_iota(jnp.int32, (1, GLA_HEAD_W), 1)
    gla_scale = jnp.where(col < GLA_DK, GLA_DK ** -0.5, 1.0)

    def gla_head(hd):
        cs = slice(hd * GLA_HEAD_W, (hd + 1) * GLA_HEAD_W)
        gla_out[:, cs] = (_dot(h, w_gla_ref[:, cs]) * gla_scale).astype(BF16)

    zq = _dot(h, w_lat_ref[:, :MLA_Q_RANK])
    zkv = _dot(h, w_lat_ref[:, MLA_Q_RANK:])
    zkrg = _dot(h, w_krg_ref[...])
    lane = lax.broadcasted_iota(jnp.int32, (1, LANES), 1)
    zkr = jnp.where(lane % (LANES // 2) < HALF_ROPE, zkrg, 0.0)
    gla_head(0)
    gla_head(1)

    pre = _dot(zkrg.astype(BF16), w2_ref[...]) + b2_ref[...]
    log_sig = jnp.minimum(pre, 0.0) - jnp.log(1.0 + jnp.exp(-jnp.abs(pre)))
    la_out[...] = log_sig * (LOG2_E / GLA_TAU)

    ang = pos_ref[...].astype(F32) * invf_ref[...]
    cos_t = jnp.cos(ang)
    sin_t = jnp.sin(ang) * jnp.where(lane < LANES // 2, -1.0, 1.0)

    def rope(v):
        return v * cos_t + pltpu.roll(v, LANES // 2, axis=1) * sin_t

    inv_qk = 1.0 / MLA_QK

    qa = _rms(zq, gqa_ref[...]).astype(BF16)
    kva = _rms(zkv, gkva_ref[...]).astype(BF16)
    qf = _dot(qa, wq_ref[...])
    kvf = _dot(kva, wkv_ref[...])
    gla_head(2)
    for hd in range(MLA_HEADS):
        c = hd * HEAD_PAD
        o = hd * MLA_HEAD_W
        blk = qf[:, c:c + HEAD_PAD]
        ss = jnp.sum(blk * blk, axis=-1, keepdims=True)
        rinv = lax.rsqrt(ss * inv_qk + EPS) * (MLA_QK ** -0.5 * LOG2_E)
        mla_out[:, o:o + LANES] = (blk[:, :LANES] * rinv * gq_ref[:, :LANES]).astype(BF16)
        mla_out[:, o + LANES:o + HEAD_PAD] = rope(
            blk[:, LANES:] * rinv * gq_ref[:, LANES:]).astype(BF16)

    gla_head(3)

    ssr = jnp.sum(zkr * zkr, axis=-1, keepdims=True)
    kr_base = rope(zkr * gk_ref[:, LANES:])
    for hd in range(MLA_HEADS):
        kn = kvf[:, hd * MLA_NOPE:(hd + 1) * MLA_NOPE]
        ss = jnp.sum(kn * kn, axis=-1, keepdims=True) + ssr
        rinv = lax.rsqrt(ss * inv_qk + EPS)
        o = hd * MLA_HEAD_W + HEAD_PAD
        mla_out[:, o:o + LANES] = (kn * rinv * gk_ref[:, :LANES]).astype(BF16)
        mla_out[:, o + LANES:o + HEAD_PAD] = (kr_base * rinv).astype(BF16)
        v0 = (MLA_HEADS + hd) * MLA_NOPE
        mla_out[:, o + HEAD_PAD:o + HEAD_PAD + MLA_V] = kvf[:, v0:v0 + MLA_V].astype(BF16)


def _mixer_proj(x, pos, *weights):
    m, d = x.shape
    tm = PROJ_TM
    row = lambda i: (i, 0)
    widths = [
        (MLA_HEADS * MLA_HEAD_W, BF16),
        (GLA_HEADS * GLA_HEAD_W, BF16),
        (GLA_HEADS * GLA_DK, F32),
    ]
    return pl.pallas_call(
        _mixer_proj_body,
        grid=(m // tm,),
        in_specs=[pl.BlockSpec((tm, d), row), pl.BlockSpec((tm, 1), row)]
        + [_resident(w.shape) for w in weights],
        out_specs=[pl.BlockSpec((tm, w), row) for w, _ in widths],
        out_shape=[jax.ShapeDtypeStruct((m, w), dt) for w, dt in widths],
        compiler_params=pltpu.CompilerParams(
            dimension_semantics=("parallel",), vmem_limit_bytes=V7X_VMEM_LIMIT),
        name="mixer_proj",
    )(x, pos, *weights)


def _mla_attn_body(qkv_ref, o_ref, vext_ref):
    seq = qkv_ref.shape[1]
    dv = MLA_V
    tq = ATTN_TQ
    q_cols, k_cols = slice(0, HEAD_PAD), slice(HEAD_PAD, 2 * HEAD_PAD)
    r = lax.broadcasted_iota(jnp.int32, (tq, tq), 0) // CHUNK
    c = lax.broadcasted_iota(jnp.int32, (tq, tq), 1) // CHUNK
    diag_mask = c <= r

    vext_ref[:, :dv] = qkv_ref[0, :, 2 * HEAD_PAD:]
    vext_ref[:, dv:] = jnp.ones((seq, dv), vext_ref.dtype)

    def scores(i):
        lo = i * tq
        q = qkv_ref[0, lo:lo + tq, q_cols]
        s_d = jnp.where(diag_mask, _dot_nt(q, qkv_ref[0, lo:lo + tq, k_cols]), -jnp.inf)
        s_o = _dot_nt(q, qkv_ref[0, :lo, k_cols]) if i > 0 else None
        m = jnp.max(s_d, axis=-1, keepdims=True)
        if i > 0:
            m = jnp.maximum(m, jnp.max(s_o, axis=-1, keepdims=True))
        return s_d, s_o, m

    n_tiles = seq // tq
    order = [n_tiles - 2, n_tiles - 1] + list(range(n_tiles - 2))[::-1]
    ahead = 2
    pending = [scores(i) for i in order[:ahead]]
    for pos, i in enumerate(order):
        lo = i * tq
        s_d, s_o, m = pending.pop(0)
        if pos + ahead < len(order):
            pending.append(scores(order[pos + ahead]))
        pv = _dot(jnp.exp2(s_d - m).astype(BF16), vext_ref[lo:lo + tq, :])
        if i > 0:
            pv = pv + _dot(jnp.exp2(s_o - m).astype(BF16), vext_ref[:lo, :])
        o_ref[0, lo:lo + tq, :] = (pv[:, :dv] * (1.0 / pv[:, dv:dv + 1])).astype(o_ref.dtype)


def _mla_attn(qkv):
    b, s, _ = qkv.shape
    return pl.pallas_call(
        _mla_attn_body,
        grid=(b, MLA_HEADS),
        in_specs=[pl.BlockSpec((1, s, MLA_HEAD_W), lambda i, j: (i, 0, j))],
        out_specs=pl.BlockSpec((1, s, MLA_V), lambda i, j: (i, 0, j)),
        out_shape=jax.ShapeDtypeStruct((b, s, MLA_HEADS * MLA_V), BF16),
        scratch_shapes=[pltpu.VMEM((s, 2 * MLA_V), BF16)],
        compiler_params=pltpu.CompilerParams(
            dimension_semantics=("parallel", "parallel"), vmem_limit_bytes=V7X_VMEM_LIMIT),
        name="mla_attn",
    )(qkv)


def _gla_body(in_ref, la_ref, g_ref, o_ref, u_ref, st_ref, kd_ref, oraw_ref):
    seq, k_dim, v_dim = in_ref.shape[1], GLA_DK, GLA_DV
    q_cols, k_cols = slice(0, k_dim), slice(k_dim, 2 * k_dim)
    v_cols = slice(2 * k_dim, 2 * k_dim + v_dim)
    zr_cols = slice(2 * k_dim + v_dim, 2 * k_dim + 2 * v_dim)
    n_chunks = seq // CHUNK
    grp = GLA_GROUP
    r = lax.broadcasted_iota(jnp.int32, (grp, grp), 0)
    c = lax.broadcasted_iota(jnp.int32, (grp, grp), 1)
    later = ((c > r) & (c // CHUNK == r // CHUNK)).astype(BF16)

    b_end = jnp.sum(la_ref[0].reshape(n_chunks, CHUNK, k_dim), axis=1)
    dec = jnp.concatenate([jnp.exp2(b_end), jnp.zeros((LANES - n_chunks, k_dim), F32)], axis=0)
    dec_t = dec.T

    for n in range(seq // grp):
        sl = slice(n * grp, (n + 1) * grp)
        g = la_ref[0, sl, :]
        g_hi = g.astype(BF16)
        g_lo = (g - g_hi.astype(F32)).astype(BF16)
        rest2 = _dot(later, jnp.concatenate([g_hi, g_lo], axis=1))
        rest = rest2[:, :k_dim] + rest2[:, k_dim:]
        kd_ref[sl, :] = (in_ref[0, sl, k_cols].astype(F32) * jnp.exp2(rest)).astype(BF16)

    for n in range(n_chunks):
        sl = slice(n * CHUNK, (n + 1) * CHUNK)
        u_ref[n] = lax.dot_general(kd_ref[sl, :], in_ref[0, sl, v_cols], (((0,), (0,)), ((), ())),
                                   preferred_element_type=F32)

    state = jnp.zeros((k_dim, v_dim), F32)
    for n in range(n_chunks):
        state = state * dec_t[:, n:n + 1] + u_ref[n]
        st_ref[n] = state.astype(BF16)

    for n in range(seq // grp):
        for c in range(n * grp // CHUNK, (n + 1) * grp // CHUNK):
            cs = slice(c * CHUNK, (c + 1) * CHUNK)
            oraw_ref[cs, :] = _dot(in_ref[0, cs, q_cols], st_ref[c])
        sl = slice(n * grp, (n + 1) * grp)
        zr = in_ref[0, sl, zr_cols].astype(F32)
        o_ref[0, sl, :] = (_rms(oraw_ref[sl, :], g_ref[...])
                           * (zr * jax.nn.sigmoid(zr))).astype(o_ref.dtype)


def _gla(qkvz, la, g):
    b, s, _ = qkvz.shape
    n_chunks = s // CHUNK
    head = lambda w: pl.BlockSpec((1, s, w), lambda i, j: (i, 0, j))
    return pl.pallas_call(
        _gla_body,
        grid=(b, GLA_HEADS),
        in_specs=[head(GLA_HEAD_W), head(GLA_DK),
                  pl.BlockSpec((1, GLA_DV), lambda i, j: (0, 0))],
        out_specs=head(GLA_DV),
        out_shape=jax.ShapeDtypeStruct((b, s, GLA_HEADS * GLA_DV), BF16),
        scratch_shapes=[pltpu.VMEM((n_chunks, GLA_DK, GLA_DV), F32),
                        pltpu.VMEM((n_chunks, GLA_DK, GLA_DV), BF16),
                        pltpu.VMEM((s, GLA_DK), BF16),
                        pltpu.VMEM((s, GLA_DV), F32)],
        compiler_params=pltpu.CompilerParams(
            dimension_semantics=("parallel", "parallel"), vmem_limit_bytes=V7X_VMEM_LIMIT),
        name="gla",
    )(qkvz, la, g)


def _mem_kv_body(m_ref, gn_ref, wk_ref, wv_ref, gk_ref, k_ref, v_ref):
    hm = _rms(m_ref[0], gn_ref[...]).astype(BF16)
    kf = _dot(hm, wk_ref[...])
    for hd in range(MEM_HEADS):
        c = hd * MEM_HEAD_DIM
        k_ref[0, :, c:c + MEM_HEAD_DIM] = _rms(kf[:, c:c + MEM_HEAD_DIM], gk_ref[...]).astype(BF16)
    v_ref[0] = _dot(hm, wv_ref[...]).astype(BF16)


def _mem_kv(mem, gn, wk, wv, gk):
    b, n, d = mem.shape
    w = wk.shape[1]
    ospec = pl.BlockSpec((1, n, w), lambda i: (i, 0, 0))
    return pl.pallas_call(
        _mem_kv_body,
        grid=(b,),
        in_specs=[pl.BlockSpec((1, n, d), lambda i: (i, 0, 0)),
                  _resident(gn.shape), _resident(wk.shape), _resident(wv.shape),
                  _resident(gk.shape)],
        out_specs=[ospec, ospec],
        out_shape=[jax.ShapeDtypeStruct((b, n, w), BF16)] * 2,
        compiler_params=pltpu.CompilerParams(
            dimension_semantics=("parallel",), vmem_limit_bytes=V7X_VMEM_LIMIT),
        name="mem_kv",
    )(mem, gn, wk, wv, gk)


def _out_mem_body(x_ref, om_ref, og_ref, wo_ref, gn_ref, wq_ref, gq_ref, km_ref, vm_ref,
                  wmo_ref, o_ref):
    n_mla = om_ref.shape[2]
    x = x_ref[0] + _dot(om_ref[0], wo_ref[:n_mla, :]) + _dot(og_ref[0], wo_ref[n_mla:, :])
    h = _rms(x, gn_ref[...]).astype(BF16)
    qm = _dot(h, wq_ref[...])
    heads = [slice(hd * MEM_HEAD_DIM, (hd + 1) * MEM_HEAD_DIM) for hd in range(MEM_HEADS)]
    scores = []
    for c in heads:
        qn = (_rms(qm[:, c], gq_ref[...]) * (MEM_HEAD_DIM ** -0.5)).astype(BF16)
        scores.append(_dot_nt(qn, km_ref[0, :, c]))
    outs = []
    for c, s in zip(heads, scores):
        p = jnp.exp(s - jnp.max(s, axis=-1, keepdims=True))
        l = jnp.sum(p, axis=-1, keepdims=True)
        oh = _dot(p.astype(BF16), vm_ref[0, :, c]) * (1.0 / l)
        outs.append(oh.astype(BF16))
    o_ref[0] = x + _dot(jnp.concatenate(outs, axis=1), wmo_ref[...])


def _out_mem(x, om, og, wo, gn, wq, gq, km, vm, wmo):
    b, s, d = x.shape
    tm = OUT_TM
    n_mem, wm = km.shape[1], km.shape[2]
    tok = lambda w: pl.BlockSpec((1, tm, w), lambda i, j: (i, j, 0))
    memspec = pl.BlockSpec((1, n_mem, wm), lambda i, j: (i, 0, 0))
    return pl.pallas_call(
        _out_mem_body,
        grid=(b, s // tm),
        in_specs=[tok(d), tok(om.shape[2]), tok(og.shape[2]),
                  _resident(wo.shape), _resident(gn.shape), _resident(wq.shape),
                  _resident(gq.shape), memspec, memspec, _resident(wmo.shape)],
        out_specs=tok(d),
        out_shape=jax.ShapeDtypeStruct((b, s, d), F32),
        compiler_params=pltpu.CompilerParams(
            dimension_semantics=("parallel", "parallel"), vmem_limit_bytes=V7X_VMEM_LIMIT),
        name="out_mem",
    )(x, om, og, wo, gn, wq, gq, km, vm, wmo)


def _pad_rope(w):
    z = jnp.zeros(w.shape[:-1] + (LANES // 2 - HALF_ROPE,), w.dtype)
    return jnp.concatenate([w[..., :HALF_ROPE], z, w[..., HALF_ROPE:], z], axis=-1)


def _pad_head(w):
    return jnp.concatenate([w[..., :MLA_NOPE], _pad_rope(w[..., MLA_NOPE:])], axis=-1)


def _w_in_columns():
    o = 0
    parts = {}
    for name, size in (("zq", MLA_Q_RANK), ("zkv", MLA_KV_RANK), ("zkr", MLA_ROPE),
                       ("gq", GLA_HEADS * GLA_DK), ("gk", GLA_HEADS * GLA_DK),
                       ("gv", GLA_HEADS * GLA_DV), ("zg", GLA_GATE_RANK),
                       ("zr", GLA_HEADS * GLA_DV)):
        parts[name] = (o, o + size)
        o += size
    return parts


def _split_w_in_body(wt_ref, lat_ref, krg_ref, gla_ref):
    parts = _w_in_columns()
    rows = lambda a, b: wt_ref[parts[a][0]:parts[b][1], :]
    kr, zg = rows("zkr", "zkr"), rows("zg", "zg")
    zeros = lambda n: jnp.zeros((n, wt_ref.shape[1]), F32)
    krg_t = jnp.concatenate(
        [kr[:HALF_ROPE], zg, zeros(LANES // 2 - HALF_ROPE - GLA_GATE_RANK),
         kr[HALF_ROPE:], zeros(LANES // 2 - HALF_ROPE)], axis=0)
    lat_ref[...] = rows("zq", "zkv").T.astype(BF16)
    krg_ref[...] = krg_t.T.astype(BF16)
    for hd in range(GLA_HEADS):
        o = hd * GLA_HEAD_W
        for name, width in (("gq", GLA_DK), ("gk", GLA_DK), ("gv", GLA_DV), ("zr", GLA_DV)):
            r0 = parts[name][0] + hd * width
            gla_ref[:, o:o + width] = wt_ref[r0:r0 + width, :].T.astype(BF16)
            o += width


def _split_w_in(w_in, layer):
    wt = jnp.swapaxes(w_in, 1, 2)
    _, n, k = wt.shape
    parts = _w_in_columns()
    widths = [parts["zkv"][1] - parts["zq"][0], LANES, GLA_HEADS * GLA_HEAD_W]
    tr = W_IN_ROWS
    return pl.pallas_call(
        _split_w_in_body,
        grid=(k // tr,),
        in_specs=[pl.BlockSpec((None, n, tr), lambda i: (layer, 0, i))],
        out_specs=[pl.BlockSpec((tr, w), lambda i: (i, 0)) for w in widths],
        out_shape=[jax.ShapeDtypeStruct((k, w), BF16) for w in widths],
        compiler_params=pltpu.CompilerParams(
            dimension_semantics=("parallel",), vmem_limit_bytes=V7X_VMEM_LIMIT),
        name="split_w_in",
    )(wt)


def kernel(x, mem, positions, ffn1_norm, ffn1_w_gate, ffn1_w_up, ffn1_w_down, mix_norm, w_in, q_a_norm, w_q_up, kv_a_norm, w_kv_up, mla_q_norm, mla_k_norm, gla_w_gate2, gla_b_gate, gla_out_norm, w_out, mem_attn_norm, mem_norm, mem_w_q, mem_w_k, mem_w_v, mem_w_o, mem_q_norm, mem_k_norm, ffn2_norm, ffn2_w_gate, ffn2_w_up, ffn2_w_down):
    b, s, d = x.shape
    m = b * s
    depth = w_in.shape[0]
    row = lambda v: v.reshape(1, -1)
    bf = lambda w: w.astype(BF16)

    half = jnp.arange(HALF_ROPE, dtype=F32)
    inv_freq = ROPE_THETA ** (-half / HALF_ROPE)
    invf = _pad_rope(jnp.concatenate([inv_freq, inv_freq])).reshape(1, LANES)
    pos = positions.reshape(m, 1)

    xf = x.reshape(m, d)
    for l in range(depth):
        xf = _ffn(xf, row(ffn1_norm[l]), ffn1_w_gate[l], ffn1_w_up[l], ffn1_w_down[l])

        wq = _pad_head(w_q_up[l].reshape(MLA_Q_RANK, MLA_HEADS, MLA_QK)).reshape(MLA_Q_RANK, -1)
        wkv = w_kv_up[l].reshape(MLA_KV_RANK, MLA_HEADS, MLA_NOPE + MLA_V)
        wkv = jnp.concatenate([wkv[..., :MLA_NOPE].reshape(MLA_KV_RANK, -1),
                               wkv[..., MLA_NOPE:].reshape(MLA_KV_RANK, -1)], axis=1)
        w2 = jnp.pad(gla_w_gate2[l], ((HALF_ROPE, LANES - HALF_ROPE - GLA_GATE_RANK), (0, 0)))
        qkv, qkvz, la = _mixer_proj(
            xf, pos, row(mix_norm[l]), *_split_w_in(w_in, l), row(q_a_norm[l]), bf(wq),
            row(kv_a_norm[l]), bf(wkv), row(_pad_head(mla_q_norm[l])), row(_pad_head(mla_k_norm[l])),
            invf, bf(w2), row(gla_b_gate[l]))

        r3 = lambda a: a.reshape(b, s, a.shape[-1])
        o_mla = _mla_attn(r3(qkv))
        o_gla = _gla(r3(qkvz), r3(la), row(gla_out_norm[l]))

        km, vm = _mem_kv(mem, row(mem_norm[l]), bf(mem_w_k[l]), bf(mem_w_v[l]), row(mem_k_norm[l]))
        x3 = _out_mem(r3(xf), o_mla, o_gla, bf(w_out[l]), row(mem_attn_norm[l]), bf(mem_w_q[l]),
                      row(mem_q_norm[l]), km, vm, bf(mem_w_o[l]))

        xf = _ffn(x3.reshape(m, d), row(ffn2_norm[l]), ffn2_w_gate[l], ffn2_w_up[l], ffn2_w_down[l])
    return xf.reshape(b, s, d)
```

```python
import jax
import jax.numpy as jnp
from jax import lax
from jax.experimental import pallas as pl
from jax.experimental.pallas import tpu as pltpu

F32 = jnp.float32
BF16 = jnp.bfloat16

EPS = 1e-6
CHUNK = 64
ROPE_THETA = 10000.0
GLA_TAU = 16.0
LOG2_E = 1.4426950408889634

MLA_HEADS = 8
MLA_NOPE = 128
MLA_ROPE = 64
MLA_QK = MLA_NOPE + MLA_ROPE
MLA_V = 128
MLA_Q_RANK = 512
MLA_KV_RANK = 256
GLA_HEADS = 4
GLA_DK = 128
GLA_DV = 256
GLA_GATE_RANK = 16
MEM_HEADS = 4
MEM_HEAD_DIM = 128

LANES = 128
HEAD_PAD = 2 * LANES
HALF_ROPE = MLA_ROPE // 2
MLA_HEAD_W = 2 * HEAD_PAD + MLA_V
GLA_HEAD_W = 2 * GLA_DK + 2 * GLA_DV

V7X_VMEM_LIMIT = 56 * 1024 * 1024

FFN_TM = 1024
FFN_TF = 512
FFN_DOWN_SPLIT = tuple((c, c + 512) for c in range(0, 2048, 512))
FFN_NORM_CHUNKS = 8
PROJ_TM = 512
ATTN_TQ = 256
GLA_GROUP = 256
OUT_TM = 512
W_IN_ROWS = 256


def _rms(x, g):
    ms = jnp.mean(x * x, axis=-1, keepdims=True)
    return x * lax.rsqrt(ms + EPS) * g


def _dot(a, b):
    return jnp.dot(a, b, preferred_element_type=F32)


def _dot_nt(a, b):
    return lax.dot_general(a, b, (((1,), (1,)), ((), ())), preferred_element_type=F32)


def _resident(shape):
    nd = len(shape)
    return pl.BlockSpec(shape, lambda *_: (0,) * nd, pipeline_mode=pl.Buffered(1))


def _ffn_body(x_hbm, g_ref, wg_ref, wu_ref, wd_ref, o_hbm, xbuf, acc, h_ref, x_sem, o_sem):
    i, f = pl.program_id(0), pl.program_id(1)
    n_i, n_f = pl.num_programs(0), pl.num_programs(1)
    tm, d = acc.shape
    cur = i % 2
    norm_rows = tm // FFN_NORM_CHUNKS

    def rows(tile):
        return pl.ds(pl.multiple_of(tile * tm, tm), tm)

    def x_copy(tile, chunk):
        r0 = pl.multiple_of(chunk * norm_rows, norm_rows)
        return pltpu.make_async_copy(
            x_hbm.at[pl.ds(pl.multiple_of(tile * tm, tm) + r0, norm_rows), :],
            xbuf.at[pl.ds(r0, norm_rows), :], x_sem.at[chunk])

    def o_copy(tile):
        return pltpu.make_async_copy(acc, o_hbm.at[rows(tile), :], o_sem)

    @pl.when((i == 0) & (f == 0))
    def _():
        for c in range(FFN_NORM_CHUNKS):
            x_copy(0, c).start()
        for c in range(FFN_NORM_CHUNKS):
            x_copy(0, c).wait()
        h_ref[0] = _rms(xbuf[...], g_ref[...]).astype(BF16)

    @pl.when((f >= 1) & (f <= FFN_NORM_CHUNKS) & (i + 1 < n_i))
    def _():
        x_copy(i + 1, f - 1).wait()

    def body(first):
        h = h_ref[cur]
        chunk = jnp.clip(f - 1, 0, FFN_NORM_CHUNKS - 1)
        rs = pl.ds(pl.multiple_of(chunk * norm_rows, norm_rows), norm_rows)
        h_ref[1 - cur, rs, :] = _rms(xbuf[rs, :], g_ref[...]).astype(BF16)

        half = FFN_TF // 2
        acts = []
        for k0 in (0, half):
            gate = _dot(h, wg_ref[:, k0:k0 + half].astype(BF16))
            up = _dot(h, wu_ref[:, k0:k0 + half].astype(BF16))
            acts.append(((0.5 * gate) * jax.nn.sigmoid(gate) * up).astype(BF16))
        if first:
            @pl.when(i > 0)
            def _():
                o_copy(i - 1).wait()

        for c0, c1 in FFN_DOWN_SPLIT:
            part = (_dot(acts[0], wd_ref[:half, c0:c1].astype(BF16))
                    + _dot(acts[1], wd_ref[half:, c0:c1].astype(BF16)))
            if first:
                acc[:, c0:c1] = xbuf[:, c0:c1] + part
            else:
                acc[:, c0:c1] += part

    @pl.when(f == 0)
    def _():
        body(True)

        @pl.when(i + 1 < n_i)
        def _():
            for c in range(FFN_NORM_CHUNKS):
                x_copy(i + 1, c).start()

    @pl.when(f > 0)
    def _():
        body(False)

    @pl.when(f == n_f - 1)
    def _():
        o_copy(i).start()

        @pl.when(i == n_i - 1)
        def _():
            o_copy(i).wait()


def _ffn(x, g, wg, wu, wd):
    m, d = x.shape
    f = wg.shape[1]
    assert f // FFN_TF > FFN_NORM_CHUNKS and FFN_TM % (16 * FFN_NORM_CHUNKS) == 0
    assert FFN_DOWN_SPLIT[-1][1] == d
    return pl.pallas_call(
        _ffn_body,
        grid=(m // FFN_TM, f // FFN_TF),
        in_specs=[
            pl.BlockSpec(memory_space=pl.ANY),
            pl.BlockSpec((1, d), lambda i, j: (0, 0)),
            pl.BlockSpec((d, FFN_TF), lambda i, j: (0, j)),
            pl.BlockSpec((d, FFN_TF), lambda i, j: (0, j)),
            pl.BlockSpec((FFN_TF, d), lambda i, j: (j, 0)),
        ],
        out_specs=pl.BlockSpec(memory_space=pl.ANY),
        out_shape=jax.ShapeDtypeStruct((m, d), F32),
        scratch_shapes=[pltpu.VMEM((FFN_TM, d), F32),
                        pltpu.VMEM((FFN_TM, d), F32),
                        pltpu.VMEM((2, FFN_TM, d), BF16),
                        pltpu.SemaphoreType.DMA((FFN_NORM_CHUNKS,)),
                        pltpu.SemaphoreType.DMA(())],
        compiler_params=pltpu.CompilerParams(
            dimension_semantics=("arbitrary", "arbitrary"),
            vmem_limit_bytes=V7X_VMEM_LIMIT),
        name="ffn",
    )(x, g, wg, wu, wd)


def _mixer_proj_body(x_ref, pos_ref, gmix_ref, w_lat_ref, w_krg_ref, w_gla_ref,
                     gqa_ref, wq_ref, gkva_ref, wkv_ref, gq_ref, gk_ref, invf_ref, w2_ref, b2_ref,
                     mla_out, gla_out, la_out):
    h = _rms(x_ref[...], gmix_ref[...]).astype(BF16)
    col = lax.broadcasted_iota(jnp.int32, (1, GLA_HEAD_W), 1)
    gla_scale = jnp.where(col < GLA_DK, GLA_DK ** -0.5, 1.0)

    def gla_head(hd):
        cs = slice(hd * GLA_HEAD_W, (hd + 1) * GLA_HEAD_W)
        gla_out[:, cs] = (_dot(h, w_gla_ref[:, cs]) * gla_scale).astype(BF16)

    zq = _dot(h, w_lat_ref[:, :MLA_Q_RANK])
    zkv = _dot(h, w_lat_ref[:, MLA_Q_RANK:])
    zkrg = _dot(h, w_krg_ref[...])
    lane = lax.broadcasted_iota(jnp.int32, (1, LANES), 1)
    zkr = jnp.where(lane % (LANES // 2) < HALF_ROPE, zkrg, 0.0)
    gla_head(0)
    gla_head(1)

    pre = _dot(zkrg.astype(BF16), w2_ref[...]) + b2_ref[...]
    log_sig = jnp.minimum(pre, 0.0) - jnp.log(1.0 + jnp.exp(-jnp.abs(pre)))
    la_out[...] = log_sig * (LOG2_E / GLA_TAU)

    ang = pos_ref[...].astype(F32) * invf_ref[...]
    cos_t = jnp.cos(ang)
    sin_t = jnp.sin(ang) * jnp.where(lane < LANES // 2, -1.0, 1.0)

    def rope(v):
        return v * cos_t + pltpu.roll(v, LANES // 2, axis=1) * sin_t

    inv_qk = 1.0 / MLA_QK

    qa = _rms(zq, gqa_ref[...]).astype(BF16)
    kva = _rms(zkv, gkva_ref[...]).astype(BF16)
    qf = _dot(qa, wq_ref[...])
    kvf = _dot(kva, wkv_ref[...])
    gla_head(2)
    for hd in range(MLA_HEADS):
        c = hd * HEAD_PAD
        o = hd * MLA_HEAD_W
        blk = qf[:, c:c + HEAD_PAD]
        ss = jnp.sum(blk * blk, axis=-1, keepdims=True)
        rinv = lax.rsqrt(ss * inv_qk + EPS) * (MLA_QK ** -0.5 * LOG2_E)
        mla_out[:, o:o + LANES] = (blk[:, :LANES] * rinv * gq_ref[:, :LANES]).astype(BF16)
        mla_out[:, o + LANES:o + HEAD_PAD] = rope(
            blk[:, LANES:] * rinv * gq_ref[:, LANES:]).astype(BF16)

    gla_head(3)

    ssr = jnp.sum(zkr * zkr, axis=-1, keepdims=True)
    kr_base = rope(zkr * gk_ref[:, LANES:])
    for hd in range(MLA_HEADS):
        kn = kvf[:, hd * MLA_NOPE:(hd + 1) * MLA_NOPE]
        ss = jnp.sum(kn * kn, axis=-1, keepdims=True) + ssr
        rinv = lax.rsqrt(ss * inv_qk + EPS)
        o = hd * MLA_HEAD_W + HEAD_PAD
        mla_out[:, o:o + LANES] = (kn * rinv * gk_ref[:, :LANES]).astype(BF16)
        mla_out[:, o + LANES:o + HEAD_PAD] = (kr_base * rinv).astype(BF16)
        v0 = (MLA_HEADS + hd) * MLA_NOPE
        mla_out[:, o + HEAD_PAD:o + HEAD_PAD + MLA_V] = kvf[:, v0:v0 + MLA_V].astype(BF16)


def _mixer_proj(x, pos, *weights):
    m, d = x.shape
    tm = PROJ_TM
    row = lambda i: (i, 0)
    widths = [
        (MLA_HEADS * MLA_HEAD_W, BF16),
        (GLA_HEADS * GLA_HEAD_W, BF16),
        (GLA_HEADS * GLA_DK, F32),
    ]
    return pl.pallas_call(
        _mixer_proj_body,
        grid=(m // tm,),
        in_specs=[pl.BlockSpec((tm, d), row), pl.BlockSpec((tm, 1), row)]
        + [_resident(w.shape) for w in weights],
        out_specs=[pl.BlockSpec((tm, w), row) for w, _ in widths],
        out_shape=[jax.ShapeDtypeStruct((m, w), dt) for w, dt in widths],
        compiler_params=pltpu.CompilerParams(
            dimension_semantics=("parallel",), vmem_limit_bytes=V7X_VMEM_LIMIT),
        name="mixer_proj",
    )(x, pos, *weights)


def _mla_attn_body(qkv_ref, o_ref, vext_ref):
    seq = qkv_ref.shape[1]
    dv = MLA_V
    tq = ATTN_TQ
    q_cols, k_cols = slice(0, HEAD_PAD), slice(HEAD_PAD, 2 * HEAD_PAD)
    r = lax.broadcasted_iota(jnp.int32, (tq, tq), 0) // CHUNK
    c = lax.broadcasted_iota(jnp.int32, (tq, tq), 1) // CHUNK
    diag_mask = c <= r

    vext_ref[:, :dv] = qkv_ref[0, :, 2 * HEAD_PAD:]
    vext_ref[:, dv:] = jnp.ones((seq, dv), vext_ref.dtype)

    def scores(i):
        lo = i * tq
        q = qkv_ref[0, lo:lo + tq, q_cols]
        s_d = jnp.where(diag_mask, _dot_nt(q, qkv_ref[0, lo:lo + tq, k_cols]), -jnp.inf)
        s_o = _dot_nt(q, qkv_ref[0, :lo, k_cols]) if i > 0 else None
        m = jnp.max(s_d, axis=-1, keepdims=True)
        if i > 0:
            m = jnp.maximum(m, jnp.max(s_o, axis=-1, keepdims=True))
        return s_d, s_o, m

    order = list(range(seq // tq))[::-1]
    ahead = 2
    pending = [scores(i) for i in order[:ahead]]
    for pos, i in enumerate(order):
        lo = i * tq
        s_d, s_o, m = pending.pop(0)
        if pos + ahead < len(order):
            pending.append(scores(order[pos + ahead]))
        pv = _dot(jnp.exp2(s_d - m).astype(BF16), vext_ref[lo:lo + tq, :])
        if i > 0:
            pv = pv + _dot(jnp.exp2(s_o - m).astype(BF16), vext_ref[:lo, :])
        o_ref[0, lo:lo + tq, :] = (pv[:, :dv] * (1.0 / pv[:, dv:dv + 1])).astype(o_ref.dtype)


def _mla_attn(qkv):
    b, s, _ = qkv.shape
    return pl.pallas_call(
        _mla_attn_body,
        grid=(b, MLA_HEADS),
        in_specs=[pl.BlockSpec((1, s, MLA_HEAD_W), lambda i, j: (i, 0, j))],
        out_specs=pl.BlockSpec((1, s, MLA_V), lambda i, j: (i, 0, j)),
        out_shape=jax.ShapeDtypeStruct((b, s, MLA_HEADS * MLA_V), BF16),
        scratch_shapes=[pltpu.VMEM((s, 2 * MLA_V), BF16)],
        compiler_params=pltpu.CompilerParams(
            dimension_semantics=("parallel", "parallel"), vmem_limit_bytes=V7X_VMEM_LIMIT),
        name="mla_attn",
    )(qkv)


def _gla_body(in_ref, la_ref, g_ref, o_ref, u_ref, st_ref, kd_ref, oraw_ref):
    seq, k_dim, v_dim = in_ref.shape[1], GLA_DK, GLA_DV
    q_cols, k_cols = slice(0, k_dim), slice(k_dim, 2 * k_dim)
    v_cols = slice(2 * k_dim, 2 * k_dim + v_dim)
    zr_cols = slice(2 * k_dim + v_dim, 2 * k_dim + 2 * v_dim)
    n_chunks = seq // CHUNK
    grp = GLA_GROUP
    r = lax.broadcasted_iota(jnp.int32, (grp, grp), 0)
    c = lax.broadcasted_iota(jnp.int32, (grp, grp), 1)
    later = ((c > r) & (c // CHUNK == r // CHUNK)).astype(BF16)

    b_end = jnp.sum(la_ref[0].reshape(n_chunks, CHUNK, k_dim), axis=1)
    dec = jnp.concatenate([jnp.exp2(b_end), jnp.zeros((LANES - n_chunks, k_dim), F32)], axis=0)
    dec_t = dec.T

    for n in range(seq // grp):
        sl = slice(n * grp, (n + 1) * grp)
        g = la_ref[0, sl, :]
        g_hi = g.astype(BF16)
        g_lo = (g - g_hi.astype(F32)).astype(BF16)
        rest2 = _dot(later, jnp.concatenate([g_hi, g_lo], axis=1))
        rest = rest2[:, :k_dim] + rest2[:, k_dim:]
        kd_ref[sl, :] = (in_ref[0, sl, k_cols].astype(F32) * jnp.exp2(rest)).astype(BF16)

    for n in range(n_chunks):
        sl = slice(n * CHUNK, (n + 1) * CHUNK)
        u_ref[n] = lax.dot_general(kd_ref[sl, :], in_ref[0, sl, v_cols], (((0,), (0,)), ((), ())),
                                   preferred_element_type=F32)

    state = jnp.zeros((k_dim, v_dim), F32)
    for n in range(n_chunks):
        state = state * dec_t[:, n:n + 1] + u_ref[n]
        st_ref[n] = state.astype(BF16)

    for n in range(seq // grp):
        for c in range(n * grp // CHUNK, (n + 1) * grp // CHUNK):
            cs = slice(c * CHUNK, (c + 1) * CHUNK)
            oraw_ref[cs, :] = _dot(in_ref[0, cs, q_cols], st_ref[c])
        sl = slice(n * grp, (n + 1) * grp)
        zr = in_ref[0, sl, zr_cols].astype(F32)
        o_ref[0, sl, :] = (_rms(oraw_ref[sl, :], g_ref[...])
                           * (zr * jax.nn.sigmoid(zr))).astype(o_ref.dtype)


def _gla(qkvz, la, g):
    b, s, _ = qkvz.shape
    n_chunks = s // CHUNK
    head = lambda w: pl.BlockSpec((1, s, w), lambda i, j: (i, 0, j))
    return pl.pallas_call(
        _gla_body,
        grid=(b, GLA_HEADS),
        in_specs=[head(GLA_HEAD_W), head(GLA_DK),
                  pl.BlockSpec((1, GLA_DV), lambda i, j: (0, 0))],
        out_specs=head(GLA_DV),
        out_shape=jax.ShapeDtypeStruct((b, s, GLA_HEADS * GLA_DV), BF16),
        scratch_shapes=[pltpu.VMEM((n_chunks, GLA_DK, GLA_DV), F32),
                        pltpu.VMEM((n_chunks, GLA_DK, GLA_DV), BF16),
                        pltpu.VMEM((s, GLA_DK), BF16),
                        pltpu.VMEM((s, GLA_DV), F32)],
        compiler_params=pltpu.CompilerParams(
            dimension_semantics=("parallel", "parallel"), vmem_limit_bytes=V7X_VMEM_LIMIT),
        name="gla",
    )(qkvz, la, g)


def _mem_kv_body(m_ref, gn_ref, wk_ref, wv_ref, gk_ref, k_ref, v_ref):
    hm = _rms(m_ref[0], gn_ref[...]).astype(BF16)
    kf = _dot(hm, wk_ref[...])
    for hd in range(MEM_HEADS):
        c = hd * MEM_HEAD_DIM
        k_ref[0, :, c:c + MEM_HEAD_DIM] = _rms(kf[:, c:c + MEM_HEAD_DIM], gk_ref[...]).astype(BF16)
    v_ref[0] = _dot(hm, wv_ref[...]).astype(BF16)


def _mem_kv(mem, gn, wk, wv, gk):
    b, n, d = mem.shape
    w = wk.shape[1]
    ospec = pl.BlockSpec((1, n, w), lambda i: (i, 0, 0))
    return pl.pallas_call(
        _mem_kv_body,
        grid=(b,),
        in_specs=[pl.BlockSpec((1, n, d), lambda i: (i, 0, 0)),
                  _resident(gn.shape), _resident(wk.shape), _resident(wv.shape),
                  _resident(gk.shape)],
        out_specs=[ospec, ospec],
        out_shape=[jax.ShapeDtypeStruct((b, n, w), BF16)] * 2,
        compiler_params=pltpu.CompilerParams(
            dimension_semantics=("parallel",), vmem_limit_bytes=V7X_VMEM_LIMIT),
        name="mem_kv",
    )(mem, gn, wk, wv, gk)


def _out_mem_body(x_ref, om_ref, og_ref, wo_ref, gn_ref, wq_ref, gq_ref, km_ref, vm_ref,
                  wmo_ref, o_ref):
    n_mla = om_ref.shape[2]
    x = x_ref[0] + _dot(om_ref[0], wo_ref[:n_mla, :]) + _dot(og_ref[0], wo_ref[n_mla:, :])
    h = _rms(x, gn_ref[...]).astype(BF16)
    qm = _dot(h, wq_ref[...])
    heads = [slice(hd * MEM_HEAD_DIM, (hd + 1) * MEM_HEAD_DIM) for hd in range(MEM_HEADS)]
    scores = []
    for c in heads:
        qn = (_rms(qm[:, c], gq_ref[...]) * (MEM_HEAD_DIM ** -0.5)).astype(BF16)
        scores.append(_dot_nt(qn, km_ref[0, :, c]))
    outs = []
    for c, s in zip(heads, scores):
        p = jnp.exp(s - jnp.max(s, axis=-1, keepdims=True))
        l = jnp.sum(p, axis=-1, keepdims=True)
        oh = _dot(p.astype(BF16), vm_ref[0, :, c]) * (1.0 / l)
        outs.append(oh.astype(BF16))
    o_ref[0] = x + _dot(jnp.concatenate(outs, axis=1), wmo_ref[...])


def _out_mem(x, om, og, wo, gn, wq, gq, km, vm, wmo):
    b, s, d = x.shape
    tm = OUT_TM
    n_mem, wm = km.shape[1], km.shape[2]
    tok = lambda w: pl.BlockSpec((1, tm, w), lambda i, j: (i, j, 0))
    memspec = pl.BlockSpec((1, n_mem, wm), lambda i, j: (i, 0, 0))
    return pl.pallas_call(
        _out_mem_body,
        grid=(b, s // tm),
        in_specs=[tok(d), tok(om.shape[2]), tok(og.shape[2]),
                  _resident(wo.shape), _resident(gn.shape), _resident(wq.shape),
                  _resident(gq.shape), memspec, memspec, _resident(wmo.shape)],
        out_specs=tok(d),
        out_shape=jax.ShapeDtypeStruct((b, s, d), F32),
        compiler_params=pltpu.CompilerParams(
            dimension_semantics=("parallel", "parallel"), vmem_limit_bytes=V7X_VMEM_LIMIT),
        name="out_mem",
    )(x, om, og, wo, gn, wq, gq, km, vm, wmo)


def _pad_rope(w):
    z = jnp.zeros(w.shape[:-1] + (LANES // 2 - HALF_ROPE,), w.dtype)
    return jnp.concatenate([w[..., :HALF_ROPE], z, w[..., HALF_ROPE:], z], axis=-1)


def _pad_head(w):
    return jnp.concatenate([w[..., :MLA_NOPE], _pad_rope(w[..., MLA_NOPE:])], axis=-1)


def _w_in_columns():
    o = 0
    parts = {}
    for name, size in (("zq", MLA_Q_RANK), ("zkv", MLA_KV_RANK), ("zkr", MLA_ROPE),
                       ("gq", GLA_HEADS * GLA_DK), ("gk", GLA_HEADS * GLA_DK),
                       ("gv", GLA_HEADS * GLA_DV), ("zg", GLA_GATE_RANK),
                       ("zr", GLA_HEADS * GLA_DV)):
        parts[name] = (o, o + size)
        o += size
    return parts


def _split_w_in_body(wt_ref, lat_ref, krg_ref, gla_ref):
    parts = _w_in_columns()
    rows = lambda a, b: wt_ref[parts[a][0]:parts[b][1], :]
    kr, zg = rows("zkr", "zkr"), rows("zg", "zg")
    zeros = lambda n: jnp.zeros((n, wt_ref.shape[1]), F32)
    krg_t = jnp.concatenate(
        [kr[:HALF_ROPE], zg, zeros(LANES // 2 - HALF_ROPE - GLA_GATE_RANK),
         kr[HALF_ROPE:], zeros(LANES // 2 - HALF_ROPE)], axis=0)
    lat_ref[...] = rows("zq", "zkv").T.astype(BF16)
    krg_ref[...] = krg_t.T.astype(BF16)
    for hd in range(GLA_HEADS):
        o = hd * GLA_HEAD_W
        for name, width in (("gq", GLA_DK), ("gk", GLA_DK), ("gv", GLA_DV), ("zr", GLA_DV)):
            r0 = parts[name][0] + hd * width
            gla_ref[:, o:o + width] = wt_ref[r0:r0 + width, :].T.astype(BF16)
            o += width


def _split_w_in(w_in, layer):
    wt = jnp.swapaxes(w_in, 1, 2)
    _, n, k = wt.shape
    parts = _w_in_columns()
    widths = [parts["zkv"][1] - parts["zq"][0], LANES, GLA_HEADS * GLA_HEAD_W]
    tr = W_IN_ROWS
    return pl.pallas_call(
        _split_w_in_body,
        grid=(k // tr,),
        in_specs=[pl.BlockSpec((None, n, tr), lambda i: (layer, 0, i))],
        out_specs=[pl.BlockSpec((tr, w), lambda i: (i, 0)) for w in widths],
        out_shape=[jax.ShapeDtypeStruct((k, w), BF16) for w in widths],
        compiler_params=pltpu.CompilerParams(
            dimension_semantics=("parallel",), vmem_limit_bytes=V7X_VMEM_LIMIT),
        name="split_w_in",
    )(wt)


def kernel(x, mem, positions, ffn1_norm, ffn1_w_gate, ffn1_w_up, ffn1_w_down, mix_norm, w_in, q_a_norm, w_q_up, kv_a_norm, w_kv_up, mla_q_norm, mla_k_norm, gla_w_gate2, gla_b_gate, gla_out_norm, w_out, mem_attn_norm, mem_norm, mem_w_q, mem_w_k, mem_w_v, mem_w_o, mem_q_norm, mem_k_norm, ffn2_norm, ffn2_w_gate, ffn2_w_up, ffn2_w_down):
    b, s, d = x.shape
    m = b * s
    depth = w_in.shape[0]
    row = lambda v: v.reshape(1, -1)
    bf = lambda w: w.astype(BF16)

    half = jnp.arange(HALF_ROPE, dtype=F32)
    inv_freq = ROPE_THETA ** (-half / HALF_ROPE)
    invf = _pad_rope(jnp.concatenate([inv_freq, inv_freq])).reshape(1, LANES)
    pos = positions.reshape(m, 1)

    xf = x.reshape(m, d)
    for l in range(depth):
        xf = _ffn(xf, row(ffn1_norm[l]), ffn1_w_gate[l], ffn1_w_up[l], ffn1_w_down[l])

        wq = _pad_head(w_q_up[l].reshape(MLA_Q_RANK, MLA_HEADS, MLA_QK)).reshape(MLA_Q_RANK, -1)
        wkv = w_kv_up[l].reshape(MLA_KV_RANK, MLA_HEADS, MLA_NOPE + MLA_V)
        wkv = jnp.concatenate([wkv[..., :MLA_NOPE].reshape(MLA_KV_RANK, -1),
                               wkv[..., MLA_NOPE:].reshape(MLA_KV_RANK, -1)], axis=1)
        w2 = jnp.pad(gla_w_gate2[l], ((HALF_ROPE, LANES - HALF_ROPE - GLA_GATE_RANK), (0, 0)))
        qkv, qkvz, la = _mixer_proj(
            xf, pos, row(mix_norm[l]), *_split_w_in(w_in, l), row(q_a_norm[l]), bf(wq),
            row(kv_a_norm[l]), bf(wkv), row(_pad_head(mla_q_norm[l])), row(_pad_head(mla_k_norm[l])),
            invf, bf(w2), row(gla_b_gate[l]))

        r3 = lambda a: a.reshape(b, s, a.shape[-1])
        o_mla = _mla_attn(r3(qkv))
        o_gla = _gla(r3(qkvz), r3(la), row(gla_out_norm[l]))

        km, vm = _mem_kv(mem, row(mem_norm[l]), bf(mem_w_k[l]), bf(mem_w_v[l]), row(mem_k_norm[l]))
        x3 = _out_mem(r3(xf), o_mla, o_gla, bf(w_out[l]), row(mem_attn_norm[l]), bf(mem_w_q[l]),
                      row(mem_q_norm[l]), km, vm, bf(mem_w_o[l]))

        xf = _ffn(x3.reshape(m, d), row(ffn2_norm[l]), ffn2_w_gate[l], ffn2_w_up[l], ffn2_w_down[l])
    return xf.reshape(b, s, d)
```

```python
import jax
import jax.numpy as jnp
from jax import lax
from jax.experimental import pallas as pl
from jax.experimental.pallas import tpu as pltpu

F32 = jnp.float32
BF16 = jnp.bfloat16

EPS = 1e-6
CHUNK = 64
ROPE_THETA = 10000.0
GLA_TAU = 16.0
LOG2_E = 1.4426950408889634

MLA_HEADS = 8
MLA_NOPE = 128
MLA_ROPE = 64
MLA_QK = MLA_NOPE + MLA_ROPE
MLA_V = 128
MLA_Q_RANK = 512
MLA_KV_RANK = 256
GLA_HEADS = 4
GLA_DK = 128
GLA_DV = 256
GLA_GATE_RANK = 16
MEM_HEADS = 4
MEM_HEAD_DIM = 128

LANES = 128
HEAD_PAD = 2 * LANES
HALF_ROPE = MLA_ROPE // 2
MLA_HEAD_W = 2 * HEAD_PAD + MLA_V
GLA_HEAD_W = 2 * GLA_DK + 2 * GLA_DV

V7X_VMEM_LIMIT = 56 * 1024 * 1024

FFN_TM = 1024
FFN_TF = 512
FFN_DOWN_SPLIT = tuple((c, c + 512) for c in range(0, 2048, 512))
FFN_NORM_CHUNKS = 8
PROJ_TM = 512
ATTN_TQ = 256
GLA_GROUP = 256
OUT_TM = 512
W_IN_ROWS = 256


def _rms(x, g):
    ms = jnp.mean(x * x, axis=-1, keepdims=True)
    return x * lax.rsqrt(ms + EPS) * g


def _dot(a, b):
    return jnp.dot(a, b, preferred_element_type=F32)


def _dot_nt(a, b):
    return lax.dot_general(a, b, (((1,), (1,)), ((), ())), preferred_element_type=F32)


def _resident(shape):
    nd = len(shape)
    return pl.BlockSpec(shape, lambda *_: (0,) * nd, pipeline_mode=pl.Buffered(1))


def _ffn_body(x_hbm, g_ref, wg_ref, wu_ref, wd_ref, o_hbm, xbuf, acc, h_ref, x_sem, o_sem):
    i, f = pl.program_id(0), pl.program_id(1)
    n_i, n_f = pl.num_programs(0), pl.num_programs(1)
    tm, d = acc.shape
    cur = i % 2
    norm_rows = tm // FFN_NORM_CHUNKS

    def rows(tile):
        return pl.ds(pl.multiple_of(tile * tm, tm), tm)

    def x_copy(tile, chunk):
        r0 = pl.multiple_of(chunk * norm_rows, norm_rows)
        return pltpu.make_async_copy(
            x_hbm.at[pl.ds(pl.multiple_of(tile * tm, tm) + r0, norm_rows), :],
            xbuf.at[pl.ds(r0, norm_rows), :], x_sem.at[chunk])

    def o_copy(tile):
        return pltpu.make_async_copy(acc, o_hbm.at[rows(tile), :], o_sem)

    @pl.when((i == 0) & (f == 0))
    def _():
        for c in range(FFN_NORM_CHUNKS):
            x_copy(0, c).start()
        for c in range(FFN_NORM_CHUNKS):
            x_copy(0, c).wait()
        h_ref[0] = _rms(xbuf[...], g_ref[...]).astype(BF16)

    @pl.when((f >= 2) & (f < FFN_NORM_CHUNKS + 2) & (i + 1 < n_i))
    def _():
        x_copy(i + 1, f - 2).wait()

    def body(first):
        h = h_ref[cur]
        last = FFN_NORM_CHUNKS - 1
        chunk = jnp.where(f < 2, last, jnp.minimum(f - 2, last))
        rs = pl.ds(pl.multiple_of(chunk * norm_rows, norm_rows), norm_rows)
        h_ref[1 - cur, rs, :] = _rms(xbuf[rs, :], g_ref[...]).astype(BF16)

        half = FFN_TF // 2
        acts = []
        for k0 in (0, half):
            gate = _dot(h, wg_ref[:, k0:k0 + half].astype(BF16))
            up = _dot(h, wu_ref[:, k0:k0 + half].astype(BF16))
            acts.append(((0.5 * gate) * jax.nn.sigmoid(gate) * up).astype(BF16))
        if first:
            @pl.when(i > 0)
            def _():
                o_copy(i - 1).wait()

        for c0, c1 in FFN_DOWN_SPLIT:
            part = (_dot(acts[0], wd_ref[:half, c0:c1].astype(BF16))
                    + _dot(acts[1], wd_ref[half:, c0:c1].astype(BF16)))
            if first:
                acc[:, c0:c1] = xbuf[:, c0:c1] + part
            else:
                acc[:, c0:c1] += part

    @pl.when(f == 0)
    def _():
        body(True)

    @pl.when(f > 0)
    def _():
        body(False)

    @pl.when((f < FFN_NORM_CHUNKS) & (i + 1 < n_i))
    def _():
        x_copy(i + 1, f).start()

    @pl.when(f == n_f - 1)
    def _():
        o_copy(i).start()

        @pl.when(i == n_i - 1)
        def _():
            o_copy(i).wait()


def _ffn(x, g, wg, wu, wd):
    m, d = x.shape
    f = wg.shape[1]
    assert f // FFN_TF >= FFN_NORM_CHUNKS + 2 and FFN_TM % (16 * FFN_NORM_CHUNKS) == 0
    assert FFN_DOWN_SPLIT[-1][1] == d
    return pl.pallas_call(
        _ffn_body,
        grid=(m // FFN_TM, f // FFN_TF),
        in_specs=[
            pl.BlockSpec(memory_space=pl.ANY),
            pl.BlockSpec((1, d), lambda i, j: (0, 0)),
            pl.BlockSpec((d, FFN_TF), lambda i, j: (0, j)),
            pl.BlockSpec((d, FFN_TF), lambda i, j: (0, j)),
            pl.BlockSpec((FFN_TF, d), lambda i, j: (j, 0)),
        ],
        out_specs=pl.BlockSpec(memory_space=pl.ANY),
        out_shape=jax.ShapeDtypeStruct((m, d), F32),
        scratch_shapes=[pltpu.VMEM((FFN_TM, d), F32),
                        pltpu.VMEM((FFN_TM, d), F32),
                        pltpu.VMEM((2, FFN_TM, d), BF16),
                        pltpu.SemaphoreType.DMA((FFN_NORM_CHUNKS,)),
                        pltpu.SemaphoreType.DMA(())],
        compiler_params=pltpu.CompilerParams(
            dimension_semantics=("arbitrary", "arbitrary"),
            vmem_limit_bytes=V7X_VMEM_LIMIT),
        name="ffn",
    )(x, g, wg, wu, wd)


def _mixer_proj_body(x_ref, pos_ref, gmix_ref, w_lat_ref, w_krg_ref, w_gla_ref,
                     gqa_ref, wq_ref, gkva_ref, wkv_ref, gq_ref, gk_ref, invf_ref, w2_ref, b2_ref,
                     mla_out, gla_out, la_out):
    h = _rms(x_ref[...], gmix_ref[...]).astype(BF16)
    col = lax.broadcasted_iota(jnp.int32, (1, GLA_HEAD_W), 1)
    gla_scale = jnp.where(col < GLA_DK, GLA_DK ** -0.5, 1.0)

    def gla_head(hd):
        cs = slice(hd * GLA_HEAD_W, (hd + 1) * GLA_HEAD_W)
        gla_out[:, cs] = (_dot(h, w_gla_ref[:, cs]) * gla_scale).astype(BF16)

    zq = _dot(h, w_lat_ref[:, :MLA_Q_RANK])
    zkv = _dot(h, w_lat_ref[:, MLA_Q_RANK:])
    zkrg = _dot(h, w_krg_ref[...])
    lane = lax.broadcasted_iota(jnp.int32, (1, LANES), 1)
    zkr = jnp.where(lane % (LANES // 2) < HALF_ROPE, zkrg, 0.0)
    gla_head(0)
    gla_head(1)

    pre = _dot(zkrg.astype(BF16), w2_ref[...]) + b2_ref[...]
    log_sig = jnp.minimum(pre, 0.0) - jnp.log(1.0 + jnp.exp(-jnp.abs(pre)))
    la_out[...] = log_sig * (LOG2_E / GLA_TAU)

    ang = pos_ref[...].astype(F32) * invf_ref[...]
    cos_t = jnp.cos(ang)
    sin_t = jnp.sin(ang) * jnp.where(lane < LANES // 2, -1.0, 1.0)

    def rope(v):
        return v * cos_t + pltpu.roll(v, LANES // 2, axis=1) * sin_t

    inv_qk = 1.0 / MLA_QK

    qa = _rms(zq, gqa_ref[...]).astype(BF16)
    kva = _rms(zkv, gkva_ref[...]).astype(BF16)
    qf = _dot(qa, wq_ref[...])
    kvf = _dot(kva, wkv_ref[...])
    gla_head(2)
    for hd in range(MLA_HEADS):
        c = hd * HEAD_PAD
        o = hd * MLA_HEAD_W
        blk = qf[:, c:c + HEAD_PAD]
        ss = jnp.sum(blk * blk, axis=-1, keepdims=True)
        rinv = lax.rsqrt(ss * inv_qk + EPS) * (MLA_QK ** -0.5 * LOG2_E)
        mla_out[:, o:o + LANES] = (blk[:, :LANES] * rinv * gq_ref[:, :LANES]).astype(BF16)
        mla_out[:, o + LANES:o + HEAD_PAD] = rope(
            blk[:, LANES:] * rinv * gq_ref[:, LANES:]).astype(BF16)

    gla_head(3)

    ssr = jnp.sum(zkr * zkr, axis=-1, keepdims=True)
    kr_base = rope(zkr * gk_ref[:, LANES:])
    for hd in range(MLA_HEADS):
        kn = kvf[:, hd * MLA_NOPE:(hd + 1) * MLA_NOPE]
        ss = jnp.sum(kn * kn, axis=-1, keepdims=True) + ssr
        rinv = lax.rsqrt(ss * inv_qk + EPS)
        o = hd * MLA_HEAD_W + HEAD_PAD
        mla_out[:, o:o + LANES] = (kn * rinv * gk_ref[:, :LANES]).astype(BF16)
        mla_out[:, o + LANES:o + HEAD_PAD] = (kr_base * rinv).astype(BF16)
        v0 = (MLA_HEADS + hd) * MLA_NOPE
        mla_out[:, o + HEAD_PAD:o + HEAD_PAD + MLA_V] = kvf[:, v0:v0 + MLA_V].astype(BF16)


def _mixer_proj(x, pos, *weights):
    m, d = x.shape
    tm = PROJ_TM
    row = lambda i: (i, 0)
    widths = [
        (MLA_HEADS * MLA_HEAD_W, BF16),
        (GLA_HEADS * GLA_HEAD_W, BF16),
        (GLA_HEADS * GLA_DK, F32),
    ]
    return pl.pallas_call(
        _mixer_proj_body,
        grid=(m // tm,),
        in_specs=[pl.BlockSpec((tm, d), row), pl.BlockSpec((tm, 1), row)]
        + [_resident(w.shape) for w in weights],
        out_specs=[pl.BlockSpec((tm, w), row) for w, _ in widths],
        out_shape=[jax.ShapeDtypeStruct((m, w), dt) for w, dt in widths],
        compiler_params=pltpu.CompilerParams(
            dimension_semantics=("parallel",), vmem_limit_bytes=V7X_VMEM_LIMIT),
        name="mixer_proj",
    )(x, pos, *weights)


def _mla_attn_body(qkv_ref, o_ref, vext_ref):
    seq = qkv_ref.shape[1]
    dv = MLA_V
    tq = ATTN_TQ
    q_cols, k_cols = slice(0, HEAD_PAD), slice(HEAD_PAD, 2 * HEAD_PAD)
    r = lax.broadcasted_iota(jnp.int32, (tq, tq), 0) // CHUNK
    c = lax.broadcasted_iota(jnp.int32, (tq, tq), 1) // CHUNK
    diag_mask = c <= r

    vext_ref[:, :dv] = qkv_ref[0, :, 2 * HEAD_PAD:]
    vext_ref[:, dv:] = jnp.ones((seq, dv), vext_ref.dtype)

    def scores(i):
        lo = i * tq
        q = qkv_ref[0, lo:lo + tq, q_cols]
        s_d = jnp.where(diag_mask, _dot_nt(q, qkv_ref[0, lo:lo + tq, k_cols]), -jnp.inf)
        s_o = _dot_nt(q, qkv_ref[0, :lo, k_cols]) if i > 0 else None
        m = jnp.max(s_d, axis=-1, keepdims=True)
        if i > 0:
            m = jnp.maximum(m, jnp.max(s_o, axis=-1, keepdims=True))
        return s_d, s_o, m

    order = list(range(seq // tq))[::-1]
    ahead = 2
    pending = [scores(i) for i in order[:ahead]]
    for pos, i in enumerate(order):
        lo = i * tq
        s_d, s_o, m = pending.pop(0)
        if pos + ahead < len(order):
            pending.append(scores(order[pos + ahead]))
        pv = _dot(jnp.exp2(s_d - m).astype(BF16), vext_ref[lo:lo + tq, :])
        if i > 0:
            pv = pv + _dot(jnp.exp2(s_o - m).astype(BF16), vext_ref[:lo, :])
        o_ref[0, lo:lo + tq, :] = (pv[:, :dv] * (1.0 / pv[:, dv:dv + 1])).astype(o_ref.dtype)


def _mla_attn(qkv):
    b, s, _ = qkv.shape
    return pl.pallas_call(
        _mla_attn_body,
        grid=(b, MLA_HEADS),
        in_specs=[pl.BlockSpec((1, s, MLA_HEAD_W), lambda i, j: (i, 0, j))],
        out_specs=pl.BlockSpec((1, s, MLA_V), lambda i, j: (i, 0, j)),
        out_shape=jax.ShapeDtypeStruct((b, s, MLA_HEADS * MLA_V), BF16),
        scratch_shapes=[pltpu.VMEM((s, 2 * MLA_V), BF16)],
        compiler_params=pltpu.CompilerParams(
            dimension_semantics=("parallel", "parallel"), vmem_limit_bytes=V7X_VMEM_LIMIT),
        name="mla_attn",
    )(qkv)


def _gla_body(in_ref, la_ref, g_ref, o_ref, u_ref, st_ref, kd_ref, oraw_ref):
    seq, k_dim, v_dim = in_ref.shape[1], GLA_DK, GLA_DV
    q_cols, k_cols = slice(0, k_dim), slice(k_dim, 2 * k_dim)
    v_cols = slice(2 * k_dim, 2 * k_dim + v_dim)
    zr_cols = slice(2 * k_dim + v_dim, 2 * k_dim + 2 * v_dim)
    n_chunks = seq // CHUNK
    grp = GLA_GROUP
    r = lax.broadcasted_iota(jnp.int32, (grp, grp), 0)
    c = lax.broadcasted_iota(jnp.int32, (grp, grp), 1)
    later = ((c > r) & (c // CHUNK == r // CHUNK)).astype(BF16)

    b_end = jnp.sum(la_ref[0].reshape(n_chunks, CHUNK, k_dim), axis=1)
    dec = jnp.concatenate([jnp.exp2(b_end), jnp.zeros((LANES - n_chunks, k_dim), F32)], axis=0)
    dec_t = dec.T

    for n in range(seq // grp):
        sl = slice(n * grp, (n + 1) * grp)
        g = la_ref[0, sl, :]
        g_hi = g.astype(BF16)
        g_lo = (g - g_hi.astype(F32)).astype(BF16)
        rest2 = _dot(later, jnp.concatenate([g_hi, g_lo], axis=1))
        rest = rest2[:, :k_dim] + rest2[:, k_dim:]
        kd_ref[sl, :] = (in_ref[0, sl, k_cols].astype(F32) * jnp.exp2(rest)).astype(BF16)

    for n in range(n_chunks):
        sl = slice(n * CHUNK, (n + 1) * CHUNK)
        u_ref[n] = lax.dot_general(kd_ref[sl, :], in_ref[0, sl, v_cols], (((0,), (0,)), ((), ())),
                                   preferred_element_type=F32)

    state = jnp.zeros((k_dim, v_dim), F32)
    for n in range(n_chunks):
        state = state * dec_t[:, n:n + 1] + u_ref[n]
        st_ref[n] = state.astype(BF16)

    for n in range(seq // grp):
        for c in range(n * grp // CHUNK, (n + 1) * grp // CHUNK):
            cs = slice(c * CHUNK, (c + 1) * CHUNK)
            oraw_ref[cs, :] = _dot(in_ref[0, cs, q_cols], st_ref[c])
        sl = slice(n * grp, (n + 1) * grp)
        zr = in_ref[0, sl, zr_cols].astype(F32)
        o_ref[0, sl, :] = (_rms(oraw_ref[sl, :], g_ref[...])
                           * (zr * jax.nn.sigmoid(zr))).astype(o_ref.dtype)


def _gla(qkvz, la, g):
    b, s, _ = qkvz.shape
    n_chunks = s // CHUNK
    head = lambda w: pl.BlockSpec((1, s, w), lambda i, j: (i, 0, j))
    return pl.pallas_call(
        _gla_body,
        grid=(b, GLA_HEADS),
        in_specs=[head(GLA_HEAD_W), head(GLA_DK),
                  pl.BlockSpec((1, GLA_DV), lambda i, j: (0, 0))],
        out_specs=head(GLA_DV),
        out_shape=jax.ShapeDtypeStruct((b, s, GLA_HEADS * GLA_DV), BF16),
        scratch_shapes=[pltpu.VMEM((n_chunks, GLA_DK, GLA_DV), F32),
                        pltpu.VMEM((n_chunks, GLA_DK, GLA_DV), BF16),
                        pltpu.VMEM((s, GLA_DK), BF16),
                        pltpu.VMEM((s, GLA_DV), F32)],
        compiler_params=pltpu.CompilerParams(
            dimension_semantics=("parallel", "parallel"), vmem_limit_bytes=V7X_VMEM_LIMIT),
        name="gla",
    )(qkvz, la, g)


def _mem_kv_body(m_ref, gn_ref, wk_ref, wv_ref, gk_ref, k_ref, v_ref):
    hm = _rms(m_ref[0], gn_ref[...]).astype(BF16)
    kf = _dot(hm, wk_ref[...])
    for hd in range(MEM_HEADS):
        c = hd * MEM_HEAD_DIM
        k_ref[0, :, c:c + MEM_HEAD_DIM] = _rms(kf[:, c:c + MEM_HEAD_DIM], gk_ref[...]).astype(BF16)
    v_ref[0] = _dot(hm, wv_ref[...]).astype(BF16)


def _mem_kv(mem, gn, wk, wv, gk):
    b, n, d = mem.shape
    w = wk.shape[1]
    ospec = pl.BlockSpec((1, n, w), lambda i: (i, 0, 0))
    return pl.pallas_call(
        _mem_kv_body,
        grid=(b,),
        in_specs=[pl.BlockSpec((1, n, d), lambda i: (i, 0, 0)),
                  _resident(gn.shape), _resident(wk.shape), _resident(wv.shape),
                  _resident(gk.shape)],
        out_specs=[ospec, ospec],
        out_shape=[jax.ShapeDtypeStruct((b, n, w), BF16)] * 2,
        compiler_params=pltpu.CompilerParams(
            dimension_semantics=("parallel",), vmem_limit_bytes=V7X_VMEM_LIMIT),
        name="mem_kv",
    )(mem, gn, wk, wv, gk)


def _out_mem_body(x_ref, om_ref, og_ref, wo_ref, gn_ref, wq_ref, gq_ref, km_ref, vm_ref,
                  wmo_ref, o_ref):
    n_mla = om_ref.shape[2]
    x = x_ref[0] + _dot(om_ref[0], wo_ref[:n_mla, :]) + _dot(og_ref[0], wo_ref[n_mla:, :])
    h = _rms(x, gn_ref[...]).astype(BF16)
    qm = _dot(h, wq_ref[...])
    heads = [slice(hd * MEM_HEAD_DIM, (hd + 1) * MEM_HEAD_DIM) for hd in range(MEM_HEADS)]
    scores = []
    for c in heads:
        qn = (_rms(qm[:, c], gq_ref[...]) * (MEM_HEAD_DIM ** -0.5)).astype(BF16)
        scores.append(_dot_nt(qn, km_ref[0, :, c]))
    outs = []
    for c, s in zip(heads, scores):
        p = jnp.exp(s - jnp.max(s, axis=-1, keepdims=True))
        l = jnp.sum(p, axis=-1, keepdims=True)
        oh = _dot(p.astype(BF16), vm_ref[0, :, c]) * (1.0 / l)
        outs.append(oh.astype(BF16))
    o_ref[0] = x + _dot(jnp.concatenate(outs, axis=1), wmo_ref[...])


def _out_mem(x, om, og, wo, gn, wq, gq, km, vm, wmo):
    b, s, d = x.shape
    tm = OUT_TM
    n_mem, wm = km.shape[1], km.shape[2]
    tok = lambda w: pl.BlockSpec((1, tm, w), lambda i, j: (i, j, 0))
    memspec = pl.BlockSpec((1, n_mem, wm), lambda i, j: (i, 0, 0))
    return pl.pallas_call(
        _out_mem_body,
        grid=(b, s // tm),
        in_specs=[tok(d), tok(om.shape[2]), tok(og.shape[2]),
                  _resident(wo.shape), _resident(gn.shape), _resident(wq.shape),
                  _resident(gq.shape), memspec, memspec, _resident(wmo.shape)],
        out_specs=tok(d),
        out_shape=jax.ShapeDtypeStruct((b, s, d), F32),
        compiler_params=pltpu.CompilerParams(
            dimension_semantics=("parallel", "parallel"), vmem_limit_bytes=V7X_VMEM_LIMIT),
        name="out_mem",
    )(x, om, og, wo, gn, wq, gq, km, vm, wmo)


def _pad_rope(w):
    z = jnp.zeros(w.shape[:-1] + (LANES // 2 - HALF_ROPE,), w.dtype)
    return jnp.concatenate([w[..., :HALF_ROPE], z, w[..., HALF_ROPE:], z], axis=-1)


def _pad_head(w):
    return jnp.concatenate([w[..., :MLA_NOPE], _pad_rope(w[..., MLA_NOPE:])], axis=-1)


def _w_in_columns():
    o = 0
    parts = {}
    for name, size in (("zq", MLA_Q_RANK), ("zkv", MLA_KV_RANK), ("zkr", MLA_ROPE),
                       ("gq", GLA_HEADS * GLA_DK), ("gk", GLA_HEADS * GLA_DK),
                       ("gv", GLA_HEADS * GLA_DV), ("zg", GLA_GATE_RANK),
                       ("zr", GLA_HEADS * GLA_DV)):
        parts[name] = (o, o + size)
        o += size
    return parts


def _split_w_in_body(wt_ref, lat_ref, krg_ref, gla_ref):
    parts = _w_in_columns()
    rows = lambda a, b: wt_ref[parts[a][0]:parts[b][1], :]
    kr, zg = rows("zkr", "zkr"), rows("zg", "zg")
    zeros = lambda n: jnp.zeros((n, wt_ref.shape[1]), F32)
    krg_t = jnp.concatenate(
        [kr[:HALF_ROPE], zg, zeros(LANES // 2 - HALF_ROPE - GLA_GATE_RANK),
         kr[HALF_ROPE:], zeros(LANES // 2 - HALF_ROPE)], axis=0)
    lat_ref[...] = rows("zq", "zkv").T.astype(BF16)
    krg_ref[...] = krg_t.T.astype(BF16)
    for hd in range(GLA_HEADS):
        o = hd * GLA_HEAD_W
        for name, width in (("gq", GLA_DK), ("gk", GLA_DK), ("gv", GLA_DV), ("zr", GLA_DV)):
            r0 = parts[name][0] + hd * width
            gla_ref[:, o:o + width] = wt_ref[r0:r0 + width, :].T.astype(BF16)
            o += width


def _split_w_in(w_in, layer):
    wt = jnp.swapaxes(w_in, 1, 2)
    _, n, k = wt.shape
    parts = _w_in_columns()
    widths = [parts["zkv"][1] - parts["zq"][0], LANES, GLA_HEADS * GLA_HEAD_W]
    tr = W_IN_ROWS
    return pl.pallas_call(
        _split_w_in_body,
        grid=(k // tr,),
        in_specs=[pl.BlockSpec((None, n, tr), lambda i: (layer, 0, i))],
        out_specs=[pl.BlockSpec((tr, w), lambda i: (i, 0)) for w in widths],
        out_shape=[jax.ShapeDtypeStruct((k, w), BF16) for w in widths],
        compiler_params=pltpu.CompilerParams(
            dimension_semantics=("parallel",), vmem_limit_bytes=V7X_VMEM_LIMIT),
        name="split_w_in",
    )(wt)


def kernel(x, mem, positions, ffn1_norm, ffn1_w_gate, ffn1_w_up, ffn1_w_down, mix_norm, w_in, q_a_norm, w_q_up, kv_a_norm, w_kv_up, mla_q_norm, mla_k_norm, gla_w_gate2, gla_b_gate, gla_out_norm, w_out, mem_attn_norm, mem_norm, mem_w_q, mem_w_k, mem_w_v, mem_w_o, mem_q_norm, mem_k_norm, ffn2_norm, ffn2_w_gate, ffn2_w_up, ffn2_w_down):
    b, s, d = x.shape
    m = b * s
    depth = w_in.shape[0]
    row = lambda v: v.reshape(1, -1)
    bf = lambda w: w.astype(BF16)

    half = jnp.arange(HALF_ROPE, dtype=F32)
    inv_freq = ROPE_THETA ** (-half / HALF_ROPE)
    invf = _pad_rope(jnp.concatenate([inv_freq, inv_freq])).reshape(1, LANES)
    pos = positions.reshape(m, 1)

    xf = x.reshape(m, d)
    for l in range(depth):
        xf = _ffn(xf, row(ffn1_norm[l]), ffn1_w_gate[l], ffn1_w_up[l], ffn1_w_down[l])

        wq = _pad_head(w_q_up[l].reshape(MLA_Q_RANK, MLA_HEADS, MLA_QK)).reshape(MLA_Q_RANK, -1)
        wkv = w_kv_up[l].reshape(MLA_KV_RANK, MLA_HEADS, MLA_NOPE + MLA_V)
        wkv = jnp.concatenate([wkv[..., :MLA_NOPE].reshape(MLA_KV_RANK, -1),
                               wkv[..., MLA_NOPE:].reshape(MLA_KV_RANK, -1)], axis=1)
        w2 = jnp.pad(gla_w_gate2[l], ((HALF_ROPE, LANES - HALF_ROPE - GLA_GATE_RANK), (0, 0)))
        qkv, qkvz, la = _mixer_proj(
            xf, pos, row(mix_norm[l]), *_split_w_in(w_in, l), row(q_a_norm[l]), bf(wq),
            row(kv_a_norm[l]), bf(wkv), row(_pad_head(mla_q_norm[l])), row(_pad_head(mla_k_norm[l])),
            invf, bf(w2), row(gla_b_gate[l]))

        r3 = lambda a: a.reshape(b, s, a.shape[-1])
        o_mla = _mla_attn(r3(qkv))
        o_gla = _gla(r3(qkvz), r3(la), row(gla_out_norm[l]))

        km, vm = _mem_kv(mem, row(mem_norm[l]), bf(mem_w_k[l]), bf(mem_w_v[l]), row(mem_k_norm[l]))
        x3 = _out_mem(r3(xf), o_mla, o_gla, bf(w_out[l]), row(mem_attn_norm[l]), bf(mem_w_q[l]),
                      row(mem_q_norm[l]), km, vm, bf(mem_w_o[l]))

        xf = _ffn(x3.reshape(m, d), row(ffn2_norm[l]), ffn2_w_gate[l], ffn2_w_up[l], ffn2_w_down[l])
    return xf.reshape(b, s, d)
```

```python
import jax
import jax.numpy as jnp
from jax import lax
from jax.experimental import pallas as pl
from jax.experimental.pallas import tpu as pltpu

F32 = jnp.float32
BF16 = jnp.bfloat16

EPS = 1e-6
CHUNK = 64
ROPE_THETA = 10000.0
GLA_TAU = 16.0
LOG2_E = 1.4426950408889634

MLA_HEADS = 8
MLA_NOPE = 128
MLA_ROPE = 64
MLA_QK = MLA_NOPE + MLA_ROPE
MLA_V = 128
MLA_Q_RANK = 512
MLA_KV_RANK = 256
GLA_HEADS = 4
GLA_DK = 128
GLA_DV = 256
GLA_GATE_RANK = 16
MEM_HEADS = 4
MEM_HEAD_DIM = 128

LANES = 128
HEAD_PAD = 2 * LANES
HALF_ROPE = MLA_ROPE // 2
MLA_HEAD_W = 2 * HEAD_PAD + MLA_V
GLA_HEAD_W = 2 * GLA_DK + 2 * GLA_DV

V7X_VMEM_LIMIT = 56 * 1024 * 1024

FFN_TM = 1024
FFN_TF = 512
FFN_DOWN_SPLIT = tuple((c, c + 512) for c in range(0, 2048, 512))
FFN_NORM_CHUNKS = 8
PROJ_TM = 512
ATTN_TQ = 256
GLA_GROUP = 256
OUT_TM = 512
W_IN_ROWS = 256


def _rms(x, g):
    ms = jnp.mean(x * x, axis=-1, keepdims=True)
    return x * lax.rsqrt(ms + EPS) * g


def _dot(a, b):
    return jnp.dot(a, b, preferred_element_type=F32)


def _dot_nt(a, b):
    return lax.dot_general(a, b, (((1,), (1,)), ((), ())), preferred_element_type=F32)


def _resident(shape):
    nd = len(shape)
    return pl.BlockSpec(shape, lambda *_: (0,) * nd, pipeline_mode=pl.Buffered(1))


def _ffn_body(x_hbm, g_ref, wg_ref, wu_ref, wd_ref, o_hbm, xbuf, acc, h_ref, x_sem, o_sem):
    i, f = pl.program_id(0), pl.program_id(1)
    n_i, n_f = pl.num_programs(0), pl.num_programs(1)
    tm, d = acc.shape
    cur = i % 2
    norm_rows = tm // FFN_NORM_CHUNKS

    def rows(tile):
        return pl.ds(pl.multiple_of(tile * tm, tm), tm)

    def x_copy(tile, chunk):
        r0 = pl.multiple_of(chunk * norm_rows, norm_rows)
        return pltpu.make_async_copy(
            x_hbm.at[pl.ds(pl.multiple_of(tile * tm, tm) + r0, norm_rows), :],
            xbuf.at[pl.ds(r0, norm_rows), :], x_sem.at[chunk])

    def o_copy(tile):
        return pltpu.make_async_copy(acc, o_hbm.at[rows(tile), :], o_sem)

    @pl.when((i == 0) & (f == 0))
    def _():
        for c in range(FFN_NORM_CHUNKS):
            x_copy(0, c).start()
        for c in range(FFN_NORM_CHUNKS):
            x_copy(0, c).wait()
        h_ref[0] = _rms(xbuf[...], g_ref[...]).astype(BF16)

    @pl.when((f >= 2) & (f < FFN_NORM_CHUNKS + 2) & (i + 1 < n_i))
    def _():
        x_copy(i + 1, f - 2).wait()

    def body(first):
        h = h_ref[cur]
        last = FFN_NORM_CHUNKS - 1
        chunk = jnp.where(f < 2, last, jnp.minimum(f - 2, last))
        rs = pl.ds(pl.multiple_of(chunk * norm_rows, norm_rows), norm_rows)
        h_ref[1 - cur, rs, :] = _rms(xbuf[rs, :], g_ref[...]).astype(BF16)

        half = FFN_TF // 2
        acts = []
        for k0 in (0, half):
            gate = _dot(h, wg_ref[:, k0:k0 + half].astype(BF16))
            up = _dot(h, wu_ref[:, k0:k0 + half].astype(BF16))
            acts.append(((0.5 * gate) * jax.nn.sigmoid(gate) * up).astype(BF16))
        if first:
            @pl.when(i > 0)
            def _():
                o_copy(i - 1).wait()

        for c0, c1 in FFN_DOWN_SPLIT:
            part = (_dot(acts[0], wd_ref[:half, c0:c1].astype(BF16))
                    + _dot(acts[1], wd_ref[half:, c0:c1].astype(BF16)))
            if first:
                acc[:, c0:c1] = xbuf[:, c0:c1] + part
            else:
                acc[:, c0:c1] += part

    @pl.when(f == 0)
    def _():
        body(True)

    @pl.when(f > 0)
    def _():
        body(False)

    @pl.when((f < FFN_NORM_CHUNKS) & (i + 1 < n_i))
    def _():
        x_copy(i + 1, f).start(priority=1)

    @pl.when(f == n_f - 1)
    def _():
        o_copy(i).start()

        @pl.when(i == n_i - 1)
        def _():
            o_copy(i).wait()


def _ffn(x, g, wg, wu, wd):
    m, d = x.shape
    f = wg.shape[1]
    assert f // FFN_TF >= FFN_NORM_CHUNKS + 2 and FFN_TM % (16 * FFN_NORM_CHUNKS) == 0
    assert FFN_DOWN_SPLIT[-1][1] == d
    return pl.pallas_call(
        _ffn_body,
        grid=(m // FFN_TM, f // FFN_TF),
        in_specs=[
            pl.BlockSpec(memory_space=pl.ANY),
            pl.BlockSpec((1, d), lambda i, j: (0, 0)),
            pl.BlockSpec((d, FFN_TF), lambda i, j: (0, j)),
            pl.BlockSpec((d, FFN_TF), lambda i, j: (0, j)),
            pl.BlockSpec((FFN_TF, d), lambda i, j: (j, 0)),
        ],
        out_specs=pl.BlockSpec(memory_space=pl.ANY),
        out_shape=jax.ShapeDtypeStruct((m, d), F32),
        scratch_shapes=[pltpu.VMEM((FFN_TM, d), F32),
                        pltpu.VMEM((FFN_TM, d), F32),
                        pltpu.VMEM((2, FFN_TM, d), BF16),
                        pltpu.SemaphoreType.DMA((FFN_NORM_CHUNKS,)),
                        pltpu.SemaphoreType.DMA(())],
        compiler_params=pltpu.CompilerParams(
            dimension_semantics=("arbitrary", "arbitrary"),
            vmem_limit_bytes=V7X_VMEM_LIMIT),
        name="ffn",
    )(x, g, wg, wu, wd)


def _mixer_proj_body(x_ref, pos_ref, gmix_ref, w_lat_ref, w_krg_ref, w_gla_ref,
                     gqa_ref, wq_ref, gkva_ref, wkv_ref, gq_ref, gk_ref, invf_ref, w2_ref, b2_ref,
                     mla_out, gla_out, la_out):
    h = _rms(x_ref[...], gmix_ref[...]).astype(BF16)
    col = lax.broadcasted_iota(jnp.int32, (1, GLA_HEAD_W), 1)
    gla_scale = jnp.where(col < GLA_DK, GLA_DK ** -0.5, 1.0)

    def gla_head(hd):
        cs = slice(hd * GLA_HEAD_W, (hd + 1) * GLA_HEAD_W)
        gla_out[:, cs] = (_dot(h, w_gla_ref[:, cs]) * gla_scale).astype(BF16)

    zq = _dot(h, w_lat_ref[:, :MLA_Q_RANK])
    zkv = _dot(h, w_lat_ref[:, MLA_Q_RANK:])
    zkrg = _dot(h, w_krg_ref[...])
    lane = lax.broadcasted_iota(jnp.int32, (1, LANES), 1)
    zkr = jnp.where(lane % (LANES // 2) < HALF_ROPE, zkrg, 0.0)
    gla_head(0)
    gla_head(1)

    pre = _dot(zkrg.astype(BF16), w2_ref[...]) + b2_ref[...]
    log_sig = jnp.minimum(pre, 0.0) - jnp.log(1.0 + jnp.exp(-jnp.abs(pre)))
    la_out[...] = log_sig * (LOG2_E / GLA_TAU)

    ang = pos_ref[...].astype(F32) * invf_ref[...]
    cos_t = jnp.cos(ang)
    sin_t = jnp.sin(ang) * jnp.where(lane < LANES // 2, -1.0, 1.0)

    def rope(v):
        return v * cos_t + pltpu.roll(v, LANES // 2, axis=1) * sin_t

    inv_qk = 1.0 / MLA_QK

    qa = _rms(zq, gqa_ref[...]).astype(BF16)
    kva = _rms(zkv, gkva_ref[...]).astype(BF16)
    qf = _dot(qa, wq_ref[...])
    kvf = _dot(kva, wkv_ref[...])
    gla_head(2)
    for hd in range(MLA_HEADS):
        c = hd * HEAD_PAD
        o = hd * MLA_HEAD_W
        blk = qf[:, c:c + HEAD_PAD]
        ss = jnp.sum(blk * blk, axis=-1, keepdims=True)
        rinv = lax.rsqrt(ss * inv_qk + EPS) * (MLA_QK ** -0.5 * LOG2_E)
        mla_out[:, o:o + LANES] = (blk[:, :LANES] * rinv * gq_ref[:, :LANES]).astype(BF16)
        mla_out[:, o + LANES:o + HEAD_PAD] = rope(
            blk[:, LANES:] * rinv * gq_ref[:, LANES:]).astype(BF16)

    gla_head(3)

    ssr = jnp.sum(zkr * zkr, axis=-1, keepdims=True)
    kr_base = rope(zkr * gk_ref[:, LANES:])
    for hd in range(MLA_HEADS):
        kn = kvf[:, hd * MLA_NOPE:(hd + 1) * MLA_NOPE]
        ss = jnp.sum(kn * kn, axis=-1, keepdims=True) + ssr
        rinv = lax.rsqrt(ss * inv_qk + EPS)
        o = hd * MLA_HEAD_W + HEAD_PAD
        mla_out[:, o:o + LANES] = (kn * rinv * gk_ref[:, :LANES]).astype(BF16)
        mla_out[:, o + LANES:o + HEAD_PAD] = (kr_base * rinv).astype(BF16)
        v0 = (MLA_HEADS + hd) * MLA_NOPE
        mla_out[:, o + HEAD_PAD:o + HEAD_PAD + MLA_V] = kvf[:, v0:v0 + MLA_V].astype(BF16)


def _mixer_proj(x, pos, *weights):
    m, d = x.shape
    tm = PROJ_TM
    row = lambda i: (i, 0)
    widths = [
        (MLA_HEADS * MLA_HEAD_W, BF16),
        (GLA_HEADS * GLA_HEAD_W, BF16),
        (GLA_HEADS * GLA_DK, F32),
    ]
    return pl.pallas_call(
        _mixer_proj_body,
        grid=(m // tm,),
        in_specs=[pl.BlockSpec((tm, d), row), pl.BlockSpec((tm, 1), row)]
        + [_resident(w.shape) for w in weights],
        out_specs=[pl.BlockSpec((tm, w), row) for w, _ in widths],
        out_shape=[jax.ShapeDtypeStruct((m, w), dt) for w, dt in widths],
        compiler_params=pltpu.CompilerParams(
            dimension_semantics=("parallel",), vmem_limit_bytes=V7X_VMEM_LIMIT),
        name="mixer_proj",
    )(x, pos, *weights)


def _mla_attn_body(qkv_ref, o_ref, vext_ref):
    seq = qkv_ref.shape[1]
    dv = MLA_V
    tq = ATTN_TQ
    q_cols, k_cols = slice(0, HEAD_PAD), slice(HEAD_PAD, 2 * HEAD_PAD)
    r = lax.broadcasted_iota(jnp.int32, (tq, tq), 0) // CHUNK
    c = lax.broadcasted_iota(jnp.int32, (tq, tq), 1) // CHUNK
    diag_mask = c <= r

    vext_ref[:, :dv] = qkv_ref[0, :, 2 * HEAD_PAD:]
    vext_ref[:, dv:] = jnp.ones((seq, dv), vext_ref.dtype)

    def scores(i):
        lo = i * tq
        q = qkv_ref[0, lo:lo + tq, q_cols]
        s_d = jnp.where(diag_mask, _dot_nt(q, qkv_ref[0, lo:lo + tq, k_cols]), -jnp.inf)
        s_o = _dot_nt(q, qkv_ref[0, :lo, k_cols]) if i > 0 else None
        m = jnp.max(s_d, axis=-1, keepdims=True)
        if i > 0:
            m = jnp.maximum(m, jnp.max(s_o, axis=-1, keepdims=True))
        return s_d, s_o, m

    order = list(range(seq // tq))[::-1]
    ahead = 2
    pending = [scores(i) for i in order[:ahead]]
    for pos, i in enumerate(order):
        lo = i * tq
        s_d, s_o, m = pending.pop(0)
        if pos + ahead < len(order):
            pending.append(scores(order[pos + ahead]))
        pv = _dot(jnp.exp2(s_d - m).astype(BF16), vext_ref[lo:lo + tq, :])
        if i > 0:
            pv = pv + _dot(jnp.exp2(s_o - m).astype(BF16), vext_ref[:lo, :])
        o_ref[0, lo:lo + tq, :] = (pv[:, :dv] * (1.0 / pv[:, dv:dv + 1])).astype(o_ref.dtype)


def _mla_attn(qkv):
    b, s, _ = qkv.shape
    return pl.pallas_call(
        _mla_attn_body,
        grid=(b, MLA_HEADS),
        in_specs=[pl.BlockSpec((1, s, MLA_HEAD_W), lambda i, j: (i, 0, j))],
        out_specs=pl.BlockSpec((1, s, MLA_V), lambda i, j: (i, 0, j)),
        out_shape=jax.ShapeDtypeStruct((b, s, MLA_HEADS * MLA_V), BF16),
        scratch_shapes=[pltpu.VMEM((s, 2 * MLA_V), BF16)],
        compiler_params=pltpu.CompilerParams(
            dimension_semantics=("parallel", "parallel"), vmem_limit_bytes=V7X_VMEM_LIMIT),
        name="mla_attn",
    )(qkv)


def _gla_body(in_ref, la_ref, g_ref, o_ref, u_ref, st_ref, kd_ref, oraw_ref):
    seq, k_dim, v_dim = in_ref.shape[1], GLA_DK, GLA_DV
    q_cols, k_cols = slice(0, k_dim), slice(k_dim, 2 * k_dim)
    v_cols = slice(2 * k_dim, 2 * k_dim + v_dim)
    zr_cols = slice(2 * k_dim + v_dim, 2 * k_dim + 2 * v_dim)
    n_chunks = seq // CHUNK
    grp = GLA_GROUP
    r = lax.broadcasted_iota(jnp.int32, (grp, grp), 0)
    c = lax.broadcasted_iota(jnp.int32, (grp, grp), 1)
    later = ((c > r) & (c // CHUNK == r // CHUNK)).astype(BF16)

    b_end = jnp.sum(la_ref[0].reshape(n_chunks, CHUNK, k_dim), axis=1)
    dec = jnp.concatenate([jnp.exp2(b_end), jnp.zeros((LANES - n_chunks, k_dim), F32)], axis=0)
    dec_t = dec.T

    for n in range(seq // grp):
        sl = slice(n * grp, (n + 1) * grp)
        g = la_ref[0, sl, :]
        g_hi = g.astype(BF16)
        g_lo = (g - g_hi.astype(F32)).astype(BF16)
        rest2 = _dot(later, jnp.concatenate([g_hi, g_lo], axis=1))
        rest = rest2[:, :k_dim] + rest2[:, k_dim:]
        kd_ref[sl, :] = (in_ref[0, sl, k_cols].astype(F32) * jnp.exp2(rest)).astype(BF16)

    for n in range(n_chunks):
        sl = slice(n * CHUNK, (n + 1) * CHUNK)
        u_ref[n] = lax.dot_general(kd_ref[sl, :], in_ref[0, sl, v_cols], (((0,), (0,)), ((), ())),
                                   preferred_element_type=F32)

    state = jnp.zeros((k_dim, v_dim), F32)
    for n in range(n_chunks):
        state = state * dec_t[:, n:n + 1] + u_ref[n]
        st_ref[n] = state.astype(BF16)

    for n in range(seq // grp):
        for c in range(n * grp // CHUNK, (n + 1) * grp // CHUNK):
            cs = slice(c * CHUNK, (c + 1) * CHUNK)
            oraw_ref[cs, :] = _dot(in_ref[0, cs, q_cols], st_ref[c])
        sl = slice(n * grp, (n + 1) * grp)
        zr = in_ref[0, sl, zr_cols].astype(F32)
        o_ref[0, sl, :] = (_rms(oraw_ref[sl, :], g_ref[...])
                           * (zr * jax.nn.sigmoid(zr))).astype(o_ref.dtype)


def _gla(qkvz, la, g):
    b, s, _ = qkvz.shape
    n_chunks = s // CHUNK
    head = lambda w: pl.BlockSpec((1, s, w), lambda i, j: (i, 0, j))
    return pl.pallas_call(
        _gla_body,
        grid=(b, GLA_HEADS),
        in_specs=[head(GLA_HEAD_W), head(GLA_DK),
                  pl.BlockSpec((1, GLA_DV), lambda i, j: (0, 0))],
        out_specs=head(GLA_DV),
        out_shape=jax.ShapeDtypeStruct((b, s, GLA_HEADS * GLA_DV), BF16),
        scratch_shapes=[pltpu.VMEM((n_chunks, GLA_DK, GLA_DV), F32),
                        pltpu.VMEM((n_chunks, GLA_DK, GLA_DV), BF16),
                        pltpu.VMEM((s, GLA_DK), BF16),
                        pltpu.VMEM((s, GLA_DV), F32)],
        compiler_params=pltpu.CompilerParams(
            dimension_semantics=("parallel", "parallel"), vmem_limit_bytes=V7X_VMEM_LIMIT),
        name="gla",
    )(qkvz, la, g)


def _mem_kv_body(m_ref, gn_ref, wk_ref, wv_ref, gk_ref, k_ref, v_ref):
    hm = _rms(m_ref[0], gn_ref[...]).astype(BF16)
    kf = _dot(hm, wk_ref[...])
    for hd in range(MEM_HEADS):
        c = hd * MEM_HEAD_DIM
        k_ref[0, :, c:c + MEM_HEAD_DIM] = _rms(kf[:, c:c + MEM_HEAD_DIM], gk_ref[...]).astype(BF16)
    v_ref[0] = _dot(hm, wv_ref[...]).astype(BF16)


def _mem_kv(mem, gn, wk, wv, gk):
    b, n, d = mem.shape
    w = wk.shape[1]
    ospec = pl.BlockSpec((1, n, w), lambda i: (i, 0, 0))
    return pl.pallas_call(
        _mem_kv_body,
        grid=(b,),
        in_specs=[pl.BlockSpec((1, n, d), lambda i: (i, 0, 0)),
                  _resident(gn.shape), _resident(wk.shape), _resident(wv.shape),
                  _resident(gk.shape)],
        out_specs=[ospec, ospec],
        out_shape=[jax.ShapeDtypeStruct((b, n, w), BF16)] * 2,
        compiler_params=pltpu.CompilerParams(
            dimension_semantics=("parallel",), vmem_limit_bytes=V7X_VMEM_LIMIT),
        name="mem_kv",
    )(mem, gn, wk, wv, gk)


def _out_mem_body(x_ref, om_ref, og_ref, wo_ref, gn_ref, wq_ref, gq_ref, km_ref, vm_ref,
                  wmo_ref, o_ref):
    n_mla = om_ref.shape[2]
    x = x_ref[0] + _dot(om_ref[0], wo_ref[:n_mla, :]) + _dot(og_ref[0], wo_ref[n_mla:, :])
    h = _rms(x, gn_ref[...]).astype(BF16)
    qm = _dot(h, wq_ref[...])
    heads = [slice(hd * MEM_HEAD_DIM, (hd + 1) * MEM_HEAD_DIM) for hd in range(MEM_HEADS)]
    scores = []
    for c in heads:
        qn = (_rms(qm[:, c], gq_ref[...]) * (MEM_HEAD_DIM ** -0.5)).astype(BF16)
        scores.append(_dot_nt(qn, km_ref[0, :, c]))
    outs = []
    for c, s in zip(heads, scores):
        p = jnp.exp(s - jnp.max(s, axis=-1, keepdims=True))
        l = jnp.sum(p, axis=-1, keepdims=True)
        oh = _dot(p.astype(BF16), vm_ref[0, :, c]) * (1.0 / l)
        outs.append(oh.astype(BF16))
    o_ref[0] = x + _dot(jnp.concatenate(outs, axis=1), wmo_ref[...])


def _out_mem(x, om, og, wo, gn, wq, gq, km, vm, wmo):
    b, s, d = x.shape
    tm = OUT_TM
    n_mem, wm = km.shape[1], km.shape[2]
    tok = lambda w: pl.BlockSpec((1, tm, w), lambda i, j: (i, j, 0))
    memspec = pl.BlockSpec((1, n_mem, wm), lambda i, j: (i, 0, 0))
    return pl.pallas_call(
        _out_mem_body,
        grid=(b, s // tm),
        in_specs=[tok(d), tok(om.shape[2]), tok(og.shape[2]),
                  _resident(wo.shape), _resident(gn.shape), _resident(wq.shape),
                  _resident(gq.shape), memspec, memspec, _resident(wmo.shape)],
        out_specs=tok(d),
        out_shape=jax.ShapeDtypeStruct((b, s, d), F32),
        compiler_params=pltpu.CompilerParams(
            dimension_semantics=("parallel", "parallel"), vmem_limit_bytes=V7X_VMEM_LIMIT),
        name="out_mem",
    )(x, om, og, wo, gn, wq, gq, km, vm, wmo)


def _pad_rope(w):
    z = jnp.zeros(w.shape[:-1] + (LANES // 2 - HALF_ROPE,), w.dtype)
    return jnp.concatenate([w[..., :HALF_ROPE], z, w[..., HALF_ROPE:], z], axis=-1)


def _pad_head(w):
    return jnp.concatenate([w[..., :MLA_NOPE], _pad_rope(w[..., MLA_NOPE:])], axis=-1)


def _w_in_columns():
    o = 0
    parts = {}
    for name, size in (("zq", MLA_Q_RANK), ("zkv", MLA_KV_RANK), ("zkr", MLA_ROPE),
                       ("gq", GLA_HEADS * GLA_DK), ("gk", GLA_HEADS * GLA_DK),
                       ("gv", GLA_HEADS * GLA_DV), ("zg", GLA_GATE_RANK),
                       ("zr", GLA_HEADS * GLA_DV)):
        parts[name] = (o, o + size)
        o += size
    return parts


def _split_w_in_body(wt_ref, lat_ref, krg_ref, gla_ref):
    parts = _w_in_columns()
    rows = lambda a, b: wt_ref[parts[a][0]:parts[b][1], :]
    kr, zg = rows("zkr", "zkr"), rows("zg", "zg")
    zeros = lambda n: jnp.zeros((n, wt_ref.shape[1]), F32)
    krg_t = jnp.concatenate(
        [kr[:HALF_ROPE], zg, zeros(LANES // 2 - HALF_ROPE - GLA_GATE_RANK),
         kr[HALF_ROPE:], zeros(LANES // 2 - HALF_ROPE)], axis=0)
    lat_ref[...] = rows("zq", "zkv").T.astype(BF16)
    krg_ref[...] = krg_t.T.astype(BF16)
    for hd in range(GLA_HEADS):
        o = hd * GLA_HEAD_W
        for name, width in (("gq", GLA_DK), ("gk", GLA_DK), ("gv", GLA_DV), ("zr", GLA_DV)):
            r0 = parts[name][0] + hd * width
            gla_ref[:, o:o + width] = wt_ref[r0:r0 + width, :].T.astype(BF16)
            o += width


def _split_w_in(w_in, layer):
    wt = jnp.swapaxes(w_in, 1, 2)
    _, n, k = wt.shape
    parts = _w_in_columns()
    widths = [parts["zkv"][1] - parts["zq"][0], LANES, GLA_HEADS * GLA_HEAD_W]
    tr = W_IN_ROWS
    return pl.pallas_call(
        _split_w_in_body,
        grid=(k // tr,),
        in_specs=[pl.BlockSpec((None, n, tr), lambda i: (layer, 0, i))],
        out_specs=[pl.BlockSpec((tr, w), lambda i: (i, 0)) for w in widths],
        out_shape=[jax.ShapeDtypeStruct((k, w), BF16) for w in widths],
        compiler_params=pltpu.CompilerParams(
            dimension_semantics=("parallel",), vmem_limit_bytes=V7X_VMEM_LIMIT),
        name="split_w_in",
    )(wt)


def kernel(x, mem, positions, ffn1_norm, ffn1_w_gate, ffn1_w_up, ffn1_w_down, mix_norm, w_in, q_a_norm, w_q_up, kv_a_norm, w_kv_up, mla_q_norm, mla_k_norm, gla_w_gate2, gla_b_gate, gla_out_norm, w_out, mem_attn_norm, mem_norm, mem_w_q, mem_w_k, mem_w_v, mem_w_o, mem_q_norm, mem_k_norm, ffn2_norm, ffn2_w_gate, ffn2_w_up, ffn2_w_down):
    b, s, d = x.shape
    m = b * s
    depth = w_in.shape[0]
    row = lambda v: v.reshape(1, -1)
    bf = lambda w: w.astype(BF16)

    half = jnp.arange(HALF_ROPE, dtype=F32)
    inv_freq = ROPE_THETA ** (-half / HALF_ROPE)
    invf = _pad_rope(jnp.concatenate([inv_freq, inv_freq])).reshape(1, LANES)
    pos = positions.reshape(m, 1)

    xf = x.reshape(m, d)
    for l in range(depth):
        xf = _ffn(xf, row(ffn1_norm[l]), ffn1_w_gate[l], ffn1_w_up[l], ffn1_w_down[l])

        wq = _pad_head(w_q_up[l].reshape(MLA_Q_RANK, MLA_HEADS, MLA_QK)).reshape(MLA_Q_RANK, -1)
        wkv = w_kv_up[l].reshape(MLA_KV_RANK, MLA_HEADS, MLA_NOPE + MLA_V)
        wkv = jnp.concatenate([wkv[..., :MLA_NOPE].reshape(MLA_KV_RANK, -1),
                               wkv[..., MLA_NOPE:].reshape(MLA_KV_RANK, -1)], axis=1)
        w2 = jnp.pad(gla_w_gate2[l], ((HALF_ROPE, LANES - HALF_ROPE - GLA_GATE_RANK), (0, 0)))
        qkv, qkvz, la = _mixer_proj(
            xf, pos, row(mix_norm[l]), *_split_w_in(w_in, l), row(q_a_norm[l]), bf(wq),
            row(kv_a_norm[l]), bf(wkv), row(_pad_head(mla_q_norm[l])), row(_pad_head(mla_k_norm[l])),
            invf, bf(w2), row(gla_b_gate[l]))

        r3 = lambda a: a.reshape(b, s, a.shape[-1])
        o_mla = _mla_attn(r3(qkv))
        o_gla = _gla(r3(qkvz), r3(la), row(gla_out_norm[l]))

        km, vm = _mem_kv(mem, row(mem_norm[l]), bf(mem_w_k[l]), bf(mem_w_v[l]), row(mem_k_norm[l]))
        x3 = _out_mem(r3(xf), o_mla, o_gla, bf(w_out[l]), row(mem_attn_norm[l]), bf(mem_w_q[l]),
                      row(mem_q_norm[l]), km, vm, bf(mem_w_o[l]))

        xf = _ffn(x3.reshape(m, d), row(ffn2_norm[l]), ffn2_w_gate[l], ffn2_w_up[l], ffn2_w_down[l])
    return xf.reshape(b, s, d)
```
